```python
import jax, jax.numpy as jnp
from jax import lax
import numpy as np

D_MODEL = 4096
BATCH = 2
SEQ = 4096
DEPTH = 1

EPS = 1e-6
ROPE_THETA = 500000.0
ROPE_FRACTION = 4
D_FF = 11008
N_MOD = 9
Q_BLOCK = 128

NSA_HEADS = 16
NSA_GROUPS = 2
NSA_HPG = NSA_HEADS // NSA_GROUPS
NSA_DK = 192
NSA_DV = 128
CMP_BLOCK = 32
CMP_STRIDE = 16
SEL_BLOCK = 64
SEL_TOP = 16
WINDOW = 512

DSA_HEADS = 16
DSA_KV_HEADS = 4
DSA_HPG = DSA_HEADS // DSA_KV_HEADS
DSA_DK = 128
DSA_DV = 128
IDX_HEADS = 32
IDX_DIM = 128
DSA_TOPK = 256

SPLITS = (
    NSA_HEADS * NSA_DK,
    NSA_GROUPS * (NSA_DK + NSA_DV),
    NSA_GROUPS * (NSA_DK + NSA_DV),
    NSA_GROUPS * (NSA_DK + NSA_DV),
    3 * NSA_HEADS,
    DSA_HEADS * DSA_DK,
    DSA_KV_HEADS * (DSA_DK + DSA_DV),
    IDX_HEADS * IDX_DIM,
    IDX_DIM,
    IDX_HEADS,
    2 * D_MODEL,
)
D_IN = sum(SPLITS)

kernel_name = 'hybrid_nsa_dsa_macaron_adaln'


def rms_norm(x, g):
    xf = x.astype(jnp.float32)
    y = xf * lax.rsqrt(jnp.mean(xf * xf, axis=-1, keepdims=True) + EPS)
    return y.astype(x.dtype) * g


def modulate(h, shift, scale):
    return h * (1 + scale[:, None, :]) + shift[:, None, :]


def swiglu(h, w1, w3, w2):
    return (jax.nn.silu(h @ w1) * (h @ w3)) @ w2


def partial_rope(x, pos):
    d = x.shape[-1]
    r = d // ROPE_FRACTION
    half = r // 2
    inv = ROPE_THETA ** (-jnp.arange(0, r, 2, dtype=jnp.float32) / r)
    ang = pos.astype(jnp.float32)[:, None] * inv[None, :]
    cos = jnp.cos(ang)[:, None, :]
    sin = jnp.sin(ang)[:, None, :]
    x1 = x[..., :half].astype(jnp.float32)
    x2 = x[..., half:r].astype(jnp.float32)
    rot = jnp.concatenate([x1 * cos - x2 * sin, x1 * sin + x2 * cos], axis=-1).astype(x.dtype)
    return jnp.concatenate([rot, x[..., r:]], axis=-1)


def masked_softmax(logits, mask):
    z = jnp.where(mask, logits.astype(jnp.float32), -jnp.inf)
    m = jnp.max(z, axis=-1, keepdims=True)
    m = jnp.where(jnp.isfinite(m), m, 0.0)
    e = jnp.exp(z - m)
    return e / jnp.maximum(jnp.sum(e, axis=-1, keepdims=True), jnp.finfo(jnp.float32).tiny)


def compress_blocks(kv, pe, w1, w2):
    b, s, g, d = kv.shape
    n_cmp = (s - CMP_BLOCK) // CMP_STRIDE + 1
    idx = jnp.arange(n_cmp)[:, None] * CMP_STRIDE + jnp.arange(CMP_BLOCK)[None, :]
    blk = kv[:, idx] + pe[None, None, :, None, :]
    blk = jnp.transpose(blk, (0, 1, 3, 2, 4)).reshape(b, n_cmp, g, CMP_BLOCK * d)
    return jax.nn.gelu(blk @ w1) @ w2


def cmp_to_sel_matrix(n_cmp, n_blk):
    r = SEL_BLOCK // CMP_STRIDE
    c = CMP_BLOCK // CMP_STRIDE
    i = jnp.arange(n_cmp)[:, None, None, None]
    j = jnp.arange(n_blk)[None, :, None, None]
    m = jnp.arange(r)[None, None, :, None]
    n = jnp.arange(c)[None, None, None, :]
    return jnp.sum(i == r * j + m - n, axis=(2, 3)).astype(jnp.float32)


def mixer(h, w_in, cmp_pe_k, cmp_w1_k, cmp_w2_k, cmp_pe_v, cmp_w1_v, cmp_w2_v, proj_a, proj_b, w_out):
    b, s, _ = h.shape
    pos = jnp.arange(s)
    n_qb = s // Q_BLOCK
    parts = jnp.split(h @ w_in, np.cumsum(SPLITS)[:-1].tolist(), axis=-1)
    q_a, kv_c, kv_s, kv_w, gate_a, q_b, kv_b, q_i, k_i, w_i, merge = parts

    def kv_split(t, n, dk, dv):
        t = t.reshape(b, s, n, dk + dv)
        return t[..., :dk], t[..., dk:]

    q_a = partial_rope(q_a.reshape(b, s, NSA_HEADS, NSA_DK), pos)
    k_c, v_c = kv_split(kv_c, NSA_GROUPS, NSA_DK, NSA_DV)
    k_cmp = compress_blocks(k_c, cmp_pe_k, cmp_w1_k, cmp_w2_k)
    v_cmp = compress_blocks(v_c, cmp_pe_v, cmp_w1_v, cmp_w2_v)
    n_cmp = k_cmp.shape[1]
    cmp_end = jnp.arange(n_cmp) * CMP_STRIDE + CMP_BLOCK - 1
    k_cmp = partial_rope(k_cmp, cmp_end)
    n_blk = s // SEL_BLOCK
    n_sel = min(SEL_TOP, n_blk)
    agg = cmp_to_sel_matrix(n_cmp, n_blk)
    k_s, v_s = kv_split(kv_s, NSA_GROUPS, NSA_DK, NSA_DV)
    k_s = partial_rope(k_s, pos)
    k_sel = k_s.reshape(b, n_blk, SEL_BLOCK, NSA_GROUPS, NSA_DK).transpose(0, 3, 1, 2, 4)
    v_sel = v_s.reshape(b, n_blk, SEL_BLOCK, NSA_GROUPS, NSA_DV).transpose(0, 3, 1, 2, 4)
    k_w, v_w = kv_split(kv_w, NSA_GROUPS, NSA_DK, NSA_DV)
    k_w = partial_rope(k_w, pos)
    pad = ((0, 0), (WINDOW, 0), (0, 0), (0, 0))
    k_wp = jnp.pad(k_w, pad)
    v_wp = jnp.pad(v_w, pad)
    g_a = jax.nn.sigmoid(gate_a).reshape(b, s, NSA_GROUPS, NSA_HPG, 3)

    q_b = partial_rope(q_b.reshape(b, s, DSA_HEADS, DSA_DK), pos)
    k_b, v_b = kv_split(kv_b, DSA_KV_HEADS, DSA_DK, DSA_DV)
    k_b = partial_rope(k_b, pos)
    q_i = partial_rope(q_i.reshape(b, s, IDX_HEADS, IDX_DIM), pos)
    k_i = partial_rope(k_i.reshape(b, s, 1, IDX_DIM), pos)[:, :, 0]
    w_i = w_i * IDX_HEADS ** -0.5
    k_top = min(DSA_TOPK, s // 4)

    b_ix = jnp.arange(b)
    g_ix = jnp.arange(NSA_GROUPS)
    scale_a = NSA_DK ** -0.5
    scale_b = DSA_DK ** -0.5

    def to_blocks(t):
        return jnp.moveaxis(t.reshape(b, n_qb, Q_BLOCK, *t.shape[2:]), 1, 0)

    def block(args):
        qa, ga, qb, qi, wi, q0 = args
        t = q0 + jnp.arange(Q_BLOCK)
        qa = qa.reshape(b, Q_BLOCK, NSA_GROUPS, NSA_HPG, NSA_DK)
        s_c = jnp.einsum('bqghd,bcgd->bghqc', qa, k_cmp) * scale_a
        p_c = masked_softmax(s_c, cmp_end[None, :] <= t[:, None])
        o_c = jnp.einsum('bghqc,bcgd->bqghd', p_c.astype(v_cmp.dtype), v_cmp)
        imp = jnp.einsum('bghqc,cn->bgqn', p_c, agg)
        blk_id = jnp.arange(n_blk)[None, :]
        cur = (t // SEL_BLOCK)[:, None]
        forced = (blk_id == 0) | (blk_id == cur) | (blk_id == cur - 1)
        imp = jnp.where(forced, jnp.inf, jnp.where(blk_id * SEL_BLOCK <= t[:, None], imp, -jnp.inf))
        top_v, top_i = lax.top_k(imp, n_sel)
        gi = (b_ix[:, None, None, None], g_ix[None, :, None, None], top_i)
        k_g = k_sel[gi].reshape(b, NSA_GROUPS, Q_BLOCK, n_sel * SEL_BLOCK, NSA_DK)
        v_g = v_sel[gi].reshape(b, NSA_GROUPS, Q_BLOCK, n_sel * SEL_BLOCK, NSA_DV)
        key_pos = (top_i[..., None] * SEL_BLOCK + jnp.arange(SEL_BLOCK)).reshape(b, NSA_GROUPS, Q_BLOCK, n_sel * SEL_BLOCK)
        ok_s = jnp.repeat(top_v > -jnp.inf, SEL_BLOCK, axis=-1) & (key_pos <= t[:, None])
        s_s = jnp.einsum('bqghd,bgqkd->bghqk', qa, k_g) * scale_a
        p_s = masked_softmax(s_s, ok_s[:, :, None])
        o_s = jnp.einsum('bghqk,bgqkd->bqghd', p_s.astype(v_g.dtype), v_g)
        k_wb = lax.dynamic_slice_in_dim(k_wp, q0, WINDOW + Q_BLOCK, axis=1)
        v_wb = lax.dynamic_slice_in_dim(v_wp, q0, WINDOW + Q_BLOCK, axis=1)
        s_pos = q0 - WINDOW + jnp.arange(WINDOW + Q_BLOCK)
        diff = t[:, None] - s_pos[None, :]
        mask_w = (diff >= 0) & (diff < WINDOW) & (s_pos[None, :] >= 0)
        s_w = jnp.einsum('bqghd,bkgd->bghqk', qa, k_wb) * scale_a
        p_w = masked_softmax(s_w, mask_w)
        o_w = jnp.einsum('bghqk,bkgd->bqghd', p_w.astype(v_wb.dtype), v_wb)
        o_a = (ga[..., 0:1] * o_c + ga[..., 1:2] * o_s + ga[..., 2:3] * o_w).reshape(b, Q_BLOCK, NSA_HEADS * NSA_DV)
        rel = jax.nn.relu(jnp.einsum('bqhd,bsd->bqhs', qi, k_i) * IDX_DIM ** -0.5)
        score = jnp.einsum('bqhs,bqh->bqs', rel, wi).astype(jnp.float32)
        score = jnp.where(pos[None, :] <= t[:, None], score, -jnp.inf)
        top_vb, top_ib = lax.top_k(score, k_top)
        k_gb = k_b[b_ix[:, None, None], top_ib]
        v_gb = v_b[b_ix[:, None, None], top_ib]
        qb = qb.reshape(b, Q_BLOCK, DSA_KV_HEADS, DSA_HPG, DSA_DK)
        s_b = jnp.einsum('bqghd,bqkgd->bghqk', qb, k_gb) * scale_b
        p_b = masked_softmax(s_b, (top_vb > -jnp.inf)[:, None, None])
        o_b = jnp.einsum('bghqk,bqkgd->bqghd', p_b.astype(v_gb.dtype), v_gb).reshape(b, Q_BLOCK, DSA_HEADS * DSA_DV)
        return o_a, o_b

    xs = (to_blocks(q_a), to_blocks(g_a), to_blocks(q_b), to_blocks(q_i), to_blocks(w_i),
          jnp.arange(n_qb) * Q_BLOCK)
    o_a, o_b = lax.map(block, xs)
    o_a = jnp.moveaxis(o_a, 0, 1).reshape(b, s, NSA_HEADS * NSA_DV)
    o_b = jnp.moveaxis(o_b, 0, 1).reshape(b, s, DSA_HEADS * DSA_DV)
    gm_a, gm_b = jnp.split(jax.nn.sigmoid(merge), 2, axis=-1)
    y = gm_a * (o_a @ proj_a) + gm_b * (o_b @ proj_b)
    return y @ w_out


def setup_inputs(seed: int = 0) -> dict:
    key = jax.random.key(seed)
    ks = iter(jax.random.split(key, 32))
    L, D = DEPTH, D_MODEL
    f32 = jnp.float32

    def w(shape, fan_in, g=1.0):
        return g * fan_in ** -0.5 * jax.random.normal(next(ks), shape, f32)

    def gain(shape):
        return 1.0 + 0.02 * jax.random.normal(next(ks), shape, f32)

    def small(shape, sc):
        return sc * jax.random.normal(next(ks), shape, f32)

    return {
        'x': jax.random.normal(next(ks), (BATCH, SEQ, D), f32),
        'c': jax.random.normal(next(ks), (BATCH, D), f32),
        'ada_w': w((L, D, N_MOD * D), D, 0.5),
        'ada_b': small((L, N_MOD * D), 0.02),
        'norm_ffn1': gain((L, D)),
        'ffn1_w1': w((L, D, D_FF), D),
        'ffn1_w3': w((L, D, D_FF), D),
        'ffn1_w2': w((L, D_FF, D), D_FF),
        'norm_mix': gain((L, D)),
        'w_in': w((L, D, D_IN), D),
        'cmp_pe_k': small((L, CMP_BLOCK, NSA_DK), 0.1),
        'cmp_w1_k': w((L, CMP_BLOCK * NSA_DK, NSA_DK), CMP_BLOCK * NSA_DK),
        'cmp_w2_k': w((L, NSA_DK, NSA_DK), NSA_DK),
        'cmp_pe_v': small((L, CMP_BLOCK, NSA_DV), 0.1),
        'cmp_w1_v': w((L, CMP_BLOCK * NSA_DV, NSA_DV), CMP_BLOCK * NSA_DV),
        'cmp_w2_v': w((L, NSA_DV, NSA_DV), NSA_DV),
        'proj_a': w((L, NSA_HEADS * NSA_DV, D), NSA_HEADS * NSA_DV),
        'proj_b': w((L, DSA_HEADS * DSA_DV, D), DSA_HEADS * DSA_DV),
        'w_out': w((L, D, D), D),
        'norm_ffn2': gain((L, D)),
        'ffn2_w1': w((L, D, D_FF), D),
        'ffn2_w3': w((L, D, D_FF), D),
        'ffn2_w2': w((L, D_FF, D), D_FF),
        'norm_final': gain((D,)),
    }


def reference(x, c, ada_w, ada_b, norm_ffn1, ffn1_w1, ffn1_w3, ffn1_w2, norm_mix, w_in,
              cmp_pe_k, cmp_w1_k, cmp_w2_k, cmp_pe_v, cmp_w1_v, cmp_w2_v, proj_a, proj_b, w_out,
              norm_ffn2, ffn2_w1, ffn2_w3, ffn2_w2, norm_final):
    c_act = jax.nn.silu(c)
    for l in range(DEPTH):
        mod = c_act @ ada_w[l] + ada_b[l]
        sh1, sc1, g1, shm, scm, gm, sh2, sc2, g2 = jnp.split(mod, N_MOD, axis=-1)
        h = modulate(rms_norm(x, norm_ffn1[l]), sh1, sc1)
        x = x + 0.5 * g1[:, None, :] * swiglu(h, ffn1_w1[l], ffn1_w3[l], ffn1_w2[l])
        h = modulate(rms_norm(x, norm_mix[l]), shm, scm)
        x = x + gm[:, None, :] * mixer(h, w_in[l], cmp_pe_k[l], cmp_w1_k[l], cmp_w2_k[l],
                                       cmp_pe_v[l], cmp_w1_v[l], cmp_w2_v[l], proj_a[l], proj_b[l], w_out[l])
        h = modulate(rms_norm(x, norm_ffn2[l]), sh2, sc2)
        x = x + 0.5 * g2[:, None, :] * swiglu(h, ffn2_w1[l], ffn2_w3[l], ffn2_w2[l])
    return rms_norm(x, norm_final)
```

```python
import functools
import math
from typing import NamedTuple

import numpy as np
import jax
import jax.numpy as jnp
from jax import lax
from jax.experimental import pallas as pl
from jax.experimental.pallas import tpu as pltpu

F32 = jnp.float32
BF16 = jnp.bfloat16

V7X_LANES = 128
V7X_VMEM_BYTES = 64 * 1024 * 1024
V7X_VMEM_BUDGET = 56 * 1024 * 1024

EPS = 1e-6
ROPE_THETA = 500000.0
ROPE_FRACTION = 4
N_MOD = 9
Q_BLOCK = 128

NSA_HEADS = 16
NSA_GROUPS = 2
NSA_HPG = NSA_HEADS // NSA_GROUPS
NSA_DK = 192
NSA_DKP = 256
NSA_DV = 128
CMP_BLOCK = 32
CMP_STRIDE = 16
SEL_BLOCK = 64
SEL_TOP = 16
WINDOW = 512

DSA_HEADS = 16
DSA_KV_HEADS = 4
DSA_HPG = DSA_HEADS // DSA_KV_HEADS
DSA_DK = 128
DSA_DV = 128
IDX_HEADS = 32
IDX_DIM = 128
DSA_TOPK = 256

GATE_COLS = 3 * NSA_HEADS
INT_MIN = np.int32(-(2**31))
NEG_INF = float("-inf")
TINY = float(np.finfo(np.float32).tiny)
DN_T = (((1,), (1,)), ((), ()))


class Cfg(NamedTuple):
    batch: int
    seq: int
    d_model: int
    d_ff: int


def _tile(n, pref):
    if n <= pref:
        return n
    t = (pref // V7X_LANES) * V7X_LANES
    while t > V7X_LANES and n % t:
        t -= V7X_LANES
    assert n % t == 0, (n, pref)
    return t


def _nbytes(shape, dtype):
    return int(np.prod(shape)) * jnp.dtype(dtype).itemsize


def _params(dims, est_bytes):
    limit = int(min(V7X_VMEM_BUDGET, max(est_bytes, 16 * 1024 * 1024)))
    return pltpu.CompilerParams(dimension_semantics=dims, vmem_limit_bytes=limit)


def _adaln_kernel(c_ref, w_ref, b_ref, o_ref, acc_ref, *, nk):
    k = pl.program_id(1)

    @pl.when(k == 0)
    def _():
        acc_ref[...] = jnp.zeros_like(acc_ref)

    c = c_ref[...]
    c_act = (c * jax.nn.sigmoid(c)).astype(BF16)
    acc_ref[...] += jnp.dot(c_act, w_ref[...].astype(BF16), preferred_element_type=F32)

    @pl.when(k == nk - 1)
    def _():
        o_ref[...] = acc_ref[...] + b_ref[...]


def _adaln(c_pad, w, b):
    m, kdim = c_pad.shape
    n = w.shape[1]
    tn, tk = _tile(n, 2048), _tile(kdim, 1024)
    nk = kdim // tk
    est = 2 * _nbytes((tk, tn), F32) + _nbytes((tk, tn), BF16) + 4 * _nbytes((m, tn), F32) + (2 << 20)
    return pl.pallas_call(
        functools.partial(_adaln_kernel, nk=nk),
        grid=(n // tn, nk),
        in_specs=[
            pl.BlockSpec((m, tk), lambda j, k: (0, k)),
            pl.BlockSpec((tk, tn), lambda j, k: (k, j)),
            pl.BlockSpec((1, tn), lambda j, k: (0, j)),
        ],
        out_specs=pl.BlockSpec((m, tn), lambda j, k: (0, j)),
        out_shape=jax.ShapeDtypeStruct((m, n), F32),
        scratch_shapes=[pltpu.VMEM((m, tn), F32)],
        compiler_params=_params(("parallel", "arbitrary"), est),
        name="adaln",
    )(c_pad, w, b)


def _norm_mod_kernel(x_ref, g_ref, sh_ref, sc_ref, o_ref):
    x = x_ref[...]
    y = x * lax.rsqrt(jnp.mean(x * x, axis=-1, keepdims=True) + EPS)
    y = y * g_ref[...]
    o_ref[...] = (y * (1.0 + sc_ref[0]) + sh_ref[0]).astype(o_ref.dtype)


def _norm_mod(x2, gain, mod3, shift_idx, scale_idx, cfg):
    t, d = x2.shape
    tm = _tile(cfg.seq, 256)
    nsb = cfg.seq // tm
    est = 4 * _nbytes((tm, d), F32) + 2 * _nbytes((tm, d), BF16) + (2 << 20)
    return pl.pallas_call(
        _norm_mod_kernel,
        grid=(t // tm,),
        in_specs=[
            pl.BlockSpec((tm, d), lambda i: (i, 0)),
            pl.BlockSpec((1, d), lambda i: (0, 0)),
            pl.BlockSpec((1, 1, d), lambda i: ((i // nsb) * N_MOD + shift_idx, 0, 0)),
            pl.BlockSpec((1, 1, d), lambda i: ((i // nsb) * N_MOD + scale_idx, 0, 0)),
        ],
        out_specs=pl.BlockSpec((tm, d), lambda i: (i, 0)),
        out_shape=jax.ShapeDtypeStruct((t, d), BF16),
        compiler_params=_params(("parallel",), est),
        name="norm_mod",
    )(x2, gain.reshape(1, d), mod3, mod3)


def _final_norm_kernel(x_ref, g_ref, o_ref):
    x = x_ref[...]
    y = x * lax.rsqrt(jnp.mean(x * x, axis=-1, keepdims=True) + EPS)
    o_ref[...] = y * g_ref[...]


def _final_norm(x2, gain):
    t, d = x2.shape
    tm = _tile(t, 256)
    est = 6 * _nbytes((tm, d), F32) + (2 << 20)
    return pl.pallas_call(
        _final_norm_kernel,
        grid=(t // tm,),
        in_specs=[pl.BlockSpec((tm, d), lambda i: (i, 0)), pl.BlockSpec((1, d), lambda i: (0, 0))],
        out_specs=pl.BlockSpec((tm, d), lambda i: (i, 0)),
        out_shape=jax.ShapeDtypeStruct((t, d), F32),
        compiler_params=_params(("parallel",), est),
        name="final_norm",
    )(x2, gain.reshape(1, d))


def _mm_kernel(*refs, n_a, n_w, pairs, n_epi, epilogue, nk):
    a_refs = refs[:n_a]
    w_refs = refs[n_a:n_a + n_w]
    e_refs = refs[n_a + n_w:n_a + n_w + n_epi]
    o_ref = refs[n_a + n_w + n_epi]
    acc_refs = refs[n_a + n_w + n_epi + 1:]

    def prods():
        return [jnp.dot(a_refs[ai][...], w_refs[wi][...], preferred_element_type=F32) for ai, wi in pairs]

    if nk == 1:
        o_ref[...] = epilogue(prods(), e_refs).astype(o_ref.dtype)
        return

    k = pl.program_id(2)

    @pl.when(k == 0)
    def _():
        for acc in acc_refs:
            acc[...] = jnp.zeros_like(acc)

    for acc, p in zip(acc_refs, prods()):
        acc[...] += p

    @pl.when(k == nk - 1)
    def _():
        o_ref[...] = epilogue([acc[...] for acc in acc_refs], e_refs).astype(o_ref.dtype)


def _matmul(a_list, w_list, pairs, epilogue, epi_inputs, n_out, out_dtype, *, tm, tn, tk, name):
    m, kdim = a_list[0].shape
    nk = kdim // tk
    assert m % tm == 0 and n_out % tn == 0 and kdim % tk == 0
    in_specs, est = [], 0
    for a in a_list:
        in_specs.append(pl.BlockSpec((tm, tk), lambda i, j, k: (i, k)))
        est += 2 * _nbytes((tm, tk), a.dtype)
    for w in w_list:
        in_specs.append(pl.BlockSpec((tk, tn), lambda i, j, k: (k, j)))
        est += 2 * _nbytes((tk, tn), w.dtype)
    for arr, bshape, imap in epi_inputs:
        in_specs.append(pl.BlockSpec(bshape, lambda i, j, k, imap=imap: imap(i, j)))
        est += 2 * _nbytes(bshape, arr.dtype)
    est += 2 * _nbytes((tm, tn), out_dtype) + (len(pairs) + 3) * _nbytes((tm, tn), F32) + (2 << 20)
    scratch = [] if nk == 1 else [pltpu.VMEM((tm, tn), F32) for _ in pairs]
    kern = functools.partial(_mm_kernel, n_a=len(a_list), n_w=len(w_list), pairs=tuple(pairs),
                             n_epi=len(epi_inputs), epilogue=epilogue, nk=nk)
    return pl.pallas_call(
        kern,
        grid=(m // tm, n_out // tn, nk),
        in_specs=in_specs,
        out_specs=pl.BlockSpec((tm, tn), lambda i, j, k: (i, j)),
        out_shape=jax.ShapeDtypeStruct((m, n_out), out_dtype),
        scratch_shapes=scratch,
        compiler_params=_params(("parallel", "parallel", "arbitrary"), est),
        name=name,
    )(*a_list, *w_list, *[e[0] for e in epi_inputs])


def _epi_plain(accs, e_refs):
    return accs[0]


def _epi_sigmoid(accs, e_refs):
    return jax.nn.sigmoid(accs[0])


def _epi_swiglu(accs, e_refs):
    return jax.nn.silu(accs[0]) * accs[1]


def _epi_merge(accs, e_refs):
    return e_refs[0][...] * accs[0] + e_refs[1][...] * accs[1]


def _epi_residual(accs, e_refs, *, coef):
    return e_refs[0][...] + (coef * e_refs[1][0]) * accs[0]


def _rope_apply(a, cos, sin_lo, sin_hi, half):
    width = a.shape[-1]
    return a * cos + pltpu.roll(a, width - half, 1) * sin_lo + pltpu.roll(a, half, 1) * sin_hi


def _epi_rope(accs, e_refs, *, half):
    out = _rope_apply(accs[0], e_refs[0][...], e_refs[1][...], e_refs[2][...], half)
    return out * e_refs[3][...]


def _rope_tables(pos, dk, head_w):
    r = dk // ROPE_FRACTION
    half = r // 2
    n = pos.shape[0]
    inv = ROPE_THETA ** (-jnp.arange(0, r, 2, dtype=F32) / r)
    ang = pos.astype(F32)[:, None] * inv[None, :]
    cos, sin = jnp.cos(ang), jnp.sin(ang)
    cos_t = jnp.concatenate([cos, cos, jnp.ones((n, head_w - r), F32)], axis=1)
    sin_lo = jnp.concatenate([-sin, jnp.zeros((n, head_w - half), F32)], axis=1)
    sin_hi = jnp.concatenate([jnp.zeros((n, half), F32), sin, jnp.zeros((n, head_w - r), F32)], axis=1)
    return cos_t, sin_lo, sin_hi, half


def _project_rope(h, w, colscale, dk, head_w, cfg, name):
    n = w.shape[1]
    tm = _tile(cfg.seq, 512)
    tn = _tile(n, 1024)
    nsb = cfg.seq // tm
    cos_t, sin_lo, sin_hi, half = _rope_tables(jnp.arange(cfg.seq), dk, head_w)
    reps = tn // head_w
    tabs = [jnp.tile(tb, (1, reps)) for tb in (cos_t, sin_lo, sin_hi)]
    epi = [(tb, (tm, tn), lambda i, j: (i % nsb, 0)) for tb in tabs]
    epi.append((colscale.reshape(1, n), (1, tn), lambda i, j: (0, j)))
    return _matmul([h], [w], [(0, 0)], functools.partial(_epi_rope, half=half), epi, n, BF16,
                   tm=tm, tn=tn, tk=h.shape[1], name=name)


def _project(h, w, epilogue, out_dtype, cfg, name):
    n = w.shape[1]
    return _matmul([h], [w], [(0, 0)], epilogue, [], n, out_dtype,
                   tm=_tile(cfg.seq, 512), tn=_tile(n, 1024), tk=h.shape[1], name=name)


def _gelu_tanh(x):
    return 0.5 * x * (1.0 + jnp.tanh(math.sqrt(2.0 / math.pi) * (x + 0.044715 * (x * x * x))))


def _compress_kernel(x_ref, pe_ref, w1_ref, w2_ref, *rest, half):
    o_ref = rest[-1]
    x = x_ref[0]
    n_rows, width = x.shape
    a_lo = (x + pe_ref[0:1, :]).astype(BF16)
    a_hi = (x + pe_ref[1:2, :]).astype(BF16)
    p = jnp.dot(a_lo, w1_ref[0:width, :], preferred_element_type=F32)
    q = jnp.dot(a_hi, w1_ref[width:2 * width, :], preferred_element_type=F32)
    hid = _gelu_tanh(p + pltpu.roll(q, n_rows - 1, 0))
    y = jnp.dot(hid.astype(BF16), w2_ref[...], preferred_element_type=F32)
    if half:
        y = _rope_apply(y, rest[0][...], rest[1][...], rest[2][...], half)
    o_ref[0] = y.astype(o_ref.dtype)


def _compress(x_chunks, pe, w1, w2, d, d_out, rope_pos, name):
    bg, n_rows, width = x_chunks.shape
    hid_w = ((d + V7X_LANES - 1) // V7X_LANES) * V7X_LANES
    pe2 = pe.reshape(2, width)
    w1p = jnp.pad(w1.astype(BF16), ((0, 0), (0, hid_w - d)))
    w2p = jnp.pad(w2.astype(BF16), ((0, hid_w - d), (0, d_out - d)))
    ins = [x_chunks, pe2, w1p, w2p]
    in_specs = [
        pl.BlockSpec((1, n_rows, width), lambda i: (i, 0, 0)),
        pl.BlockSpec((2, width), lambda i: (0, 0)),
        pl.BlockSpec((2 * width, hid_w), lambda i: (0, 0)),
        pl.BlockSpec((hid_w, d_out), lambda i: (0, 0)),
    ]
    half = 0
    if rope_pos is not None:
        cos_t, sin_lo, sin_hi, half = _rope_tables(rope_pos, d, d_out)
        ins += [cos_t, sin_lo, sin_hi]
        in_specs += [pl.BlockSpec((n_rows, d_out), lambda i: (0, 0))] * 3
    est = (2 * _nbytes((n_rows, width), F32) + 2 * _nbytes((n_rows, width), BF16)
           + 2 * _nbytes((2 * width, hid_w), BF16) + (8 << 20))
    return pl.pallas_call(
        functools.partial(_compress_kernel, half=half),
        grid=(bg,),
        in_specs=in_specs,
        out_specs=pl.BlockSpec((1, n_rows, d_out), lambda i: (i, 0, 0)),
        out_shape=jax.ShapeDtypeStruct((bg, n_rows, d_out), BF16),
        compiler_params=_params(("parallel",), est),
        name=name,
    )(*ins)


def _stack_heads(ref, first, count, width):
    return jnp.concatenate([ref[:, (first + h) * width:(first + h + 1) * width] for h in range(count)], axis=0)


def _softmax_masked(s3, mask2):
    z = jnp.where(mask2[None], s3, NEG_INF)
    m = jnp.max(z, axis=-1, keepdims=True)
    m = jnp.where(m > NEG_INF, m, 0.0)
    e = jnp.exp(z - m)
    return e / jnp.maximum(jnp.sum(e, axis=-1, keepdims=True), TINY)


def _flash_init(m_ref, l_ref, acc_ref):
    m_ref[...] = jnp.full(m_ref.shape, NEG_INF, F32)
    l_ref[...] = jnp.zeros(l_ref.shape, F32)
    acc_ref[...] = jnp.zeros(acc_ref.shape, F32)


def _flash_step(s, mask2, v, heads, m_ref, l_ref, acc_ref):
    rows, kc = s.shape
    z = jnp.where(mask2[None], s.reshape(heads, rows // heads, kc), NEG_INF).reshape(rows, kc)
    m_old = m_ref[...]
    m_new = jnp.maximum(m_old, jnp.max(z, axis=-1, keepdims=True))
    m_safe = jnp.where(m_new > NEG_INF, m_new, 0.0)
    alpha = jnp.exp(m_old - m_safe)
    p = jnp.exp(z - m_safe)
    l_ref[...] = alpha * l_ref[...] + jnp.sum(p, axis=-1, keepdims=True)
    acc_ref[...] = alpha * acc_ref[...] + jnp.dot(p.astype(BF16), v, preferred_element_type=F32)
    m_ref[...] = m_new


def _flash_result(l_ref, acc_ref):
    return acc_ref[...] / jnp.maximum(l_ref[...], TINY)


def _nsa_kernel(q_ref, gate_ref, kc_ref, vc_ref, ks_ref, vs_ref, kw_ref, vw_ref, aggt_ref, o_ref,
                m_ref, l_ref, acc_ref, *, seq, kc):
    n_cmp = (seq - CMP_BLOCK) // CMP_STRIDE + 1
    n_blk = seq // SEL_BLOCK
    n_sel = min(SEL_TOP, n_blk)
    blk_pad = aggt_ref.shape[0]
    cmp_pad = aggt_ref.shape[1]
    q0 = pl.program_id(1) * Q_BLOCK
    t_col = q0 + lax.broadcasted_iota(jnp.int32, (Q_BLOCK, 1), 0)
    t_row = q0 + lax.broadcasted_iota(jnp.int32, (1, Q_BLOCK), 1)
    gates = jax.nn.sigmoid(gate_ref[...])

    cidx = lax.broadcasted_iota(jnp.int32, (Q_BLOCK, cmp_pad), 1)
    mask_c = (cidx * CMP_STRIDE + (CMP_BLOCK - 1) <= t_col) & (cidx < n_cmp)

    blk = lax.broadcasted_iota(jnp.int32, (blk_pad, Q_BLOCK), 0)
    cur = lax.shift_right_logical(t_row, int(math.log2(SEL_BLOCK)))
    forced = (blk == 0) | (blk == cur) | (blk == cur - 1)
    valid = (blk * SEL_BLOCK <= t_row) & (blk < n_blk)

    win_k = WINDOW + Q_BLOCK
    ws = pl.multiple_of(jnp.maximum(q0 - WINDOW, 0), Q_BLOCK)
    diff = t_col - (ws + lax.broadcasted_iota(jnp.int32, (1, win_k), 1))
    mask_w = (diff >= 0) & (diff < WINDOW)
    n_chunks = (q0 + Q_BLOCK + kc - 1) // kc

    for g in range(NSA_GROUPS):
        qg = _stack_heads(q_ref, g * NSA_HPG, NSA_HPG, NSA_DKP)

        s = lax.dot_general(qg, kc_ref[0, g], DN_T, preferred_element_type=F32)
        p_c = _softmax_masked(s.reshape(NSA_HPG, Q_BLOCK, cmp_pad), mask_c)
        o_c = jnp.dot(p_c.reshape(NSA_HPG * Q_BLOCK, cmp_pad).astype(BF16), vc_ref[0, g],
                      preferred_element_type=F32)

        p_sum = jnp.sum(p_c, axis=0)
        hi = p_sum.astype(BF16)
        r1 = p_sum - hi.astype(F32)
        mid = r1.astype(BF16)
        lo = (r1 - mid.astype(F32)).astype(BF16)
        aggt = aggt_ref[...]
        imp = (lax.dot_general(aggt, hi, DN_T, preferred_element_type=F32)
               + lax.dot_general(aggt, mid, DN_T, preferred_element_type=F32)
               + lax.dot_general(aggt, lo, DN_T, preferred_element_type=F32))
        val = jnp.where(forced, jnp.inf, jnp.where(valid, imp, NEG_INF))
        rank = jnp.zeros((blk_pad, Q_BLOCK), F32)
        for ii in range(n_blk):
            row = val[ii:ii + 1, :]
            beats = (row > val) | ((row == val) & (blk > ii))
            rank = rank + jnp.where(beats, 1.0, 0.0)
        sel_t = jnp.where((rank < n_sel) & (val > NEG_INF), 1.0, 0.0)
        sel = jnp.transpose(sel_t).astype(BF16)

        _flash_init(m_ref, l_ref, acc_ref)

        def sel_body(c, carry, g=g, qg=qg, sel=sel):
            k0 = pl.multiple_of(c * kc, kc)
            k_blk = ks_ref[pl.ds(k0, kc), g * NSA_DKP:(g + 1) * NSA_DKP]
            v_blk = vs_ref[pl.ds(k0, kc), g * NSA_DV:(g + 1) * NSA_DV]
            s_s = lax.dot_general(qg, k_blk, DN_T, preferred_element_type=F32)
            kpos_b = k0 + lax.broadcasted_iota(jnp.int32, (blk_pad, kc), 1)
            expand = jnp.where(
                lax.shift_right_logical(kpos_b, int(math.log2(SEL_BLOCK)))
                == lax.broadcasted_iota(jnp.int32, (blk_pad, kc), 0), 1.0, 0.0).astype(BF16)
            sel_keys = jnp.dot(sel, expand, preferred_element_type=F32)
            kpos = k0 + lax.broadcasted_iota(jnp.int32, (1, kc), 1)
            mask = (sel_keys > 0.5) & (kpos <= t_col)
            _flash_step(s_s, mask, v_blk, NSA_HPG, m_ref, l_ref, acc_ref)
            return carry

        lax.fori_loop(0, n_chunks, sel_body, 0)
        o_s = _flash_result(l_ref, acc_ref)

        k_win = kw_ref[pl.ds(ws, win_k), g * NSA_DKP:(g + 1) * NSA_DKP]
        v_win = vw_ref[pl.ds(ws, win_k), g * NSA_DV:(g + 1) * NSA_DV]
        s_w = lax.dot_general(qg, k_win, DN_T, preferred_element_type=F32)
        p_w = _softmax_masked(s_w.reshape(NSA_HPG, Q_BLOCK, win_k), mask_w)
        o_w = jnp.dot(p_w.reshape(NSA_HPG * Q_BLOCK, win_k).astype(BF16), v_win, preferred_element_type=F32)

        for h in range(NSA_HPG):
            c0 = (g * NSA_HPG + h) * 3
            rows = slice(h * Q_BLOCK, (h + 1) * Q_BLOCK)
            o = (gates[:, c0:c0 + 1] * o_c[rows] + gates[:, c0 + 1:c0 + 2] * o_s[rows]
                 + gates[:, c0 + 2:c0 + 3] * o_w[rows])
            head = g * NSA_HPG + h
            o_ref[:, head * NSA_DV:(head + 1) * NSA_DV] = o.astype(o_ref.dtype)


def _agg_t(seq, blk_pad, cmp_pad):
    n_cmp = (seq - CMP_BLOCK) // CMP_STRIDE + 1
    n_blk = seq // SEL_BLOCK
    r = SEL_BLOCK // CMP_STRIDE
    c = CMP_BLOCK // CMP_STRIDE
    j = np.arange(blk_pad)[:, None, None, None]
    i = np.arange(cmp_pad)[None, :, None, None]
    m = np.arange(r)[None, None, :, None]
    n = np.arange(c)[None, None, None, :]
    a = np.sum(i == r * j + m - n, axis=(2, 3)).astype(np.float32)
    a = a * (np.arange(blk_pad)[:, None] < n_blk) * (np.arange(cmp_pad)[None, :] < n_cmp)
    return jnp.asarray(a, dtype=BF16)


def _nsa_attention(p192, pv, small, k_cmp, v_cmp, cfg):
    b, s = cfg.batch, cfg.seq
    n_qb = s // Q_BLOCK
    kc = _tile(s, 512)
    cmp_pad = k_cmp.shape[2]
    blk_pad = max(V7X_LANES, s // SEL_BLOCK)
    qw = NSA_HEADS * NSA_DKP
    kw = NSA_GROUPS * NSA_DKP
    vw = NSA_GROUPS * NSA_DV
    rows = NSA_HPG * Q_BLOCK
    est = (2 * (_nbytes((Q_BLOCK, qw), BF16) + 2 * _nbytes((s, kw), BF16) + 2 * _nbytes((s, vw), BF16))
           + 3 * _nbytes((rows, V7X_LANES), F32) + 10 * _nbytes((rows, WINDOW + Q_BLOCK), F32) + (4 << 20))
    return pl.pallas_call(
        functools.partial(_nsa_kernel, seq=s, kc=kc),
        grid=(b, n_qb),
        in_specs=[
            pl.BlockSpec((Q_BLOCK, qw), lambda bi, i: (bi * n_qb + i, 0)),
            pl.BlockSpec((Q_BLOCK, V7X_LANES), lambda bi, i: (bi * n_qb + i, 0)),
            pl.BlockSpec((1, NSA_GROUPS, cmp_pad, NSA_DKP), lambda bi, i: (bi, 0, 0, 0)),
            pl.BlockSpec((1, NSA_GROUPS, cmp_pad, NSA_DV), lambda bi, i: (bi, 0, 0, 0)),
            pl.BlockSpec((s, kw), lambda bi, i: (bi, qw // kw)),
            pl.BlockSpec((s, vw), lambda bi, i: (bi, 0)),
            pl.BlockSpec((s, kw), lambda bi, i: (bi, qw // kw + 1)),
            pl.BlockSpec((s, vw), lambda bi, i: (bi, 1)),
            pl.BlockSpec((blk_pad, cmp_pad), lambda bi, i: (0, 0)),
        ],
        out_specs=pl.BlockSpec((Q_BLOCK, NSA_HEADS * NSA_DV), lambda bi, i: (bi * n_qb + i, 0)),
        out_shape=jax.ShapeDtypeStruct((b * s, NSA_HEADS * NSA_DV), BF16),
        scratch_shapes=[pltpu.VMEM((rows, 1), F32), pltpu.VMEM((rows, 1), F32), pltpu.VMEM((rows, NSA_DV), F32)],
        compiler_params=_params(("parallel", "arbitrary"), est),
        name="nsa_attention",
    )(p192, small, k_cmp, v_cmp, p192, pv, p192, pv, _agg_t(s, blk_pad, cmp_pad))


def _dsa_kernel(qi_ref, w_ref, ki_ref, qb_ref, kb_ref, vb_ref, o_ref, key_ref, m_ref, l_ref, acc_ref,
                *, kc, k_top):
    n_ck = key_ref.shape[0]
    q0 = pl.program_id(1) * Q_BLOCK
    t_col = q0 + lax.broadcasted_iota(jnp.int32, (Q_BLOCK, 1), 0)
    n_chunks = (q0 + Q_BLOCK + kc - 1) // kc
    w = w_ref[...] * (IDX_HEADS ** -0.5)
    key_ref[...] = jnp.full(key_ref.shape, INT_MIN, jnp.int32)
    hg = 8

    def score_body(c, carry):
        k0 = pl.multiple_of(c * kc, kc)
        k_i = ki_ref[pl.ds(k0, kc), :]
        acc = jnp.zeros((Q_BLOCK, kc), F32)
        for g in range(IDX_HEADS // hg):
            q_h = _stack_heads(qi_ref, g * hg, hg, IDX_DIM)
            rel = jnp.maximum(lax.dot_general(q_h, k_i, DN_T, preferred_element_type=F32), 0.0)
            for h in range(hg):
                col = GATE_COLS + g * hg + h
                acc = acc + rel[h * Q_BLOCK:(h + 1) * Q_BLOCK] * w[:, col:col + 1]
        bits = lax.bitcast_convert_type(acc, jnp.int32)
        skey = bits ^ (lax.shift_right_arithmetic(bits, 31) & np.int32(0x7FFFFFFF))
        kpos = k0 + lax.broadcasted_iota(jnp.int32, (1, kc), 1)
        key_ref[c] = jnp.where(kpos <= t_col, skey, INT_MIN)
        return carry

    lax.fori_loop(0, n_chunks, score_body, 0)

    def select_body(it, thr_u):
        bit = lax.shift_left(jnp.int32(1), 31 - it)
        cand = thr_u | bit
        cand_s = cand ^ INT_MIN
        cnt = jnp.zeros((Q_BLOCK, V7X_LANES), F32)
        for j in range(n_ck):
            keys = key_ref[j]
            for jj in range(kc // V7X_LANES):
                cnt = cnt + jnp.where(keys[:, jj * V7X_LANES:(jj + 1) * V7X_LANES] >= cand_s, 1.0, 0.0)
        total = jnp.sum(cnt, axis=-1, keepdims=True)
        return jnp.where(total >= k_top, cand, thr_u)

    thr_u = lax.fori_loop(0, 32, select_body, jnp.zeros((Q_BLOCK, 1), jnp.int32))
    thr = thr_u ^ INT_MIN

    _flash_init(m_ref, l_ref, acc_ref)
    q_groups = [_stack_heads(qb_ref, g * DSA_HPG, DSA_HPG, DSA_DK) for g in range(DSA_KV_HEADS)]

    def attn_body(c, carry):
        k0 = pl.multiple_of(c * kc, kc)
        kpos = k0 + lax.broadcasted_iota(jnp.int32, (1, kc), 1)
        mask = (key_ref[c] >= thr) & (kpos <= t_col)
        for g in range(DSA_KV_HEADS):
            k_blk = kb_ref[pl.ds(k0, kc), g * DSA_DK:(g + 1) * DSA_DK]
            v_blk = vb_ref[pl.ds(k0, kc), g * DSA_DV:(g + 1) * DSA_DV]
            s = lax.dot_general(q_groups[g], k_blk, DN_T, preferred_element_type=F32)
            _flash_step(s, mask, v_blk, DSA_HPG, m_ref.at[g], l_ref.at[g], acc_ref.at[g])
        return carry

    lax.fori_loop(0, n_chunks, attn_body, 0)
    for g in range(DSA_KV_HEADS):
        o = _flash_result(l_ref.at[g], acc_ref.at[g])
        for h in range(DSA_HPG):
            head = g * DSA_HPG + h
            o_ref[:, head * DSA_DV:(head + 1) * DSA_DV] = o[h * Q_BLOCK:(h + 1) * Q_BLOCK].astype(o_ref.dtype)


def _dsa_attention(p128, pv, small, cfg):
    b, s = cfg.batch, cfg.seq
    n_qb = s // Q_BLOCK
    kc = _tile(s, 512)
    k_top = min(DSA_TOPK, s // 4)
    qiw = IDX_HEADS * IDX_DIM
    qbw = DSA_HEADS * DSA_DK
    kbw = DSA_KV_HEADS * DSA_DK
    rows = DSA_HPG * Q_BLOCK
    est = (2 * (_nbytes((Q_BLOCK, qiw + qbw), BF16) + _nbytes((s, IDX_DIM + 2 * kbw), BF16))
           + _nbytes((Q_BLOCK, s), jnp.int32) + 12 * _nbytes((8 * Q_BLOCK, kc), F32) + (4 << 20))
    return pl.pallas_call(
        functools.partial(_dsa_kernel, kc=kc, k_top=k_top),
        grid=(b, n_qb),
        in_specs=[
            pl.BlockSpec((Q_BLOCK, qiw), lambda bi, i: (bi * n_qb + i, 0)),
            pl.BlockSpec((Q_BLOCK, V7X_LANES), lambda bi, i: (bi * n_qb + i, 0)),
            pl.BlockSpec((s, IDX_DIM), lambda bi, i: (bi, (qiw + qbw + kbw) // IDX_DIM)),
            pl.BlockSpec((Q_BLOCK, qbw), lambda bi, i: (bi * n_qb + i, qiw // qbw)),
            pl.BlockSpec((s, kbw), lambda bi, i: (bi, (qiw + qbw) // kbw)),
            pl.BlockSpec((s, kbw), lambda bi, i: (bi, 1)),
        ],
        out_specs=pl.BlockSpec((Q_BLOCK, DSA_HEADS * DSA_DV), lambda bi, i: (bi * n_qb + i, 0)),
        out_shape=jax.ShapeDtypeStruct((b * s, DSA_HEADS * DSA_DV), BF16),
        scratch_shapes=[
            pltpu.VMEM((s // kc, Q_BLOCK, kc), jnp.int32),
            pltpu.VMEM((DSA_KV_HEADS, rows, 1), F32),
            pltpu.VMEM((DSA_KV_HEADS, rows, 1), F32),
            pltpu.VMEM((DSA_KV_HEADS, rows, DSA_DV), F32),
        ],
        compiler_params=_params(("parallel", "arbitrary"), est),
        name="dsa_attention",
    )(p128, small, p128, p128, p128, pv)


def _pad_heads(w, heads, dk, dkp):
    d = w.shape[0]
    return jnp.pad(w.reshape(d, heads, dk), ((0, 0), (0, 0), (0, dkp - dk))).reshape(d, heads * dkp)


def _split_kv(w, heads, dk, dv):
    d = w.shape[0]
    w3 = w.reshape(d, heads, dk + dv)
    return w3[:, :, :dk].reshape(d, heads * dk), w3[:, :, dk:].reshape(d, heads * dv)


def _mixer_weights(w_in, d_model):
    sizes = (
        NSA_HEADS * NSA_DK, NSA_GROUPS * (NSA_DK + NSA_DV), NSA_GROUPS * (NSA_DK + NSA_DV),
        NSA_GROUPS * (NSA_DK + NSA_DV), 3 * NSA_HEADS, DSA_HEADS * DSA_DK, DSA_KV_HEADS * (DSA_DK + DSA_DV),
        IDX_HEADS * IDX_DIM, IDX_DIM, IDX_HEADS, 2 * d_model,
    )
    offs = np.cumsum((0,) + sizes)
    assert offs[-1] == w_in.shape[1]
    q_a, kv_c, kv_s, kv_w, gate_a, q_b, kv_b, q_i, k_i, w_i, merge = (
        w_in[:, offs[n]:offs[n + 1]] for n in range(len(sizes)))
    k_c, v_c = _split_kv(kv_c, NSA_GROUPS, NSA_DK, NSA_DV)
    k_s, v_s = _split_kv(kv_s, NSA_GROUPS, NSA_DK, NSA_DV)
    k_w, v_w = _split_kv(kv_w, NSA_GROUPS, NSA_DK, NSA_DV)
    k_b, v_b = _split_kv(kv_b, DSA_KV_HEADS, DSA_DK, DSA_DV)
    w192 = jnp.concatenate([
        _pad_heads(q_a, NSA_HEADS, NSA_DK, NSA_DKP), _pad_heads(k_s, NSA_GROUPS, NSA_DK, NSA_DKP),
        _pad_heads(k_w, NSA_GROUPS, NSA_DK, NSA_DKP)], axis=1).astype(BF16)
    scale192 = jnp.concatenate([
        jnp.full((NSA_HEADS * NSA_DKP,), NSA_DK ** -0.5, F32), jnp.ones((2 * NSA_GROUPS * NSA_DKP,), F32)])
    n128 = IDX_HEADS * IDX_DIM + DSA_HEADS * DSA_DK + DSA_KV_HEADS * DSA_DK + IDX_DIM
    n128_pad = -(-n128 // 1024) * 1024
    w128 = jnp.pad(jnp.concatenate([q_i, q_b, k_b, k_i], axis=1), ((0, 0), (0, n128_pad - n128))).astype(BF16)
    scale128 = jnp.concatenate([
        jnp.full((IDX_HEADS * IDX_DIM,), IDX_DIM ** -0.5, F32), jnp.full((DSA_HEADS * DSA_DK,), DSA_DK ** -0.5, F32),
        jnp.ones((n128_pad - IDX_HEADS * IDX_DIM - DSA_HEADS * DSA_DK,), F32)])
    wv = jnp.concatenate([v_s, v_w, v_b], axis=1).astype(BF16)
    wc = jnp.concatenate([k_c, v_c], axis=1).astype(BF16)
    small_n = GATE_COLS + IDX_HEADS
    wsmall = jnp.pad(jnp.concatenate([gate_a, w_i], axis=1), ((0, 0), (0, V7X_LANES - small_n))).astype(BF16)
    return w192, scale192, w128, scale128, wv, wc, wsmall, merge.astype(BF16)


def _ffn(x2, gain, mod3, idx, w1, w3, w2, cfg):
    t, d = x2.shape
    ff = w1.shape[1]
    ff_pad = -(-ff // 1024) * 1024 if ff > 1024 else ff
    w1p = jnp.pad(w1.astype(BF16), ((0, 0), (0, ff_pad - ff)))
    w3p = jnp.pad(w3.astype(BF16), ((0, 0), (0, ff_pad - ff)))
    w2p = jnp.pad(w2.astype(BF16), ((0, ff_pad - ff), (0, 0)))
    h = _norm_mod(x2, gain, mod3, idx, idx + 1, cfg)
    tm = _tile(cfg.seq, 1024)
    nsb = cfg.seq // tm
    u = _matmul([h], [w1p, w3p], [(0, 0), (0, 1)], _epi_swiglu, [], ff_pad, BF16,
                tm=tm, tn=_tile(ff_pad, 512), tk=d, name="ffn_up")
    tn = _tile(d, 1024)
    epi = [(x2, (tm, tn), lambda i, j: (i, j)),
           (mod3, (1, 1, tn), lambda i, j: ((i // nsb) * N_MOD + idx + 2, 0, j))]
    return _matmul([u], [w2p], [(0, 0)], functools.partial(_epi_residual, coef=0.5), epi, d, F32,
                   tm=tm, tn=tn, tk=_tile(ff_pad, 1408), name="ffn_down")


def _mixer(x2, gain, mod3, w_in, cmp_pe_k, cmp_w1_k, cmp_w2_k, cmp_pe_v, cmp_w1_v, cmp_w2_v,
           proj_a, proj_b, w_out, cfg):
    b, s, d = cfg.batch, cfg.seq, cfg.d_model
    w192, scale192, w128, scale128, wv, wc, wsmall, wmerge = _mixer_weights(w_in, d)
    h = _norm_mod(x2, gain, mod3, 3, 4, cfg)
    p192 = _project_rope(h, w192, scale192, NSA_DK, NSA_DKP, cfg, "proj_rope192")
    p128 = _project_rope(h, w128, scale128, DSA_DK, DSA_DK, cfg, "proj_rope128")
    pv = _project(h, wv, _epi_plain, BF16, cfg, "proj_values")
    pc = _project(h, wc, _epi_plain, F32, cfg, "proj_cmp")
    small = _project(h, wsmall, _epi_plain, F32, cfg, "proj_small")
    gates_m = _project(h, wmerge, _epi_sigmoid, F32, cfg, "proj_merge")

    n_rows = s // CMP_STRIDE
    kdim = NSA_GROUPS * NSA_DK

    def chunks(cols, dd):
        xg = cols.reshape(b, n_rows, CMP_STRIDE, NSA_GROUPS, dd)
        return jnp.transpose(xg, (0, 3, 1, 2, 4)).reshape(b * NSA_GROUPS, n_rows, CMP_STRIDE * dd)

    cmp_pos = jnp.arange(n_rows) * CMP_STRIDE + CMP_BLOCK - 1
    k_cmp = _compress(chunks(pc[:, :kdim], NSA_DK), cmp_pe_k, cmp_w1_k, cmp_w2_k, NSA_DK, NSA_DKP, cmp_pos,
                      "compress_k").reshape(b, NSA_GROUPS, n_rows, NSA_DKP)
    v_cmp = _compress(chunks(pc[:, kdim:], NSA_DV), cmp_pe_v, cmp_w1_v, cmp_w2_v, NSA_DV, NSA_DV, None,
                      "compress_v").reshape(b, NSA_GROUPS, n_rows, NSA_DV)

    o_a = _nsa_attention(p192, pv, small, k_cmp, v_cmp, cfg)
    o_b = _dsa_attention(p128, pv, small, cfg)

    tm = _tile(s, 512)
    tn = _tile(d, 1024)
    nsb = s // tm
    epi = [(gates_m, (tm, tn), lambda i, j: (i, j)), (gates_m, (tm, tn), lambda i, j: (i, j + d // tn))]
    y = _matmul([o_a, o_b], [proj_a.astype(BF16), proj_b.astype(BF16)], [(0, 0), (1, 1)], _epi_merge, epi,
                d, BF16, tm=tm, tn=tn, tk=o_a.shape[1], name="merge_proj")
    epi = [(x2, (tm, tn), lambda i, j: (i, j)),
           (mod3, (1, 1, tn), lambda i, j: ((i // nsb) * N_MOD + 5, 0, j))]
    return _matmul([y], [w_out.astype(BF16)], [(0, 0)], functools.partial(_epi_residual, coef=1.0), epi,
                   d, F32, tm=tm, tn=tn, tk=d, name="out_proj")


def _layer(x, c, ada_w, ada_b, norm_ffn1, ffn1_w1, ffn1_w3, ffn1_w2, norm_mix, w_in,
           cmp_pe_k, cmp_w1_k, cmp_w2_k, cmp_pe_v, cmp_w1_v, cmp_w2_v, proj_a, proj_b, w_out,
           norm_ffn2, ffn2_w1, ffn2_w3, ffn2_w2, norm_final):
    b, s, d = x.shape
    cfg = Cfg(b, s, d, ffn1_w1.shape[-1])
    assert s % Q_BLOCK == 0 and s >= WINDOW + Q_BLOCK and d % V7X_LANES == 0
    x2 = x.reshape(b * s, d)
    c_pad = jnp.pad(c, ((0, 8 - b), (0, 0)))
    for l in range(ada_w.shape[0]):
        mod = _adaln(c_pad, ada_w[l], ada_b[l].reshape(1, -1))
        mod3 = mod[:b].reshape(b * N_MOD, 1, d)
        x2 = _ffn(x2, norm_ffn1[l], mod3, 0, ffn1_w1[l], ffn1_w3[l], ffn1_w2[l], cfg)
        x2 = _mixer(x2, norm_mix[l], mod3, w_in[l], cmp_pe_k[l], cmp_w1_k[l], cmp_w2_k[l],
                    cmp_pe_v[l], cmp_w1_v[l], cmp_w2_v[l], proj_a[l], proj_b[l], w_out[l], cfg)
        x2 = _ffn(x2, norm_ffn2[l], mod3, 6, ffn2_w1[l], ffn2_w3[l], ffn2_w2[l], cfg)
    return _final_norm(x2, norm_final).reshape(b, s, d)


def kernel(x, c, ada_w, ada_b, norm_ffn1, ffn1_w1, ffn1_w3, ffn1_w2, norm_mix, w_in, cmp_pe_k, cmp_w1_k, cmp_w2_k, cmp_pe_v, cmp_w1_v, cmp_w2_v, proj_a, proj_b, w_out, norm_ffn2, ffn2_w1, ffn2_w3, ffn2_w2, norm_final):
    return _layer(x, c, ada_w, ada_b, norm_ffn1, ffn1_w1, ffn1_w3, ffn1_w2, norm_mix, w_in,
                  cmp_pe_k, cmp_w1_k, cmp_w2_k, cmp_pe_v, cmp_w1_v, cmp_w2_v, proj_a, proj_b, w_out,
                  norm_ffn2, ffn2_w1, ffn2_w3, ffn2_w2, norm_final)
```

```python
import functools
import math
from typing import NamedTuple

import numpy as np
import jax
import jax.numpy as jnp
from jax import lax
from jax.experimental import pallas as pl
from jax.experimental.pallas import tpu as pltpu

F32 = jnp.float32
BF16 = jnp.bfloat16

V7X_LANES = 128
V7X_VMEM_BYTES = 64 * 1024 * 1024
V7X_VMEM_BUDGET = 56 * 1024 * 1024

EPS = 1e-6
ROPE_THETA = 500000.0
ROPE_FRACTION = 4
N_MOD = 9
Q_BLOCK = 128

NSA_HEADS = 16
NSA_GROUPS = 2
NSA_HPG = NSA_HEADS // NSA_GROUPS
NSA_DK = 192
NSA_DKP = 256
NSA_DV = 128
CMP_BLOCK = 32
CMP_STRIDE = 16
SEL_BLOCK = 64
SEL_TOP = 16
WINDOW = 512

DSA_HEADS = 16
DSA_KV_HEADS = 4
DSA_HPG = DSA_HEADS // DSA_KV_HEADS
DSA_DK = 128
DSA_DV = 128
IDX_HEADS = 32
IDX_DIM = 128
DSA_TOPK = 256

GATE_COLS = 3 * NSA_HEADS
INT_MIN = np.int32(-(2**31))
NEG_INF = float("-inf")
LOG2E = math.log2(math.e)
TINY = float(np.finfo(np.float32).tiny)
DN_T = (((1,), (1,)), ((), ()))


class Cfg(NamedTuple):
    batch: int
    seq: int
    d_model: int
    d_ff: int


def _tile(n, pref):
    if n <= pref:
        return n
    t = (pref // V7X_LANES) * V7X_LANES
    while t > V7X_LANES and n % t:
        t -= V7X_LANES
    assert n % t == 0, (n, pref)
    return t


def _nbytes(shape, dtype):
    return int(np.prod(shape)) * jnp.dtype(dtype).itemsize


def _params(dims, est_bytes):
    limit = int(min(V7X_VMEM_BUDGET, max(est_bytes, 16 * 1024 * 1024)))
    return pltpu.CompilerParams(dimension_semantics=dims, vmem_limit_bytes=limit)


def _adaln_kernel(c_ref, w_ref, b_ref, o_ref, acc_ref, *, nk):
    k = pl.program_id(1)

    @pl.when(k == 0)
    def _():
        acc_ref[...] = jnp.zeros_like(acc_ref)

    c = c_ref[...]
    c_act = (c * jax.nn.sigmoid(c)).astype(BF16)
    acc_ref[...] += jnp.dot(c_act, w_ref[...].astype(BF16), preferred_element_type=F32)

    @pl.when(k == nk - 1)
    def _():
        o_ref[...] = acc_ref[...] + b_ref[...]


def _adaln(c_pad, w, b):
    m, kdim = c_pad.shape
    n = w.shape[1]
    tn, tk = _tile(n, 2048), _tile(kdim, 1024)
    nk = kdim // tk
    est = 2 * _nbytes((tk, tn), F32) + _nbytes((tk, tn), BF16) + 4 * _nbytes((m, tn), F32) + (2 << 20)
    return pl.pallas_call(
        functools.partial(_adaln_kernel, nk=nk),
        grid=(n // tn, nk),
        in_specs=[
            pl.BlockSpec((m, tk), lambda j, k: (0, k)),
            pl.BlockSpec((tk, tn), lambda j, k: (k, j)),
            pl.BlockSpec((1, tn), lambda j, k: (0, j)),
        ],
        out_specs=pl.BlockSpec((m, tn), lambda j, k: (0, j)),
        out_shape=jax.ShapeDtypeStruct((m, n), F32),
        scratch_shapes=[pltpu.VMEM((m, tn), F32)],
        compiler_params=_params(("parallel", "arbitrary"), est),
        name="adaln",
    )(c_pad, w, b)


def _norm_mod_kernel(x_ref, g_ref, sh_ref, sc_ref, o_ref):
    x = x_ref[...]
    y = x * lax.rsqrt(jnp.mean(x * x, axis=-1, keepdims=True) + EPS)
    y = y * g_ref[...]
    o_ref[...] = (y * (1.0 + sc_ref[0]) + sh_ref[0]).astype(o_ref.dtype)


def _norm_mod(x2, gain, mod3, shift_idx, scale_idx, cfg):
    t, d = x2.shape
    tm = _tile(cfg.seq, 256)
    nsb = cfg.seq // tm
    est = 4 * _nbytes((tm, d), F32) + 2 * _nbytes((tm, d), BF16) + (2 << 20)
    return pl.pallas_call(
        _norm_mod_kernel,
        grid=(t // tm,),
        in_specs=[
            pl.BlockSpec((tm, d), lambda i: (i, 0)),
            pl.BlockSpec((1, d), lambda i: (0, 0)),
            pl.BlockSpec((1, 1, d), lambda i: ((i // nsb) * N_MOD + shift_idx, 0, 0)),
            pl.BlockSpec((1, 1, d), lambda i: ((i // nsb) * N_MOD + scale_idx, 0, 0)),
        ],
        out_specs=pl.BlockSpec((tm, d), lambda i: (i, 0)),
        out_shape=jax.ShapeDtypeStruct((t, d), BF16),
        compiler_params=_params(("parallel",), est),
        name="norm_mod",
    )(x2, gain.reshape(1, d), mod3, mod3)


def _final_norm_kernel(x_ref, g_ref, o_ref):
    x = x_ref[...]
    y = x * lax.rsqrt(jnp.mean(x * x, axis=-1, keepdims=True) + EPS)
    o_ref[...] = y * g_ref[...]


def _final_norm(x2, gain):
    t, d = x2.shape
    tm = _tile(t, 256)
    est = 6 * _nbytes((tm, d), F32) + (2 << 20)
    return pl.pallas_call(
        _final_norm_kernel,
        grid=(t // tm,),
        in_specs=[pl.BlockSpec((tm, d), lambda i: (i, 0)), pl.BlockSpec((1, d), lambda i: (0, 0))],
        out_specs=pl.BlockSpec((tm, d), lambda i: (i, 0)),
        out_shape=jax.ShapeDtypeStruct((t, d), F32),
        compiler_params=_params(("parallel",), est),
        name="final_norm",
    )(x2, gain.reshape(1, d))


def _mm_kernel(*refs, n_a, n_w, pairs, n_epi, epilogue, nk):
    a_refs = refs[:n_a]
    w_refs = refs[n_a:n_a + n_w]
    e_refs = refs[n_a + n_w:n_a + n_w + n_epi]
    o_ref = refs[n_a + n_w + n_epi]
    acc_refs = refs[n_a + n_w + n_epi + 1:]

    def prods():
        return [jnp.dot(a_refs[ai][...], w_refs[wi][...], preferred_element_type=F32) for ai, wi in pairs]

    if nk == 1:
        o_ref[...] = epilogue(prods(), e_refs).astype(o_ref.dtype)
        return

    k = pl.program_id(2)

    @pl.when(k == 0)
    def _():
        for acc in acc_refs:
            acc[...] = jnp.zeros_like(acc)

    for acc, p in zip(acc_refs, prods()):
        acc[...] += p

    @pl.when(k == nk - 1)
    def _():
        o_ref[...] = epilogue([acc[...] for acc in acc_refs], e_refs).astype(o_ref.dtype)


def _matmul(a_list, w_list, pairs, epilogue, epi_inputs, n_out, out_dtype, *, tm, tn, tk, name):
    m, kdim = a_list[0].shape
    nk = kdim // tk
    assert m % tm == 0 and n_out % tn == 0 and kdim % tk == 0
    in_specs, est = [], 0
    for a in a_list:
        in_specs.append(pl.BlockSpec((tm, tk), lambda i, j, k: (i, k)))
        est += 2 * _nbytes((tm, tk), a.dtype)
    for w in w_list:
        in_specs.append(pl.BlockSpec((tk, tn), lambda i, j, k: (k, j)))
        est += 2 * _nbytes((tk, tn), w.dtype)
    for arr, bshape, imap in epi_inputs:
        in_specs.append(pl.BlockSpec(bshape, lambda i, j, k, imap=imap: imap(i, j)))
        est += 2 * _nbytes(bshape, arr.dtype)
    est += 2 * _nbytes((tm, tn), out_dtype) + (len(pairs) + 3) * _nbytes((tm, tn), F32) + (2 << 20)
    scratch = [] if nk == 1 else [pltpu.VMEM((tm, tn), F32) for _ in pairs]
    kern = functools.partial(_mm_kernel, n_a=len(a_list), n_w=len(w_list), pairs=tuple(pairs),
                             n_epi=len(epi_inputs), epilogue=epilogue, nk=nk)
    return pl.pallas_call(
        kern,
        grid=(m // tm, n_out // tn, nk),
        in_specs=in_specs,
        out_specs=pl.BlockSpec((tm, tn), lambda i, j, k: (i, j)),
        out_shape=jax.ShapeDtypeStruct((m, n_out), out_dtype),
        scratch_shapes=scratch,
        compiler_params=_params(("parallel", "parallel", "arbitrary"), est),
        name=name,
    )(*a_list, *w_list, *[e[0] for e in epi_inputs])


def _epi_plain(accs, e_refs):
    return accs[0]


def _epi_sigmoid(accs, e_refs):
    return jax.nn.sigmoid(accs[0])


def _epi_swiglu(accs, e_refs):
    return jax.nn.silu(accs[0]) * accs[1]


def _epi_merge(accs, e_refs):
    return e_refs[0][...] * accs[0] + e_refs[1][...] * accs[1]


def _epi_residual(accs, e_refs, *, coef):
    return e_refs[0][...] + (coef * e_refs[1][0]) * accs[0]


def _rope_apply(a, cos, sin_lo, sin_hi, half):
    width = a.shape[-1]
    return a * cos + pltpu.roll(a, width - half, 1) * sin_lo + pltpu.roll(a, half, 1) * sin_hi


def _epi_rope(accs, e_refs, *, half):
    out = _rope_apply(accs[0], e_refs[0][...], e_refs[1][...], e_refs[2][...], half)
    return out * e_refs[3][...]


def _rope_tables(pos, dk, head_w):
    r = dk // ROPE_FRACTION
    half = r // 2
    n = pos.shape[0]
    inv = ROPE_THETA ** (-jnp.arange(0, r, 2, dtype=F32) / r)
    ang = pos.astype(F32)[:, None] * inv[None, :]
    cos, sin = jnp.cos(ang), jnp.sin(ang)
    cos_t = jnp.concatenate([cos, cos, jnp.ones((n, head_w - r), F32)], axis=1)
    sin_lo = jnp.concatenate([-sin, jnp.zeros((n, head_w - half), F32)], axis=1)
    sin_hi = jnp.concatenate([jnp.zeros((n, half), F32), sin, jnp.zeros((n, head_w - r), F32)], axis=1)
    return cos_t, sin_lo, sin_hi, half


def _project_rope(h, w, colscale, dk, head_w, cfg, name):
    n = w.shape[1]
    tm = _tile(cfg.seq, 512)
    tn = _tile(n, 1024)
    nsb = cfg.seq // tm
    cos_t, sin_lo, sin_hi, half = _rope_tables(jnp.arange(cfg.seq), dk, head_w)
    reps = tn // head_w
    tabs = [jnp.tile(tb, (1, reps)) for tb in (cos_t, sin_lo, sin_hi)]
    epi = [(tb, (tm, tn), lambda i, j: (i % nsb, 0)) for tb in tabs]
    epi.append((colscale.reshape(1, n), (1, tn), lambda i, j: (0, j)))
    return _matmul([h], [w], [(0, 0)], functools.partial(_epi_rope, half=half), epi, n, BF16,
                   tm=tm, tn=tn, tk=h.shape[1], name=name)


def _project(h, w, epilogue, out_dtype, cfg, name):
    n = w.shape[1]
    return _matmul([h], [w], [(0, 0)], epilogue, [], n, out_dtype,
                   tm=_tile(cfg.seq, 512), tn=_tile(n, 1024), tk=h.shape[1], name=name)


def _gelu_tanh(x):
    return 0.5 * x * (1.0 + jnp.tanh(math.sqrt(2.0 / math.pi) * (x + 0.044715 * (x * x * x))))


def _compress_kernel(x_ref, pe_ref, w1_ref, w2_ref, *rest, half):
    o_ref = rest[-1]
    x = x_ref[0]
    n_rows, width = x.shape
    a_lo = (x + pe_ref[0:1, :]).astype(BF16)
    a_hi = (x + pe_ref[1:2, :]).astype(BF16)
    p = jnp.dot(a_lo, w1_ref[0:width, :], preferred_element_type=F32)
    q = jnp.dot(a_hi, w1_ref[width:2 * width, :], preferred_element_type=F32)
    hid = _gelu_tanh(p + pltpu.roll(q, n_rows - 1, 0))
    y = jnp.dot(hid.astype(BF16), w2_ref[...], preferred_element_type=F32)
    if half:
        y = _rope_apply(y, rest[0][...], rest[1][...], rest[2][...], half)
    o_ref[0] = y.astype(o_ref.dtype)


def _compress(x_chunks, pe, w1, w2, d, d_out, rope_pos, name):
    bg, n_rows, width = x_chunks.shape
    hid_w = ((d + V7X_LANES - 1) // V7X_LANES) * V7X_LANES
    pe2 = pe.reshape(2, width)
    w1p = jnp.pad(w1.astype(BF16), ((0, 0), (0, hid_w - d)))
    w2p = jnp.pad(w2.astype(BF16), ((0, hid_w - d), (0, d_out - d)))
    ins = [x_chunks, pe2, w1p, w2p]
    in_specs = [
        pl.BlockSpec((1, n_rows, width), lambda i: (i, 0, 0)),
        pl.BlockSpec((2, width), lambda i: (0, 0)),
        pl.BlockSpec((2 * width, hid_w), lambda i: (0, 0)),
        pl.BlockSpec((hid_w, d_out), lambda i: (0, 0)),
    ]
    half = 0
    if rope_pos is not None:
        cos_t, sin_lo, sin_hi, half = _rope_tables(rope_pos, d, d_out)
        ins += [cos_t, sin_lo, sin_hi]
        in_specs += [pl.BlockSpec((n_rows, d_out), lambda i: (0, 0))] * 3
    est = (2 * _nbytes((n_rows, width), F32) + 2 * _nbytes((n_rows, width), BF16)
           + 2 * _nbytes((2 * width, hid_w), BF16) + (8 << 20))
    return pl.pallas_call(
        functools.partial(_compress_kernel, half=half),
        grid=(bg,),
        in_specs=in_specs,
        out_specs=pl.BlockSpec((1, n_rows, d_out), lambda i: (i, 0, 0)),
        out_shape=jax.ShapeDtypeStruct((bg, n_rows, d_out), BF16),
        compiler_params=_params(("parallel",), est),
        name=name,
    )(*ins)


def _stack_heads(ref, first, count, width):
    return jnp.concatenate([ref[:, (first + h) * width:(first + h + 1) * width] for h in range(count)], axis=0)


def _mask_bias(mask):
    return jnp.where(mask, 0.0, NEG_INF)


def _with_ones(v):
    return jnp.concatenate([v, jnp.ones((v.shape[0], V7X_LANES), v.dtype)], axis=1)


def _exp2_masked(s3, bias2):
    z = s3 + bias2[None]
    m = jnp.max(z, axis=-1, keepdims=True)
    m = jnp.where(m > NEG_INF, m, 0.0)
    return jnp.exp2(z - m)


def _normalize(pv, dv):
    return pv[:, :dv] / jnp.maximum(pv[:, dv:dv + V7X_LANES], TINY)


def _flash_init(m_ref, acc_ref):
    m_ref[...] = jnp.full(m_ref.shape, NEG_INF, F32)
    acc_ref[...] = jnp.zeros(acc_ref.shape, F32)


def _flash_step(s, bias2, v_ones, heads, m_ref, acc_ref):
    rows, kc = s.shape
    z = (s.reshape(heads, rows // heads, kc) + bias2[None]).reshape(rows, kc)
    m_old = m_ref[...]
    m_new = jnp.maximum(m_old, jnp.max(z, axis=-1, keepdims=True))
    m_safe = jnp.where(m_new > NEG_INF, m_new, 0.0)
    alpha = jnp.exp2(m_old - m_safe)
    p = jnp.exp2(z - jnp.tile(m_safe, (1, kc // V7X_LANES)))
    pv = jnp.dot(p.astype(BF16), v_ones, preferred_element_type=F32)
    acc_ref[...] = jnp.tile(alpha, (1, acc_ref.shape[-1] // V7X_LANES)) * acc_ref[...] + pv
    m_ref[...] = m_new


def _chunk_loop(n_chunks, process):
    def pair_body(j, carry):
        process([2 * j, 2 * j + 1])
        return carry

    lax.fori_loop(0, n_chunks // 2, pair_body, 0)

    @pl.when(n_chunks % 2 == 1)
    def _():
        process([n_chunks - 1])


def _nsa_kernel(q_ref, gate_ref, kc_ref, vc_ref, ks_ref, vs_ref, kw_ref, vw_ref, aggt_ref, o_ref,
                m_ref, acc_ref, *, seq, kc, blk_pad):
    n_cmp = (seq - CMP_BLOCK) // CMP_STRIDE + 1
    n_blk = seq // SEL_BLOCK
    n_sel = min(SEL_TOP, n_blk)
    cmp_pad = aggt_ref.shape[1]
    q0 = pl.program_id(1) * Q_BLOCK
    t_col = q0 + lax.broadcasted_iota(jnp.int32, (Q_BLOCK, 1), 0)
    t_row = q0 + lax.broadcasted_iota(jnp.int32, (1, Q_BLOCK), 1)
    gates = jax.nn.sigmoid(gate_ref[...])

    cidx = lax.broadcasted_iota(jnp.int32, (Q_BLOCK, cmp_pad), 1)
    bias_c = _mask_bias((cidx * CMP_STRIDE + (CMP_BLOCK - 1) <= t_col) & (cidx < n_cmp))

    blk = lax.broadcasted_iota(jnp.int32, (n_blk, Q_BLOCK), 0)
    cur = lax.shift_right_logical(t_row, int(math.log2(SEL_BLOCK)))
    forced = (blk == 0) | (blk == cur) | (blk == cur - 1)
    valid = blk * SEL_BLOCK <= t_row

    win_k = WINDOW + Q_BLOCK
    ws = pl.multiple_of(jnp.maximum(q0 - WINDOW, 0), Q_BLOCK)
    diff = t_col - (ws + lax.broadcasted_iota(jnp.int32, (1, win_k), 1))
    bias_w = _mask_bias((diff >= 0) & (diff < WINDOW))
    n_chunks = (q0 + Q_BLOCK + kc - 1) // kc
    rows = NSA_HPG * Q_BLOCK

    q_groups = [_stack_heads(q_ref, g * NSA_HPG, NSA_HPG, NSA_DKP) for g in range(NSA_GROUPS)]
    o_cmp, sels = [], []
    for g in range(NSA_GROUPS):
        qg = q_groups[g]

        s = lax.dot_general(qg, kc_ref[0, g], DN_T, preferred_element_type=F32)
        e_c = _exp2_masked(s.reshape(NSA_HPG, Q_BLOCK, cmp_pad), bias_c)
        p_c = e_c / jnp.maximum(jnp.sum(e_c, axis=-1, keepdims=True), TINY)
        o_c = jnp.dot(p_c.reshape(rows, cmp_pad).astype(BF16), vc_ref[0, g], preferred_element_type=F32)

        p_sum = jnp.sum(p_c, axis=0)
        hi = p_sum.astype(BF16)
        r1 = p_sum - hi.astype(F32)
        mid = r1.astype(BF16)
        lo = (r1 - mid.astype(F32)).astype(BF16)
        aggt = aggt_ref[...]
        imp = (lax.dot_general(aggt, hi, DN_T, preferred_element_type=F32)
               + lax.dot_general(aggt, mid, DN_T, preferred_element_type=F32)
               + lax.dot_general(aggt, lo, DN_T, preferred_element_type=F32))
        val = jnp.where(forced, jnp.inf, jnp.where(valid, imp, NEG_INF))
        rank = jnp.zeros((n_blk, Q_BLOCK), F32)
        for ii in range(n_blk):
            row = val[ii:ii + 1, :]
            beats = (row > val) | ((row == val) & (blk > ii))
            rank = rank + jnp.where(beats, 1.0, 0.0)
        sel_t = jnp.where((rank < n_sel) & (val > NEG_INF), 1.0, 0.0)
        if blk_pad > n_blk:
            sel_t = jnp.concatenate([sel_t, jnp.zeros((blk_pad - n_blk, Q_BLOCK), F32)], axis=0)
        sels.append(jnp.transpose(sel_t).astype(BF16))
        o_cmp.append(o_c)

    _flash_init(m_ref, acc_ref)

    def sel_chunks(chunks):
        scores = []
        for c in chunks:
            k0 = pl.multiple_of(c * kc, kc)
            scores.append([
                lax.dot_general(q_groups[g], ks_ref[pl.ds(k0, kc), g * NSA_DKP:(g + 1) * NSA_DKP], DN_T,
                                preferred_element_type=F32) for g in range(NSA_GROUPS)])
        for c, s_c in zip(chunks, scores):
            k0 = pl.multiple_of(c * kc, kc)
            kpos_b = k0 + lax.broadcasted_iota(jnp.int32, (blk_pad, kc), 1)
            expand = jnp.where(
                lax.shift_right_logical(kpos_b, int(math.log2(SEL_BLOCK)))
                == lax.broadcasted_iota(jnp.int32, (blk_pad, kc), 0), 1.0, 0.0).astype(BF16)
            causal = k0 + lax.broadcasted_iota(jnp.int32, (1, kc), 1) <= t_col
            for g in range(NSA_GROUPS):
                v_blk = vs_ref[pl.ds(k0, kc), g * NSA_DV:(g + 1) * NSA_DV]
                sel_keys = jnp.dot(sels[g], expand, preferred_element_type=F32)
                bias = _mask_bias((sel_keys > 0.5) & causal)
                _flash_step(s_c[g], bias, _with_ones(v_blk), NSA_HPG, m_ref.at[g], acc_ref.at[g])

    _chunk_loop(n_chunks, sel_chunks)

    for g in range(NSA_GROUPS):
        qg, o_c = q_groups[g], o_cmp[g]
        o_s = _normalize(acc_ref[g], NSA_DV)

        k_win = kw_ref[pl.ds(ws, win_k), g * NSA_DKP:(g + 1) * NSA_DKP]
        v_win = vw_ref[pl.ds(ws, win_k), g * NSA_DV:(g + 1) * NSA_DV]
        s_w = lax.dot_general(qg, k_win, DN_T, preferred_element_type=F32)
        e_w = _exp2_masked(s_w.reshape(NSA_HPG, Q_BLOCK, win_k), bias_w)
        o_w = _normalize(jnp.dot(e_w.reshape(rows, win_k).astype(BF16), _with_ones(v_win),
                                 preferred_element_type=F32), NSA_DV)

        for h in range(NSA_HPG):
            c0 = (g * NSA_HPG + h) * 3
            hr = slice(h * Q_BLOCK, (h + 1) * Q_BLOCK)
            o = (gates[:, c0:c0 + 1] * o_c[hr] + gates[:, c0 + 1:c0 + 2] * o_s[hr]
                 + gates[:, c0 + 2:c0 + 3] * o_w[hr])
            head = g * NSA_HPG + h
            o_ref[:, head * NSA_DV:(head + 1) * NSA_DV] = o.astype(o_ref.dtype)


def _agg_t(seq, cmp_pad):
    n_cmp = (seq - CMP_BLOCK) // CMP_STRIDE + 1
    n_blk = seq // SEL_BLOCK
    r = SEL_BLOCK // CMP_STRIDE
    c = CMP_BLOCK // CMP_STRIDE
    j = np.arange(n_blk)[:, None, None, None]
    i = np.arange(cmp_pad)[None, :, None, None]
    m = np.arange(r)[None, None, :, None]
    n = np.arange(c)[None, None, None, :]
    a = np.sum(i == r * j + m - n, axis=(2, 3)).astype(np.float32)
    a = a * (np.arange(cmp_pad)[None, :] < n_cmp)
    return jnp.asarray(a, dtype=BF16)


def _nsa_attention(p192, pv, small, k_cmp, v_cmp, cfg):
    b, s = cfg.batch, cfg.seq
    n_qb = s // Q_BLOCK
    kc = _tile(s, 512)
    cmp_pad = k_cmp.shape[2]
    blk_pad = max(V7X_LANES, s // SEL_BLOCK)
    qw = NSA_HEADS * NSA_DKP
    kw = NSA_GROUPS * NSA_DKP
    vw = NSA_GROUPS * NSA_DV
    rows = NSA_HPG * Q_BLOCK
    est = (2 * (_nbytes((Q_BLOCK, qw), BF16) + 2 * _nbytes((s, kw), BF16) + 2 * _nbytes((s, vw), BF16))
           + 3 * _nbytes((rows, V7X_LANES), F32) + 10 * _nbytes((rows, WINDOW + Q_BLOCK), F32) + (4 << 20))
    return pl.pallas_call(
        functools.partial(_nsa_kernel, seq=s, kc=kc, blk_pad=blk_pad),
        grid=(b, n_qb),
        in_specs=[
            pl.BlockSpec((Q_BLOCK, qw), lambda bi, i: (bi * n_qb + i, 0)),
            pl.BlockSpec((Q_BLOCK, V7X_LANES), lambda bi, i: (bi * n_qb + i, 0)),
            pl.BlockSpec((1, NSA_GROUPS, cmp_pad, NSA_DKP), lambda bi, i: (bi, 0, 0, 0)),
            pl.BlockSpec((1, NSA_GROUPS, cmp_pad, NSA_DV), lambda bi, i: (bi, 0, 0, 0)),
            pl.BlockSpec((s, kw), lambda bi, i: (bi, qw // kw)),
            pl.BlockSpec((s, vw), lambda bi, i: (bi, 0)),
            pl.BlockSpec((s, kw), lambda bi, i: (bi, qw // kw + 1)),
            pl.BlockSpec((s, vw), lambda bi, i: (bi, 1)),
            pl.BlockSpec((s // SEL_BLOCK, cmp_pad), lambda bi, i: (0, 0)),
        ],
        out_specs=pl.BlockSpec((Q_BLOCK, NSA_HEADS * NSA_DV), lambda bi, i: (bi * n_qb + i, 0)),
        out_shape=jax.ShapeDtypeStruct((b * s, NSA_HEADS * NSA_DV), BF16),
        scratch_shapes=[pltpu.VMEM((NSA_GROUPS, rows, V7X_LANES), F32),
                        pltpu.VMEM((NSA_GROUPS, rows, NSA_DV + V7X_LANES), F32)],
        compiler_params=_params(("parallel", "arbitrary"), est),
        name="nsa_attention",
    )(p192, small, k_cmp, v_cmp, p192, pv, p192, pv, _agg_t(s, cmp_pad))


def _dsa_kernel(qi_ref, w_ref, ki_ref, qb_ref, kb_ref, vb_ref, o_ref, key_ref, m_ref, acc_ref, *, kc, k_top):
    q0 = pl.program_id(1) * Q_BLOCK
    t_row = q0 + lax.broadcasted_iota(jnp.int32, (1, Q_BLOCK), 1)
    n_chunks = (q0 + Q_BLOCK + kc - 1) // kc
    w_t = jnp.transpose(w_ref[...] * (IDX_HEADS ** -0.5))
    hg = 8
    cnt_rows = 64

    def causal_t(c):
        return c * kc + lax.broadcasted_iota(jnp.int32, (kc, 1), 0) <= t_row

    def score_body(c, carry):
        k0 = pl.multiple_of(c * kc, kc)
        k_i = ki_ref[pl.ds(k0, kc), :]
        acc = jnp.zeros((kc, Q_BLOCK), F32)
        for g in range(IDX_HEADS // hg):
            q_h = _stack_heads(qi_ref, g * hg, hg, IDX_DIM)
            rel = jnp.maximum(lax.dot_general(k_i, q_h, DN_T, preferred_element_type=F32), 0.0)
            for h in range(hg):
                col = GATE_COLS + g * hg + h
                acc = acc + rel[:, h * Q_BLOCK:(h + 1) * Q_BLOCK] * w_t[col:col + 1, :]
        bits = lax.bitcast_convert_type(acc, jnp.int32)
        skey = bits ^ (lax.shift_right_arithmetic(bits, 31) & np.int32(0x7FFFFFFF))
        key_ref[c] = jnp.where(causal_t(c), skey, INT_MIN)
        return carry

    lax.fori_loop(0, n_chunks, score_body, 0)

    def select_body(it, thr_u):
        bit = lax.shift_left(jnp.int32(1), 31 - it)
        cand = thr_u | bit
        cand_key = cand ^ INT_MIN

        def count_body(c, cnt):
            hit = jnp.where(key_ref[c] >= cand_key, 1.0, 0.0)
            return cnt + jnp.sum(hit.reshape(kc // cnt_rows, cnt_rows, Q_BLOCK), axis=0)

        cnt = lax.fori_loop(0, n_chunks, count_body, jnp.zeros((cnt_rows, Q_BLOCK), F32))
        return jnp.where(jnp.sum(cnt, axis=0, keepdims=True) >= k_top, cand, thr_u)

    thr_u = lax.fori_loop(0, 32, select_body, jnp.zeros((1, Q_BLOCK), jnp.int32))
    thr = thr_u ^ INT_MIN

    _flash_init(m_ref, acc_ref)
    q_groups = [_stack_heads(qb_ref, g * DSA_HPG, DSA_HPG, DSA_DK) for g in range(DSA_KV_HEADS)]

    def attn_chunks(chunks):
        scores = []
        for c in chunks:
            k0 = pl.multiple_of(c * kc, kc)
            scores.append([
                lax.dot_general(q_groups[g], kb_ref[pl.ds(k0, kc), g * DSA_DK:(g + 1) * DSA_DK], DN_T,
                                preferred_element_type=F32) for g in range(DSA_KV_HEADS)])
        for c, s_c in zip(chunks, scores):
            k0 = pl.multiple_of(c * kc, kc)
            bias = jnp.transpose(_mask_bias((key_ref[c] >= thr) & causal_t(c)))
            for g in range(DSA_KV_HEADS):
                v_blk = vb_ref[pl.ds(k0, kc), g * DSA_DV:(g + 1) * DSA_DV]
                _flash_step(s_c[g], bias, _with_ones(v_blk), DSA_HPG, m_ref.at[g], acc_ref.at[g])

    _chunk_loop(n_chunks, attn_chunks)
    for g in range(DSA_KV_HEADS):
        o = _normalize(acc_ref[g], DSA_DV)
        for h in range(DSA_HPG):
            head = g * DSA_HPG + h
            o_ref[:, head * DSA_DV:(head + 1) * DSA_DV] = o[h * Q_BLOCK:(h + 1) * Q_BLOCK].astype(o_ref.dtype)


def _dsa_attention(p128, pv, small, cfg):
    b, s = cfg.batch, cfg.seq
    n_qb = s // Q_BLOCK
    kc = _tile(s, 512)
    k_top = min(DSA_TOPK, s // 4)
    qiw = IDX_HEADS * IDX_DIM
    qbw = DSA_HEADS * DSA_DK
    kbw = DSA_KV_HEADS * DSA_DK
    rows = DSA_HPG * Q_BLOCK
    est = (2 * (_nbytes((Q_BLOCK, qiw + qbw), BF16) + _nbytes((s, IDX_DIM + 2 * kbw), BF16))
           + _nbytes((Q_BLOCK, s), jnp.int32) + 12 * _nbytes((8 * Q_BLOCK, kc), F32) + (4 << 20))
    return pl.pallas_call(
        functools.partial(_dsa_kernel, kc=kc, k_top=k_top),
        grid=(b, n_qb),
        in_specs=[
            pl.BlockSpec((Q_BLOCK, qiw), lambda bi, i: (bi * n_qb + i, 0)),
            pl.BlockSpec((Q_BLOCK, V7X_LANES), lambda bi, i: (bi * n_qb + i, 0)),
            pl.BlockSpec((s, IDX_DIM), lambda bi, i: (bi, (qiw + qbw + kbw) // IDX_DIM)),
            pl.BlockSpec((Q_BLOCK, qbw), lambda bi, i: (bi * n_qb + i, qiw // qbw)),
            pl.BlockSpec((s, kbw), lambda bi, i: (bi, (qiw + qbw) // kbw)),
            pl.BlockSpec((s, kbw), lambda bi, i: (bi, 1)),
        ],
        out_specs=pl.BlockSpec((Q_BLOCK, DSA_HEADS * DSA_DV), lambda bi, i: (bi * n_qb + i, 0)),
        out_shape=jax.ShapeDtypeStruct((b * s, DSA_HEADS * DSA_DV), BF16),
        scratch_shapes=[
            pltpu.VMEM((s // kc, kc, Q_BLOCK), jnp.int32),
            pltpu.VMEM((DSA_KV_HEADS, rows, V7X_LANES), F32),
            pltpu.VMEM((DSA_KV_HEADS, rows, DSA_DV + V7X_LANES), F32),
        ],
        compiler_params=_params(("parallel", "arbitrary"), est),
        name="dsa_attention",
    )(p128, small, p128, p128, p128, pv)


def _pad_heads(w, heads, dk, dkp):
    d = w.shape[0]
    return jnp.pad(w.reshape(d, heads, dk), ((0, 0), (0, 0), (0, dkp - dk))).reshape(d, heads * dkp)


def _split_kv(w, heads, dk, dv):
    d = w.shape[0]
    w3 = w.reshape(d, heads, dk + dv)
    return w3[:, :, :dk].reshape(d, heads * dk), w3[:, :, dk:].reshape(d, heads * dv)


def _mixer_weights(w_in, d_model):
    sizes = (
        NSA_HEADS * NSA_DK, NSA_GROUPS * (NSA_DK + NSA_DV), NSA_GROUPS * (NSA_DK + NSA_DV),
        NSA_GROUPS * (NSA_DK + NSA_DV), 3 * NSA_HEADS, DSA_HEADS * DSA_DK, DSA_KV_HEADS * (DSA_DK + DSA_DV),
        IDX_HEADS * IDX_DIM, IDX_DIM, IDX_HEADS, 2 * d_model,
    )
    offs = np.cumsum((0,) + sizes)
    assert offs[-1] == w_in.shape[1]
    w_bf = w_in.astype(BF16)
    q_a, kv_c, kv_s, kv_w, gate_a, q_b, kv_b, q_i, k_i, w_i, merge = (
        w_bf[:, offs[n]:offs[n + 1]] for n in range(len(sizes)))
    k_c, v_c = _split_kv(kv_c, NSA_GROUPS, NSA_DK, NSA_DV)
    k_s, v_s = _split_kv(kv_s, NSA_GROUPS, NSA_DK, NSA_DV)
    k_w, v_w = _split_kv(kv_w, NSA_GROUPS, NSA_DK, NSA_DV)
    k_b, v_b = _split_kv(kv_b, DSA_KV_HEADS, DSA_DK, DSA_DV)
    w192 = jnp.concatenate([
        _pad_heads(q_a, NSA_HEADS, NSA_DK, NSA_DKP), _pad_heads(k_s, NSA_GROUPS, NSA_DK, NSA_DKP),
        _pad_heads(k_w, NSA_GROUPS, NSA_DK, NSA_DKP)], axis=1).astype(BF16)
    scale192 = jnp.concatenate([
        jnp.full((NSA_HEADS * NSA_DKP,), NSA_DK ** -0.5 * LOG2E, F32), jnp.ones((2 * NSA_GROUPS * NSA_DKP,), F32)])
    n128 = IDX_HEADS * IDX_DIM + DSA_HEADS * DSA_DK + DSA_KV_HEADS * DSA_DK + IDX_DIM
    n128_pad = -(-n128 // 1024) * 1024
    w128 = jnp.pad(jnp.concatenate([q_i, q_b, k_b, k_i], axis=1), ((0, 0), (0, n128_pad - n128))).astype(BF16)
    scale128 = jnp.concatenate([
        jnp.full((IDX_HEADS * IDX_DIM,), IDX_DIM ** -0.5, F32),
        jnp.full((DSA_HEADS * DSA_DK,), DSA_DK ** -0.5 * LOG2E, F32),
        jnp.ones((n128_pad - IDX_HEADS * IDX_DIM - DSA_HEADS * DSA_DK,), F32)])
    wv = jnp.concatenate([v_s, v_w, v_b], axis=1).astype(BF16)
    wc = jnp.concatenate([k_c, v_c], axis=1).astype(BF16)
    small_n = GATE_COLS + IDX_HEADS
    wsmall = jnp.pad(jnp.concatenate([gate_a, w_i], axis=1), ((0, 0), (0, V7X_LANES - small_n))).astype(BF16)
    return w192, scale192, w128, scale128, wv, wc, wsmall, merge.astype(BF16)


def _ffn_up_kernel(h_ref, w1_ref, w3_ref, o_ref, w1b_ref, w3b_ref):
    @pl.when(pl.program_id(1) == 0)
    def _():
        w1b_ref[...] = w1_ref[...].astype(BF16)
        w3b_ref[...] = w3_ref[...].astype(BF16)

    h = h_ref[...]
    a = jnp.dot(h, w1b_ref[...], preferred_element_type=F32)
    b = jnp.dot(h, w3b_ref[...], preferred_element_type=F32)
    o_ref[...] = (jax.nn.silu(a) * b).astype(o_ref.dtype)


def _ffn_up(h, w1, w3, cfg):
    t, d = h.shape
    ff = w1.shape[1]
    tm = _tile(cfg.seq, 1024)
    tn = _tile(ff, 256)
    est = (2 * _nbytes((tm, d), BF16) + 4 * _nbytes((d, tn), F32) + 3 * _nbytes((d, tn), BF16)
           + 2 * _nbytes((tm, tn), BF16) + 4 * _nbytes((tm, tn), F32) + (2 << 20))
    return pl.pallas_call(
        _ffn_up_kernel,
        grid=(ff // tn, t // tm),
        in_specs=[
            pl.BlockSpec((tm, d), lambda j, i: (i, 0)),
            pl.BlockSpec((d, tn), lambda j, i: (0, j)),
            pl.BlockSpec((d, tn), lambda j, i: (0, j)),
        ],
        out_specs=pl.BlockSpec((tm, tn), lambda j, i: (i, j)),
        out_shape=jax.ShapeDtypeStruct((t, ff), BF16),
        scratch_shapes=[pltpu.VMEM((d, tn), BF16), pltpu.VMEM((d, tn), BF16)],
        compiler_params=_params(("parallel", "arbitrary"), est),
        name="ffn_up",
    )(h, w1, w3)


def _ffn(x2, gain, mod3, idx, w1, w3, w2, cfg):
    t, d = x2.shape
    ff = w1.shape[1]
    h = _norm_mod(x2, gain, mod3, idx, idx + 1, cfg)
    u = _ffn_up(h, w1, w3, cfg)
    tm = _tile(cfg.seq, 512)
    nsb = cfg.seq // tm
    tn = _tile(d, 1024)
    epi = [(x2, (tm, tn), lambda i, j: (i, j)),
           (mod3, (1, 1, tn), lambda i, j: ((i // nsb) * N_MOD + idx + 2, 0, j))]
    return _matmul([u], [w2.astype(BF16)], [(0, 0)], functools.partial(_epi_residual, coef=0.5), epi, d, F32,
                   tm=tm, tn=tn, tk=_tile(ff, 5632), name="ffn_down")


def _mixer(x2, gain, mod3, w_in, cmp_pe_k, cmp_w1_k, cmp_w2_k, cmp_pe_v, cmp_w1_v, cmp_w2_v,
           proj_a, proj_b, w_out, cfg):
    b, s, d = cfg.batch, cfg.seq, cfg.d_model
    w192, scale192, w128, scale128, wv, wc, wsmall, wmerge = _mixer_weights(w_in, d)
    h = _norm_mod(x2, gain, mod3, 3, 4, cfg)
    p192 = _project_rope(h, w192, scale192, NSA_DK, NSA_DKP, cfg, "proj_rope192")
    p128 = _project_rope(h, w128, scale128, DSA_DK, DSA_DK, cfg, "proj_rope128")
    pv = _project(h, wv, _epi_plain, BF16, cfg, "proj_values")
    pc = _project(h, wc, _epi_plain, F32, cfg, "proj_cmp")
    small = _project(h, wsmall, _epi_plain, F32, cfg, "proj_small")
    gates_m = _project(h, wmerge, _epi_sigmoid, F32, cfg, "proj_merge")

    n_rows = s // CMP_STRIDE
    kdim = NSA_GROUPS * NSA_DK

    def chunks(cols, dd):
        xg = cols.reshape(b, n_rows, CMP_STRIDE, NSA_GROUPS, dd)
        return jnp.transpose(xg, (0, 3, 1, 2, 4)).reshape(b * NSA_GROUPS, n_rows, CMP_STRIDE * dd)

    cmp_pos = jnp.arange(n_rows) * CMP_STRIDE + CMP_BLOCK - 1
    k_cmp = _compress(chunks(pc[:, :kdim], NSA_DK), cmp_pe_k, cmp_w1_k, cmp_w2_k, NSA_DK, NSA_DKP, cmp_pos,
                      "compress_k").reshape(b, NSA_GROUPS, n_rows, NSA_DKP)
    v_cmp = _compress(chunks(pc[:, kdim:], NSA_DV), cmp_pe_v, cmp_w1_v, cmp_w2_v, NSA_DV, NSA_DV, None,
                      "compress_v").reshape(b, NSA_GROUPS, n_rows, NSA_DV)

    o_a = _nsa_attention(p192, pv, small, k_cmp, v_cmp, cfg)
    o_b = _dsa_attention(p128, pv, small, cfg)

    tm = _tile(s, 512)
    tn = _tile(d, 1024)
    nsb = s // tm
    epi = [(gates_m, (tm, tn), lambda i, j: (i, j)), (gates_m, (tm, tn), lambda i, j: (i, j + d // tn))]
    y = _matmul([o_a, o_b], [proj_a.astype(BF16), proj_b.astype(BF16)], [(0, 0), (1, 1)], _epi_merge, epi,
                d, BF16, tm=tm, tn=tn, tk=o_a.shape[1], name="merge_proj")
    epi = [(x2, (tm, tn), lambda i, j: (i, j)),
           (mod3, (1, 1, tn), lambda i, j: ((i // nsb) * N_MOD + 5, 0, j))]
    return _matmul([y], [w_out.astype(BF16)], [(0, 0)], functools.partial(_epi_residual, coef=1.0), epi,
                   d, F32, tm=tm, tn=tn, tk=d, name="out_proj")


def _layer(x, c, ada_w, ada_b, norm_ffn1, ffn1_w1, ffn1_w3, ffn1_w2, norm_mix, w_in,
           cmp_pe_k, cmp_w1_k, cmp_w2_k, cmp_pe_v, cmp_w1_v, cmp_w2_v, proj_a, proj_b, w_out,
           norm_ffn2, ffn2_w1, ffn2_w3, ffn2_w2, norm_final):
    b, s, d = x.shape
    cfg = Cfg(b, s, d, ffn1_w1.shape[-1])
    assert s % Q_BLOCK == 0 and s >= WINDOW + Q_BLOCK and d % V7X_LANES == 0
    x2 = x.reshape(b * s, d)
    c_pad = jnp.pad(c, ((0, 8 - b), (0, 0)))
    for l in range(ada_w.shape[0]):
        mod = _adaln(c_pad, ada_w[l], ada_b[l].reshape(1, -1))
        mod3 = mod[:b].reshape(b * N_MOD, 1, d)
        x2 = _ffn(x2, norm_ffn1[l], mod3, 0, ffn1_w1[l], ffn1_w3[l], ffn1_w2[l], cfg)
        x2 = _mixer(x2, norm_mix[l], mod3, w_in[l], cmp_pe_k[l], cmp_w1_k[l], cmp_w2_k[l],
                    cmp_pe_v[l], cmp_w1_v[l], cmp_w2_v[l], proj_a[l], proj_b[l], w_out[l], cfg)
        x2 = _ffn(x2, norm_ffn2[l], mod3, 6, ffn2_w1[l], ffn2_w3[l], ffn2_w2[l], cfg)
    return _final_norm(x2, norm_final).reshape(b, s, d)


def kernel(x, c, ada_w, ada_b, norm_ffn1, ffn1_w1, ffn1_w3, ffn1_w2, norm_mix, w_in, cmp_pe_k, cmp_w1_k, cmp_w2_k, cmp_pe_v, cmp_w1_v, cmp_w2_v, proj_a, proj_b, w_out, norm_ffn2, ffn2_w1, ffn2_w3, ffn2_w2, norm_final):
    return _layer(x, c, ada_w, ada_b, norm_ffn1, ffn1_w1, ffn1_w3, ffn1_w2, norm_mix, w_in,
                  cmp_pe_k, cmp_w1_k, cmp_w2_k, cmp_pe_v, cmp_w1_v, cmp_w2_v, proj_a, proj_b, w_out,
                  norm_ffn2, ffn2_w1, ffn2_w3, ffn2_w2, norm_final)
```

```python
import functools
import math
from typing import NamedTuple

import numpy as np
import jax
import jax.numpy as jnp
from jax import lax
from jax.experimental import pallas as pl
from jax.experimental.pallas import tpu as pltpu

F32 = jnp.float32
BF16 = jnp.bfloat16

V7X_LANES = 128
V7X_VMEM_BYTES = 64 * 1024 * 1024
V7X_VMEM_BUDGET = 56 * 1024 * 1024

EPS = 1e-6
ROPE_THETA = 500000.0
ROPE_FRACTION = 4
N_MOD = 9
Q_BLOCK = 128

NSA_HEADS = 16
NSA_GROUPS = 2
NSA_HPG = NSA_HEADS // NSA_GROUPS
NSA_DK = 192
NSA_DKP = 256
NSA_DV = 128
CMP_BLOCK = 32
CMP_STRIDE = 16
SEL_BLOCK = 64
SEL_TOP = 16
WINDOW = 512

DSA_HEADS = 16
DSA_KV_HEADS = 4
DSA_HPG = DSA_HEADS // DSA_KV_HEADS
DSA_DK = 128
DSA_DV = 128
IDX_HEADS = 32
IDX_DIM = 128
DSA_TOPK = 256

GATE_COLS = 3 * NSA_HEADS
INT_MIN = np.int32(-(2**31))
NEG_INF = float("-inf")
LOG2E = math.log2(math.e)
TINY = float(np.finfo(np.float32).tiny)
DN_T = (((1,), (1,)), ((), ()))


class Cfg(NamedTuple):
    batch: int
    seq: int
    d_model: int
    d_ff: int


def _tile(n, pref):
    if n <= pref:
        return n
    t = (pref // V7X_LANES) * V7X_LANES
    while t > V7X_LANES and n % t:
        t -= V7X_LANES
    assert n % t == 0, (n, pref)
    return t


def _nbytes(shape, dtype):
    return int(np.prod(shape)) * jnp.dtype(dtype).itemsize


def _params(dims, est_bytes):
    limit = int(min(V7X_VMEM_BUDGET, max(est_bytes, 16 * 1024 * 1024)))
    return pltpu.CompilerParams(dimension_semantics=dims, vmem_limit_bytes=limit)


def _adaln_kernel(c_ref, w_ref, b_ref, o_ref, acc_ref, *, nk):
    k = pl.program_id(1)

    @pl.when(k == 0)
    def _():
        acc_ref[...] = jnp.zeros_like(acc_ref)

    c = c_ref[...]
    c_act = (c * jax.nn.sigmoid(c)).astype(BF16)
    acc_ref[...] += jnp.dot(c_act, w_ref[...].astype(BF16), preferred_element_type=F32)

    @pl.when(k == nk - 1)
    def _():
        o_ref[...] = acc_ref[...] + b_ref[...]


def _adaln(c_pad, w, b):
    m, kdim = c_pad.shape
    n = w.shape[1]
    tn, tk = _tile(n, 2048), _tile(kdim, 1024)
    nk = kdim // tk
    est = 2 * _nbytes((tk, tn), F32) + _nbytes((tk, tn), BF16) + 4 * _nbytes((m, tn), F32) + (2 << 20)
    return pl.pallas_call(
        functools.partial(_adaln_kernel, nk=nk),
        grid=(n // tn, nk),
        in_specs=[
            pl.BlockSpec((m, tk), lambda j, k: (0, k)),
            pl.BlockSpec((tk, tn), lambda j, k: (k, j)),
            pl.BlockSpec((1, tn), lambda j, k: (0, j)),
        ],
        out_specs=pl.BlockSpec((m, tn), lambda j, k: (0, j)),
        out_shape=jax.ShapeDtypeStruct((m, n), F32),
        scratch_shapes=[pltpu.VMEM((m, tn), F32)],
        compiler_params=_params(("parallel", "arbitrary"), est),
        name="adaln",
    )(c_pad, w, b)


def _norm_mod_kernel(x_ref, g_ref, sh_ref, sc_ref, o_ref):
    x = x_ref[...]
    y = x * lax.rsqrt(jnp.mean(x * x, axis=-1, keepdims=True) + EPS)
    y = y * g_ref[...]
    o_ref[...] = (y * (1.0 + sc_ref[0]) + sh_ref[0]).astype(o_ref.dtype)


def _norm_mod(x2, gain, mod3, shift_idx, scale_idx, cfg):
    t, d = x2.shape
    tm = _tile(cfg.seq, 256)
    nsb = cfg.seq // tm
    est = 4 * _nbytes((tm, d), F32) + 2 * _nbytes((tm, d), BF16) + (2 << 20)
    return pl.pallas_call(
        _norm_mod_kernel,
        grid=(t // tm,),
        in_specs=[
            pl.BlockSpec((tm, d), lambda i: (i, 0)),
            pl.BlockSpec((1, d), lambda i: (0, 0)),
            pl.BlockSpec((1, 1, d), lambda i: ((i // nsb) * N_MOD + shift_idx, 0, 0)),
            pl.BlockSpec((1, 1, d), lambda i: ((i // nsb) * N_MOD + scale_idx, 0, 0)),
        ],
        out_specs=pl.BlockSpec((tm, d), lambda i: (i, 0)),
        out_shape=jax.ShapeDtypeStruct((t, d), BF16),
        compiler_params=_params(("parallel",), est),
        name="norm_mod",
    )(x2, gain.reshape(1, d), mod3, mod3)


def _final_norm_kernel(x_ref, g_ref, o_ref):
    x = x_ref[...]
    y = x * lax.rsqrt(jnp.mean(x * x, axis=-1, keepdims=True) + EPS)
    o_ref[...] = y * g_ref[...]


def _final_norm(x2, gain):
    t, d = x2.shape
    tm = _tile(t, 256)
    est = 6 * _nbytes((tm, d), F32) + (2 << 20)
    return pl.pallas_call(
        _final_norm_kernel,
        grid=(t // tm,),
        in_specs=[pl.BlockSpec((tm, d), lambda i: (i, 0)), pl.BlockSpec((1, d), lambda i: (0, 0))],
        out_specs=pl.BlockSpec((tm, d), lambda i: (i, 0)),
        out_shape=jax.ShapeDtypeStruct((t, d), F32),
        compiler_params=_params(("parallel",), est),
        name="final_norm",
    )(x2, gain.reshape(1, d))


MM_SUB = 256


def _mm_kernel(*refs, n_a, n_w, pairs, n_epi, epilogue, sub):
    a_refs = refs[:n_a]
    w_refs = refs[n_a:n_a + n_w]
    e_refs = refs[n_a + n_w:n_a + n_w + n_epi]
    o_ref = refs[n_a + n_w + n_epi]
    for c in range(o_ref.shape[1] // sub):
        cols = slice(c * sub, (c + 1) * sub)
        accs = [jnp.dot(a_refs[ai][...], w_refs[wi][:, cols], preferred_element_type=F32) for ai, wi in pairs]
        o_ref[:, cols] = epilogue(accs, e_refs, cols).astype(o_ref.dtype)


def _matmul(a_list, w_list, pairs, epilogue, epi_inputs, n_out, out_dtype, *, tm, tn, name):
    m = a_list[0].shape[0]
    assert m % tm == 0 and n_out % tn == 0
    sub = MM_SUB if tn % MM_SUB == 0 else tn
    in_specs, est = [], 0
    for a in a_list:
        in_specs.append(pl.BlockSpec((tm, a.shape[1]), lambda i, j: (i, 0)))
        est += 2 * _nbytes((tm, a.shape[1]), a.dtype)
    for w in w_list:
        in_specs.append(pl.BlockSpec((w.shape[0], tn), lambda i, j: (0, j)))
        est += 2 * _nbytes((w.shape[0], tn), w.dtype)
    for arr, bshape, imap in epi_inputs:
        in_specs.append(pl.BlockSpec(bshape, imap))
        est += 2 * _nbytes(bshape, arr.dtype)
    est += 2 * _nbytes((tm, tn), out_dtype) + 4 * (len(pairs) + 3) * _nbytes((tm, sub), F32) + (2 << 20)
    kern = functools.partial(_mm_kernel, n_a=len(a_list), n_w=len(w_list), pairs=tuple(pairs),
                             n_epi=len(epi_inputs), epilogue=epilogue, sub=sub)
    return pl.pallas_call(
        kern,
        grid=(m // tm, n_out // tn),
        in_specs=in_specs,
        out_specs=pl.BlockSpec((tm, tn), lambda i, j: (i, j)),
        out_shape=jax.ShapeDtypeStruct((m, n_out), out_dtype),
        compiler_params=_params(("parallel", "parallel"), est),
        name=name,
    )(*a_list, *w_list, *[e[0] for e in epi_inputs])


def _epi_plain(accs, e_refs, cols):
    return accs[0]


def _epi_sigmoid(accs, e_refs, cols):
    return jax.nn.sigmoid(accs[0])


def _epi_merge(accs, e_refs, cols):
    return e_refs[0][:, cols].astype(F32) * accs[0] + e_refs[1][:, cols].astype(F32) * accs[1]


def _epi_residual(accs, e_refs, cols, *, coef):
    return e_refs[0][:, cols] + (coef * e_refs[1][0][:, cols]) * accs[0]


def _rope_apply(a, cos, sin_lo, sin_hi, half):
    width = a.shape[-1]
    reps = (1, width // cos.shape[-1])
    return (a * jnp.tile(cos, reps) + pltpu.roll(a, width - half, 1) * jnp.tile(sin_lo, reps)
            + pltpu.roll(a, half, 1) * jnp.tile(sin_hi, reps))


def _epi_rope(accs, e_refs, cols, *, half):
    out = _rope_apply(accs[0], e_refs[0][...], e_refs[1][...], e_refs[2][...], half)
    return out * e_refs[3][:, cols]


def _rope_tables(pos, dk, head_w):
    r = dk // ROPE_FRACTION
    half = r // 2
    n = pos.shape[0]
    inv = ROPE_THETA ** (-jnp.arange(0, r, 2, dtype=F32) / r)
    ang = pos.astype(F32)[:, None] * inv[None, :]
    cos, sin = jnp.cos(ang), jnp.sin(ang)
    cos_t = jnp.concatenate([cos, cos, jnp.ones((n, head_w - r), F32)], axis=1)
    sin_lo = jnp.concatenate([-sin, jnp.zeros((n, head_w - half), F32)], axis=1)
    sin_hi = jnp.concatenate([jnp.zeros((n, half), F32), sin, jnp.zeros((n, head_w - r), F32)], axis=1)
    return cos_t, sin_lo, sin_hi, half


def _project_rope(h, w, colscale, dk, head_w, cfg, name):
    n = w.shape[1]
    tm = _tile(cfg.seq, 1024)
    tn = _tile(n, 1024)
    nsb = cfg.seq // tm
    cos_t, sin_lo, sin_hi, half = _rope_tables(jnp.arange(cfg.seq), dk, head_w)
    epi = [(tb, (tm, head_w), lambda i, j: (i % nsb, 0)) for tb in (cos_t, sin_lo, sin_hi)]
    epi.append((colscale.reshape(1, n), (1, tn), lambda i, j: (0, j)))
    return _matmul([h], [w], [(0, 0)], functools.partial(_epi_rope, half=half), epi, n, BF16,
                   tm=tm, tn=tn, name=name)


def _project(h, w, epilogue, out_dtype, cfg, name):
    n = w.shape[1]
    return _matmul([h], [w], [(0, 0)], epilogue, [], n, out_dtype,
                   tm=_tile(cfg.seq, 1024), tn=_tile(n, 1024), name=name)


def _gelu_tanh(x):
    return 0.5 * x * (1.0 + jnp.tanh(math.sqrt(2.0 / math.pi) * (x + 0.044715 * (x * x * x))))


def _compress_kernel(x_ref, pe_ref, w1_ref, w2_ref, *rest, half):
    o_ref = rest[-1]
    x = x_ref[0]
    n_rows, width = x.shape
    a_lo = (x + pe_ref[0:1, :]).astype(BF16)
    a_hi = (x + pe_ref[1:2, :]).astype(BF16)
    p = jnp.dot(a_lo, w1_ref[0:width, :], preferred_element_type=F32)
    q = jnp.dot(a_hi, w1_ref[width:2 * width, :], preferred_element_type=F32)
    hid = _gelu_tanh(p + pltpu.roll(q, n_rows - 1, 0))
    y = jnp.dot(hid.astype(BF16), w2_ref[...], preferred_element_type=F32)
    if half:
        y = _rope_apply(y, rest[0][...], rest[1][...], rest[2][...], half)
    o_ref[0] = y.astype(o_ref.dtype)


def _compress(x_chunks, pe, w1, w2, d, d_out, rope_pos, name):
    bg, n_rows, width = x_chunks.shape
    hid_w = ((d + V7X_LANES - 1) // V7X_LANES) * V7X_LANES
    pe2 = pe.reshape(2, width)
    w1p = jnp.pad(w1.astype(BF16), ((0, 0), (0, hid_w - d)))
    w2p = jnp.pad(w2.astype(BF16), ((0, hid_w - d), (0, d_out - d)))
    ins = [x_chunks, pe2, w1p, w2p]
    in_specs = [
        pl.BlockSpec((1, n_rows, width), lambda i: (i, 0, 0)),
        pl.BlockSpec((2, width), lambda i: (0, 0)),
        pl.BlockSpec((2 * width, hid_w), lambda i: (0, 0)),
        pl.BlockSpec((hid_w, d_out), lambda i: (0, 0)),
    ]
    half = 0
    if rope_pos is not None:
        cos_t, sin_lo, sin_hi, half = _rope_tables(rope_pos, d, d_out)
        ins += [cos_t, sin_lo, sin_hi]
        in_specs += [pl.BlockSpec((n_rows, d_out), lambda i: (0, 0))] * 3
    est = (2 * _nbytes((n_rows, width), F32) + 2 * _nbytes((n_rows, width), BF16)
           + 2 * _nbytes((2 * width, hid_w), BF16) + (8 << 20))
    return pl.pallas_call(
        functools.partial(_compress_kernel, half=half),
        grid=(bg,),
        in_specs=in_specs,
        out_specs=pl.BlockSpec((1, n_rows, d_out), lambda i: (i, 0, 0)),
        out_shape=jax.ShapeDtypeStruct((bg, n_rows, d_out), BF16),
        compiler_params=_params(("parallel",), est),
        name=name,
    )(*ins)


def _stack_heads(ref, first, count, width):
    return jnp.concatenate([ref[:, (first + h) * width:(first + h + 1) * width] for h in range(count)], axis=0)


def _mask_bias(mask):
    return jnp.where(mask, 0.0, NEG_INF)


def _with_ones(v):
    return jnp.concatenate([v, jnp.ones((v.shape[0], V7X_LANES), v.dtype)], axis=1)


def _exp2_masked(s3, bias2):
    z = s3 + bias2[None]
    m = jnp.max(z, axis=-1, keepdims=True)
    m = jnp.where(m > NEG_INF, m, 0.0)
    return jnp.exp2(z - m)


def _normalize(pv, dv):
    return pv[:, :dv] / jnp.maximum(pv[:, dv:dv + V7X_LANES], TINY)


def _flash_init(m_ref, acc_ref):
    m_ref[...] = jnp.full(m_ref.shape, NEG_INF, F32)
    acc_ref[...] = jnp.zeros(acc_ref.shape, F32)


def _flash_step(s, bias2, v_ones, heads, m_ref, acc_ref):
    rows, kc = s.shape
    z = (s.reshape(heads, rows // heads, kc) + bias2[None]).reshape(rows, kc)
    m_old = m_ref[...]
    m_new = jnp.maximum(m_old, jnp.max(z, axis=-1, keepdims=True))
    m_safe = jnp.where(m_new > NEG_INF, m_new, 0.0)
    alpha = jnp.exp2(m_old - m_safe)
    p = jnp.exp2(z - jnp.tile(m_safe, (1, kc // V7X_LANES)))
    pv = jnp.dot(p.astype(BF16), v_ones, preferred_element_type=F32)
    acc_ref[...] = jnp.tile(alpha, (1, acc_ref.shape[-1] // V7X_LANES)) * acc_ref[...] + pv
    m_ref[...] = m_new


def _chunk_loop(n_chunks, process):
    def pair_body(j, carry):
        process([2 * j, 2 * j + 1])
        return carry

    lax.fori_loop(0, n_chunks // 2, pair_body, 0)

    @pl.when(n_chunks % 2 == 1)
    def _():
        process([n_chunks - 1])


def _nsa_kernel(q_ref, gate_ref, kc_ref, vc_ref, ks_ref, vs_ref, kw_ref, vw_ref, aggt_ref, o_ref,
                m_ref, acc_ref, *, seq, kc, blk_pad):
    n_cmp = (seq - CMP_BLOCK) // CMP_STRIDE + 1
    n_blk = seq // SEL_BLOCK
    n_sel = min(SEL_TOP, n_blk)
    cmp_pad = aggt_ref.shape[1]
    q0 = pl.program_id(1) * Q_BLOCK
    t_col = q0 + lax.broadcasted_iota(jnp.int32, (Q_BLOCK, 1), 0)
    t_row = q0 + lax.broadcasted_iota(jnp.int32, (1, Q_BLOCK), 1)
    gates = jax.nn.sigmoid(gate_ref[...])

    cidx = lax.broadcasted_iota(jnp.int32, (Q_BLOCK, cmp_pad), 1)
    bias_c = _mask_bias((cidx * CMP_STRIDE + (CMP_BLOCK - 1) <= t_col) & (cidx < n_cmp))

    blk = lax.broadcasted_iota(jnp.int32, (n_blk, Q_BLOCK), 0)
    cur = lax.shift_right_logical(t_row, int(math.log2(SEL_BLOCK)))
    forced = (blk == 0) | (blk == cur) | (blk == cur - 1)
    valid = blk * SEL_BLOCK <= t_row

    win_k = WINDOW + Q_BLOCK
    ws = pl.multiple_of(jnp.maximum(q0 - WINDOW, 0), Q_BLOCK)
    diff = t_col - (ws + lax.broadcasted_iota(jnp.int32, (1, win_k), 1))
    bias_w = _mask_bias((diff >= 0) & (diff < WINDOW))
    n_chunks = (q0 + Q_BLOCK + kc - 1) // kc
    rows = NSA_HPG * Q_BLOCK

    q_groups = [_stack_heads(q_ref, g * NSA_HPG, NSA_HPG, NSA_DKP) for g in range(NSA_GROUPS)]
    o_cmp, sels = [], []
    for g in range(NSA_GROUPS):
        qg = q_groups[g]

        s = lax.dot_general(qg, kc_ref[0, g], DN_T, preferred_element_type=F32)
        e_c = _exp2_masked(s.reshape(NSA_HPG, Q_BLOCK, cmp_pad), bias_c)
        p_c = e_c / jnp.maximum(jnp.sum(e_c, axis=-1, keepdims=True), TINY)
        o_c = jnp.dot(p_c.reshape(rows, cmp_pad).astype(BF16), vc_ref[0, g], preferred_element_type=F32)

        p_sum = jnp.sum(p_c, axis=0)
        hi = p_sum.astype(BF16)
        r1 = p_sum - hi.astype(F32)
        mid = r1.astype(BF16)
        lo = (r1 - mid.astype(F32)).astype(BF16)
        aggt = aggt_ref[...]
        imp = (lax.dot_general(aggt, hi, DN_T, preferred_element_type=F32)
               + lax.dot_general(aggt, mid, DN_T, preferred_element_type=F32)
               + lax.dot_general(aggt, lo, DN_T, preferred_element_type=F32))
        val = jnp.where(forced, jnp.inf, jnp.where(valid, imp, NEG_INF))
        rank = jnp.zeros((n_blk, Q_BLOCK), F32)
        for ii in range(n_blk):
            row = val[ii:ii + 1, :]
            beats = (row > val) | ((row == val) & (blk > ii))
            rank = rank + jnp.where(beats, 1.0, 0.0)
        sel_t = jnp.where((rank < n_sel) & (val > NEG_INF), 1.0, 0.0)
        if blk_pad > n_blk:
            sel_t = jnp.concatenate([sel_t, jnp.zeros((blk_pad - n_blk, Q_BLOCK), F32)], axis=0)
        sels.append(jnp.transpose(sel_t).astype(BF16))
        o_cmp.append(o_c)

    _flash_init(m_ref, acc_ref)

    def sel_chunks(chunks):
        scores = []
        for c in chunks:
            k0 = pl.multiple_of(c * kc, kc)
            scores.append([
                lax.dot_general(q_groups[g], ks_ref[pl.ds(k0, kc), g * NSA_DKP:(g + 1) * NSA_DKP], DN_T,
                                preferred_element_type=F32) for g in range(NSA_GROUPS)])
        for c, s_c in zip(chunks, scores):
            k0 = pl.multiple_of(c * kc, kc)
            kpos_b = k0 + lax.broadcasted_iota(jnp.int32, (blk_pad, kc), 1)
            expand = jnp.where(
                lax.shift_right_logical(kpos_b, int(math.log2(SEL_BLOCK)))
                == lax.broadcasted_iota(jnp.int32, (blk_pad, kc), 0), 1.0, 0.0).astype(BF16)
            causal = k0 + lax.broadcasted_iota(jnp.int32, (1, kc), 1) <= t_col
            for g in range(NSA_GROUPS):
                v_blk = vs_ref[pl.ds(k0, kc), g * NSA_DV:(g + 1) * NSA_DV]
                sel_keys = jnp.dot(sels[g], expand, preferred_element_type=F32)
                bias = _mask_bias((sel_keys > 0.5) & causal)
                _flash_step(s_c[g], bias, _with_ones(v_blk), NSA_HPG, m_ref.at[g], acc_ref.at[g])

    _chunk_loop(n_chunks, sel_chunks)

    for g in range(NSA_GROUPS):
        qg, o_c = q_groups[g], o_cmp[g]
        o_s = _normalize(acc_ref[g], NSA_DV)

        k_win = kw_ref[pl.ds(ws, win_k), g * NSA_DKP:(g + 1) * NSA_DKP]
        v_win = vw_ref[pl.ds(ws, win_k), g * NSA_DV:(g + 1) * NSA_DV]
        s_w = lax.dot_general(qg, k_win, DN_T, preferred_element_type=F32)
        e_w = _exp2_masked(s_w.reshape(NSA_HPG, Q_BLOCK, win_k), bias_w)
        o_w = _normalize(jnp.dot(e_w.reshape(rows, win_k).astype(BF16), _with_ones(v_win),
                                 preferred_element_type=F32), NSA_DV)

        for h in range(NSA_HPG):
            c0 = (g * NSA_HPG + h) * 3
            hr = slice(h * Q_BLOCK, (h + 1) * Q_BLOCK)
            o = (gates[:, c0:c0 + 1] * o_c[hr] + gates[:, c0 + 1:c0 + 2] * o_s[hr]
                 + gates[:, c0 + 2:c0 + 3] * o_w[hr])
            head = g * NSA_HPG + h
            o_ref[:, head * NSA_DV:(head + 1) * NSA_DV] = o.astype(o_ref.dtype)


def _agg_t(seq, cmp_pad):
    n_cmp = (seq - CMP_BLOCK) // CMP_STRIDE + 1
    n_blk = seq // SEL_BLOCK
    r = SEL_BLOCK // CMP_STRIDE
    c = CMP_BLOCK // CMP_STRIDE
    j = np.arange(n_blk)[:, None, None, None]
    i = np.arange(cmp_pad)[None, :, None, None]
    m = np.arange(r)[None, None, :, None]
    n = np.arange(c)[None, None, None, :]
    a = np.sum(i == r * j + m - n, axis=(2, 3)).astype(np.float32)
    a = a * (np.arange(cmp_pad)[None, :] < n_cmp)
    return jnp.asarray(a, dtype=BF16)


def _nsa_attention(p192, pv, small, k_cmp, v_cmp, cfg):
    b, s = cfg.batch, cfg.seq
    n_qb = s // Q_BLOCK
    kc = _tile(s, 512)
    cmp_pad = k_cmp.shape[2]
    blk_pad = max(V7X_LANES, s // SEL_BLOCK)
    qw = NSA_HEADS * NSA_DKP
    kw = NSA_GROUPS * NSA_DKP
    vw = NSA_GROUPS * NSA_DV
    rows = NSA_HPG * Q_BLOCK
    est = (2 * (_nbytes((Q_BLOCK, qw), BF16) + 2 * _nbytes((s, kw), BF16) + 2 * _nbytes((s, vw), BF16))
           + 3 * _nbytes((rows, V7X_LANES), F32) + 10 * _nbytes((rows, WINDOW + Q_BLOCK), F32) + (4 << 20))
    return pl.pallas_call(
        functools.partial(_nsa_kernel, seq=s, kc=kc, blk_pad=blk_pad),
        grid=(b, n_qb),
        in_specs=[
            pl.BlockSpec((Q_BLOCK, qw), lambda bi, i: (bi * n_qb + i, 0)),
            pl.BlockSpec((Q_BLOCK, V7X_LANES), lambda bi, i: (bi * n_qb + i, 0)),
            pl.BlockSpec((1, NSA_GROUPS, cmp_pad, NSA_DKP), lambda bi, i: (bi, 0, 0, 0)),
            pl.BlockSpec((1, NSA_GROUPS, cmp_pad, NSA_DV), lambda bi, i: (bi, 0, 0, 0)),
            pl.BlockSpec((s, kw), lambda bi, i: (bi, qw // kw)),
            pl.BlockSpec((s, vw), lambda bi, i: (bi, 0)),
            pl.BlockSpec((s, kw), lambda bi, i: (bi, qw // kw + 1)),
            pl.BlockSpec((s, vw), lambda bi, i: (bi, 1)),
            pl.BlockSpec((s // SEL_BLOCK, cmp_pad), lambda bi, i: (0, 0)),
        ],
        out_specs=pl.BlockSpec((Q_BLOCK, NSA_HEADS * NSA_DV), lambda bi, i: (bi * n_qb + i, 0)),
        out_shape=jax.ShapeDtypeStruct((b * s, NSA_HEADS * NSA_DV), BF16),
        scratch_shapes=[pltpu.VMEM((NSA_GROUPS, rows, V7X_LANES), F32),
                        pltpu.VMEM((NSA_GROUPS, rows, NSA_DV + V7X_LANES), F32)],
        compiler_params=_params(("parallel", "arbitrary"), est),
        name="nsa_attention",
    )(p192, small, k_cmp, v_cmp, p192, pv, p192, pv, _agg_t(s, cmp_pad))


def _dsa_kernel(qi_ref, w_ref, ki_ref, qb_ref, kb_ref, vb_ref, o_ref, key_ref, m_ref, acc_ref, *, kc, k_top):
    q0 = pl.program_id(1) * Q_BLOCK
    t_row = q0 + lax.broadcasted_iota(jnp.int32, (1, Q_BLOCK), 1)
    n_chunks = (q0 + Q_BLOCK + kc - 1) // kc
    w_t = jnp.transpose(w_ref[...] * (IDX_HEADS ** -0.5))
    hg = 8
    cnt_rows = 64

    def causal_t(c):
        return c * kc + lax.broadcasted_iota(jnp.int32, (kc, 1), 0) <= t_row

    def score_body(c, carry):
        k0 = pl.multiple_of(c * kc, kc)
        k_i = ki_ref[pl.ds(k0, kc), :]
        acc = jnp.zeros((kc, Q_BLOCK), F32)
        for g in range(IDX_HEADS // hg):
            q_h = _stack_heads(qi_ref, g * hg, hg, IDX_DIM)
            rel = jnp.maximum(lax.dot_general(k_i, q_h, DN_T, preferred_element_type=F32), 0.0)
            for h in range(hg):
                col = GATE_COLS + g * hg + h
                acc = acc + rel[:, h * Q_BLOCK:(h + 1) * Q_BLOCK] * w_t[col:col + 1, :]
        bits = lax.bitcast_convert_type(acc, jnp.int32)
        skey = bits ^ (lax.shift_right_arithmetic(bits, 31) & np.int32(0x7FFFFFFF))
        key_ref[c] = jnp.where(causal_t(c), skey, INT_MIN)
        return carry

    lax.fori_loop(0, n_chunks, score_body, 0)

    def select_body(it, thr_u):
        bit = lax.shift_left(jnp.int32(1), 31 - it)
        cand = thr_u | bit
        cand_key = cand ^ INT_MIN

        def count_body(c, cnt):
            hit = jnp.where(key_ref[c] >= cand_key, 1.0, 0.0)
            return cnt + jnp.sum(hit.reshape(kc // cnt_rows, cnt_rows, Q_BLOCK), axis=0)

        cnt = lax.fori_loop(0, n_chunks, count_body, jnp.zeros((cnt_rows, Q_BLOCK), F32))
        return jnp.where(jnp.sum(cnt, axis=0, keepdims=True) >= k_top, cand, thr_u)

    thr_u = lax.fori_loop(0, 32, select_body, jnp.zeros((1, Q_BLOCK), jnp.int32))
    thr = thr_u ^ INT_MIN

    _flash_init(m_ref, acc_ref)
    q_groups = [_stack_heads(qb_ref, g * DSA_HPG, DSA_HPG, DSA_DK) for g in range(DSA_KV_HEADS)]

    def attn_chunks(chunks):
        scores = []
        for c in chunks:
            k0 = pl.multiple_of(c * kc, kc)
            scores.append([
                lax.dot_general(q_groups[g], kb_ref[pl.ds(k0, kc), g * DSA_DK:(g + 1) * DSA_DK], DN_T,
                                preferred_element_type=F32) for g in range(DSA_KV_HEADS)])
        for c, s_c in zip(chunks, scores):
            k0 = pl.multiple_of(c * kc, kc)
            bias = jnp.transpose(_mask_bias((key_ref[c] >= thr) & causal_t(c)))
            for g in range(DSA_KV_HEADS):
                v_blk = vb_ref[pl.ds(k0, kc), g * DSA_DV:(g + 1) * DSA_DV]
                _flash_step(s_c[g], bias, _with_ones(v_blk), DSA_HPG, m_ref.at[g], acc_ref.at[g])

    _chunk_loop(n_chunks, attn_chunks)
    for g in range(DSA_KV_HEADS):
        o = _normalize(acc_ref[g], DSA_DV)
        for h in range(DSA_HPG):
            head = g * DSA_HPG + h
            o_ref[:, head * DSA_DV:(head + 1) * DSA_DV] = o[h * Q_BLOCK:(h + 1) * Q_BLOCK].astype(o_ref.dtype)


def _dsa_attention(p128, pv, small, cfg):
    b, s = cfg.batch, cfg.seq
    n_qb = s // Q_BLOCK
    kc = _tile(s, 512)
    k_top = min(DSA_TOPK, s // 4)
    qiw = IDX_HEADS * IDX_DIM
    qbw = DSA_HEADS * DSA_DK
    kbw = DSA_KV_HEADS * DSA_DK
    rows = DSA_HPG * Q_BLOCK
    est = (2 * (_nbytes((Q_BLOCK, qiw + qbw), BF16) + _nbytes((s, IDX_DIM + 2 * kbw), BF16))
           + _nbytes((Q_BLOCK, s), jnp.int32) + 12 * _nbytes((8 * Q_BLOCK, kc), F32) + (4 << 20))
    return pl.pallas_call(
        functools.partial(_dsa_kernel, kc=kc, k_top=k_top),
        grid=(b, n_qb),
        in_specs=[
            pl.BlockSpec((Q_BLOCK, qiw), lambda bi, i: (bi * n_qb + i, 0)),
            pl.BlockSpec((Q_BLOCK, V7X_LANES), lambda bi, i: (bi * n_qb + i, 0)),
            pl.BlockSpec((s, IDX_DIM), lambda bi, i: (bi, (qiw + qbw + kbw) // IDX_DIM)),
            pl.BlockSpec((Q_BLOCK, qbw), lambda bi, i: (bi * n_qb + i, qiw // qbw)),
            pl.BlockSpec((s, kbw), lambda bi, i: (bi, (qiw + qbw) // kbw)),
            pl.BlockSpec((s, kbw), lambda bi, i: (bi, 1)),
        ],
        out_specs=pl.BlockSpec((Q_BLOCK, DSA_HEADS * DSA_DV), lambda bi, i: (bi * n_qb + i, 0)),
        out_shape=jax.ShapeDtypeStruct((b * s, DSA_HEADS * DSA_DV), BF16),
        scratch_shapes=[
            pltpu.VMEM((s // kc, kc, Q_BLOCK), jnp.int32),
            pltpu.VMEM((DSA_KV_HEADS, rows, V7X_LANES), F32),
            pltpu.VMEM((DSA_KV_HEADS, rows, DSA_DV + V7X_LANES), F32),
        ],
        compiler_params=_params(("parallel", "arbitrary"), est),
        name="dsa_attention",
    )(p128, small, p128, p128, p128, pv)


def _pad_heads(w, heads, dk, dkp):
    d = w.shape[0]
    return jnp.pad(w.reshape(d, heads, dk), ((0, 0), (0, 0), (0, dkp - dk))).reshape(d, heads * dkp)


def _split_kv(w, heads, dk, dv):
    d = w.shape[0]
    w3 = w.reshape(d, heads, dk + dv)
    return w3[:, :, :dk].reshape(d, heads * dk), w3[:, :, dk:].reshape(d, heads * dv)


def _mixer_weights(w_in, d_model):
    sizes = (
        NSA_HEADS * NSA_DK, NSA_GROUPS * (NSA_DK + NSA_DV), NSA_GROUPS * (NSA_DK + NSA_DV),
        NSA_GROUPS * (NSA_DK + NSA_DV), 3 * NSA_HEADS, DSA_HEADS * DSA_DK, DSA_KV_HEADS * (DSA_DK + DSA_DV),
        IDX_HEADS * IDX_DIM, IDX_DIM, IDX_HEADS, 2 * d_model,
    )
    offs = np.cumsum((0,) + sizes)
    assert offs[-1] == w_in.shape[1]
    w_bf = w_in.astype(BF16)
    q_a, kv_c, kv_s, kv_w, gate_a, q_b, kv_b, q_i, k_i, w_i, merge = (
        w_bf[:, offs[n]:offs[n + 1]] for n in range(len(sizes)))
    k_c, v_c = _split_kv(kv_c, NSA_GROUPS, NSA_DK, NSA_DV)
    k_s, v_s = _split_kv(kv_s, NSA_GROUPS, NSA_DK, NSA_DV)
    k_w, v_w = _split_kv(kv_w, NSA_GROUPS, NSA_DK, NSA_DV)
    k_b, v_b = _split_kv(kv_b, DSA_KV_HEADS, DSA_DK, DSA_DV)
    w192 = jnp.concatenate([
        _pad_heads(q_a, NSA_HEADS, NSA_DK, NSA_DKP), _pad_heads(k_s, NSA_GROUPS, NSA_DK, NSA_DKP),
        _pad_heads(k_w, NSA_GROUPS, NSA_DK, NSA_DKP)], axis=1)
    scale192 = jnp.concatenate([
        jnp.full((NSA_HEADS * NSA_DKP,), NSA_DK ** -0.5 * LOG2E, F32), jnp.ones((2 * NSA_GROUPS * NSA_DKP,), F32)])
    n128 = IDX_HEADS * IDX_DIM + DSA_HEADS * DSA_DK + DSA_KV_HEADS * DSA_DK + IDX_DIM
    n128_pad = -(-n128 // 1024) * 1024
    w128 = jnp.pad(jnp.concatenate([q_i, q_b, k_b, k_i], axis=1), ((0, 0), (0, n128_pad - n128)))
    scale128 = jnp.concatenate([
        jnp.full((IDX_HEADS * IDX_DIM,), IDX_DIM ** -0.5, F32),
        jnp.full((DSA_HEADS * DSA_DK,), DSA_DK ** -0.5 * LOG2E, F32),
        jnp.ones((n128_pad - IDX_HEADS * IDX_DIM - DSA_HEADS * DSA_DK,), F32)])
    wv = jnp.concatenate([v_s, v_w, v_b], axis=1)
    wc = jnp.concatenate([k_c, v_c], axis=1)
    small_n = GATE_COLS + IDX_HEADS
    wsmall = jnp.pad(jnp.concatenate([gate_a, w_i], axis=1), ((0, 0), (0, V7X_LANES - small_n)))
    return w192, scale192, w128, scale128, wv, wc, wsmall, merge


def _ffn_up_kernel(h_ref, w1_ref, w3_ref, o_ref, w1b_ref, w3b_ref):
    @pl.when(pl.program_id(1) == 0)
    def _():
        w1b_ref[...] = w1_ref[...].astype(BF16)
        w3b_ref[...] = w3_ref[...].astype(BF16)

    h = h_ref[...]
    a = jnp.dot(h, w1b_ref[...], preferred_element_type=F32)
    b = jnp.dot(h, w3b_ref[...], preferred_element_type=F32)
    o_ref[...] = (jax.nn.silu(a) * b).astype(o_ref.dtype)


def _ffn_up(h, w1, w3, cfg):
    t, d = h.shape
    ff = w1.shape[1]
    tm = _tile(cfg.seq, 1024)
    tn = _tile(ff, 256)
    est = (2 * _nbytes((tm, d), BF16) + 4 * _nbytes((d, tn), F32) + 3 * _nbytes((d, tn), BF16)
           + 2 * _nbytes((tm, tn), BF16) + 4 * _nbytes((tm, tn), F32) + (2 << 20))
    return pl.pallas_call(
        _ffn_up_kernel,
        grid=(ff // tn, t // tm),
        in_specs=[
            pl.BlockSpec((tm, d), lambda j, i: (i, 0)),
            pl.BlockSpec((d, tn), lambda j, i: (0, j)),
            pl.BlockSpec((d, tn), lambda j, i: (0, j)),
        ],
        out_specs=pl.BlockSpec((tm, tn), lambda j, i: (i, j)),
        out_shape=jax.ShapeDtypeStruct((t, ff), BF16),
        scratch_shapes=[pltpu.VMEM((d, tn), BF16), pltpu.VMEM((d, tn), BF16)],
        compiler_params=_params(("parallel", "arbitrary"), est),
        name="ffn_up",
    )(h, w1, w3)


def _ffn(x2, gain, mod3, idx, w1, w3, w2, cfg):
    t, d = x2.shape
    ff = w1.shape[1]
    h = _norm_mod(x2, gain, mod3, idx, idx + 1, cfg)
    u = _ffn_up(h, w1, w3, cfg)
    tm = _tile(cfg.seq, 512)
    nsb = cfg.seq // tm
    tn = _tile(d, 512)
    epi = [(x2, (tm, tn), lambda i, j: (i, j)),
           (mod3, (1, 1, tn), lambda i, j: ((i // nsb) * N_MOD + idx + 2, 0, j))]
    return _matmul([u], [w2.astype(BF16)], [(0, 0)], functools.partial(_epi_residual, coef=0.5), epi, d, F32,
                   tm=tm, tn=tn, name="ffn_down")


def _mixer(x2, gain, mod3, w_in, cmp_pe_k, cmp_w1_k, cmp_w2_k, cmp_pe_v, cmp_w1_v, cmp_w2_v,
           proj_a, proj_b, w_out, cfg):
    b, s, d = cfg.batch, cfg.seq, cfg.d_model
    w192, scale192, w128, scale128, wv, wc, wsmall, wmerge = _mixer_weights(w_in, d)
    h = _norm_mod(x2, gain, mod3, 3, 4, cfg)
    p192 = _project_rope(h, w192, scale192, NSA_DK, NSA_DKP, cfg, "proj_rope192")
    p128 = _project_rope(h, w128, scale128, DSA_DK, DSA_DK, cfg, "proj_rope128")
    pv = _project(h, wv, _epi_plain, BF16, cfg, "proj_values")
    pc = _project(h, wc, _epi_plain, F32, cfg, "proj_cmp")
    small = _project(h, wsmall, _epi_plain, F32, cfg, "proj_small")
    gates_m = _project(h, wmerge, _epi_sigmoid, BF16, cfg, "proj_merge")

    n_rows = s // CMP_STRIDE
    kdim = NSA_GROUPS * NSA_DK

    def chunks(cols, dd):
        xg = cols.reshape(b, n_rows, CMP_STRIDE, NSA_GROUPS, dd)
        return jnp.transpose(xg, (0, 3, 1, 2, 4)).reshape(b * NSA_GROUPS, n_rows, CMP_STRIDE * dd)

    cmp_pos = jnp.arange(n_rows) * CMP_STRIDE + CMP_BLOCK - 1
    k_cmp = _compress(chunks(pc[:, :kdim], NSA_DK), cmp_pe_k, cmp_w1_k, cmp_w2_k, NSA_DK, NSA_DKP, cmp_pos,
                      "compress_k").reshape(b, NSA_GROUPS, n_rows, NSA_DKP)
    v_cmp = _compress(chunks(pc[:, kdim:], NSA_DV), cmp_pe_v, cmp_w1_v, cmp_w2_v, NSA_DV, NSA_DV, None,
                      "compress_v").reshape(b, NSA_GROUPS, n_rows, NSA_DV)

    o_a = _nsa_attention(p192, pv, small, k_cmp, v_cmp, cfg)
    o_b = _dsa_attention(p128, pv, small, cfg)

    tm = _tile(s, 1024)
    tn = _tile(d, 1024)
    nsb = s // tm
    epi = [(gates_m, (tm, tn), lambda i, j: (i, j)), (gates_m, (tm, tn), lambda i, j: (i, j + d // tn))]
    y = _matmul([o_a, o_b], [proj_a.astype(BF16), proj_b.astype(BF16)], [(0, 0), (1, 1)], _epi_merge, epi,
                d, BF16, tm=tm, tn=tn, name="merge_proj")
    epi = [(x2, (tm, tn), lambda i, j: (i, j)),
           (mod3, (1, 1, tn), lambda i, j: ((i // nsb) * N_MOD + 5, 0, j))]
    return _matmul([y], [w_out.astype(BF16)], [(0, 0)], functools.partial(_epi_residual, coef=1.0), epi,
                   d, F32, tm=tm, tn=tn, name="out_proj")


def _layer(x, c, ada_w, ada_b, norm_ffn1, ffn1_w1, ffn1_w3, ffn1_w2, norm_mix, w_in,
           cmp_pe_k, cmp_w1_k, cmp_w2_k, cmp_pe_v, cmp_w1_v, cmp_w2_v, proj_a, proj_b, w_out,
           norm_ffn2, ffn2_w1, ffn2_w3, ffn2_w2, norm_final):
    b, s, d = x.shape
    cfg = Cfg(b, s, d, ffn1_w1.shape[-1])
    assert s % Q_BLOCK == 0 and s >= WINDOW + Q_BLOCK and d % V7X_LANES == 0
    x2 = x.reshape(b * s, d)
    c_pad = jnp.pad(c, ((0, 8 - b), (0, 0)))
    for l in range(ada_w.shape[0]):
        mod = _adaln(c_pad, ada_w[l], ada_b[l].reshape(1, -1))
        mod3 = mod[:b].reshape(b * N_MOD, 1, d)
        x2 = _ffn(x2, norm_ffn1[l], mod3, 0, ffn1_w1[l], ffn1_w3[l], ffn1_w2[l], cfg)
        x2 = _mixer(x2, norm_mix[l], mod3, w_in[l], cmp_pe_k[l], cmp_w1_k[l], cmp_w2_k[l],
                    cmp_pe_v[l], cmp_w1_v[l], cmp_w2_v[l], proj_a[l], proj_b[l], w_out[l], cfg)
        x2 = _ffn(x2, norm_ffn2[l], mod3, 6, ffn2_w1[l], ffn2_w3[l], ffn2_w2[l], cfg)
    return _final_norm(x2, norm_final).reshape(b, s, d)


def kernel(x, c, ada_w, ada_b, norm_ffn1, ffn1_w1, ffn1_w3, ffn1_w2, norm_mix, w_in, cmp_pe_k, cmp_w1_k, cmp_w2_k, cmp_pe_v, cmp_w1_v, cmp_w2_v, proj_a, proj_b, w_out, norm_ffn2, ffn2_w1, ffn2_w3, ffn2_w2, norm_final):
    return _layer(x, c, ada_w, ada_b, norm_ffn1, ffn1_w1, ffn1_w3, ffn1_w2, norm_mix, w_in,
                  cmp_pe_k, cmp_w1_k, cmp_w2_k, cmp_pe_v, cmp_w1_v, cmp_w2_v, proj_a, proj_b, w_out,
                  norm_ffn2, ffn2_w1, ffn2_w3, ffn2_w2, norm_final)
```

```python
import functools
import math
from typing import NamedTuple

import numpy as np
import jax
import jax.numpy as jnp
from jax import lax
from jax.experimental import pallas as pl
from jax.experimental.pallas import tpu as pltpu

F32 = jnp.float32
BF16 = jnp.bfloat16

V7X_LANES = 128
V7X_VMEM_BYTES = 64 * 1024 * 1024
V7X_VMEM_BUDGET = 56 * 1024 * 1024

EPS = 1e-6
ROPE_THETA = 500000.0
ROPE_FRACTION = 4
N_MOD = 9
Q_BLOCK = 128

NSA_HEADS = 16
NSA_GROUPS = 2
NSA_HPG = NSA_HEADS // NSA_GROUPS
NSA_DK = 192
NSA_DKP = 256
NSA_DV = 128
CMP_BLOCK = 32
CMP_STRIDE = 16
SEL_BLOCK = 64
SEL_TOP = 16
WINDOW = 512

DSA_HEADS = 16
DSA_KV_HEADS = 4
DSA_HPG = DSA_HEADS // DSA_KV_HEADS
DSA_DK = 128
DSA_DV = 128
IDX_HEADS = 32
IDX_DIM = 128
DSA_TOPK = 256

GATE_COLS = 3 * NSA_HEADS
INT_MIN = np.int32(-(2**31))
NEG_INF = float("-inf")
LOG2E = math.log2(math.e)
TINY = float(np.finfo(np.float32).tiny)
DN_T = (((1,), (1,)), ((), ()))


class Cfg(NamedTuple):
    batch: int
    seq: int
    d_model: int
    d_ff: int


def _tile(n, pref):
    if n <= pref:
        return n
    t = (pref // V7X_LANES) * V7X_LANES
    while t > V7X_LANES and n % t:
        t -= V7X_LANES
    assert n % t == 0, (n, pref)
    return t


def _nbytes(shape, dtype):
    return int(np.prod(shape)) * jnp.dtype(dtype).itemsize


def _params(dims, est_bytes):
    limit = int(min(V7X_VMEM_BUDGET, max(est_bytes, 16 * 1024 * 1024)))
    return pltpu.CompilerParams(dimension_semantics=dims, vmem_limit_bytes=limit)


def _adaln_kernel(c_ref, w_ref, b_ref, o_ref, acc_ref, *, nk):
    k = pl.program_id(1)

    @pl.when(k == 0)
    def _():
        acc_ref[...] = jnp.zeros_like(acc_ref)

    c = c_ref[...]
    c_act = (c * jax.nn.sigmoid(c)).astype(BF16)
    acc_ref[...] += jnp.dot(c_act, w_ref[...].astype(BF16), preferred_element_type=F32)

    @pl.when(k == nk - 1)
    def _():
        o_ref[...] = acc_ref[...] + b_ref[...]


def _adaln(c_pad, w, b):
    m, kdim = c_pad.shape
    n = w.shape[1]
    tn, tk = _tile(n, 2048), _tile(kdim, 1024)
    nk = kdim // tk
    est = 2 * _nbytes((tk, tn), F32) + _nbytes((tk, tn), BF16) + 4 * _nbytes((m, tn), F32) + (2 << 20)
    return pl.pallas_call(
        functools.partial(_adaln_kernel, nk=nk),
        grid=(n // tn, nk),
        in_specs=[
            pl.BlockSpec((m, tk), lambda j, k: (0, k)),
            pl.BlockSpec((tk, tn), lambda j, k: (k, j)),
            pl.BlockSpec((1, tn), lambda j, k: (0, j)),
        ],
        out_specs=pl.BlockSpec((m, tn), lambda j, k: (0, j)),
        out_shape=jax.ShapeDtypeStruct((m, n), F32),
        scratch_shapes=[pltpu.VMEM((m, tn), F32)],
        compiler_params=_params(("parallel", "arbitrary"), est),
        name="adaln",
    )(c_pad, w, b)


def _norm_mod_kernel(x_ref, g_ref, sh_ref, sc_ref, o_ref):
    x = x_ref[...]
    y = x * lax.rsqrt(jnp.mean(x * x, axis=-1, keepdims=True) + EPS)
    y = y * g_ref[...]
    o_ref[...] = (y * (1.0 + sc_ref[0]) + sh_ref[0]).astype(o_ref.dtype)


def _norm_mod(x2, gain, mod3, shift_idx, scale_idx, cfg):
    t, d = x2.shape
    tm = _tile(cfg.seq, 512)
    nsb = cfg.seq // tm
    est = 4 * _nbytes((tm, d), F32) + 2 * _nbytes((tm, d), BF16) + (2 << 20)
    return pl.pallas_call(
        _norm_mod_kernel,
        grid=(t // tm,),
        in_specs=[
            pl.BlockSpec((tm, d), lambda i: (i, 0)),
            pl.BlockSpec((1, d), lambda i: (0, 0)),
            pl.BlockSpec((1, 1, d), lambda i: ((i // nsb) * N_MOD + shift_idx, 0, 0)),
            pl.BlockSpec((1, 1, d), lambda i: ((i // nsb) * N_MOD + scale_idx, 0, 0)),
        ],
        out_specs=pl.BlockSpec((tm, d), lambda i: (i, 0)),
        out_shape=jax.ShapeDtypeStruct((t, d), BF16),
        compiler_params=_params(("parallel",), est),
        name="norm_mod",
    )(x2, gain.reshape(1, d), mod3, mod3)


def _final_norm_kernel(x_ref, g_ref, o_ref):
    x = x_ref[...]
    y = x * lax.rsqrt(jnp.mean(x * x, axis=-1, keepdims=True) + EPS)
    o_ref[...] = y * g_ref[...]


def _final_norm(x2, gain):
    t, d = x2.shape
    tm = _tile(t, 512)
    est = 6 * _nbytes((tm, d), F32) + (2 << 20)
    return pl.pallas_call(
        _final_norm_kernel,
        grid=(t // tm,),
        in_specs=[pl.BlockSpec((tm, d), lambda i: (i, 0)), pl.BlockSpec((1, d), lambda i: (0, 0))],
        out_specs=pl.BlockSpec((tm, d), lambda i: (i, 0)),
        out_shape=jax.ShapeDtypeStruct((t, d), F32),
        compiler_params=_params(("parallel",), est),
        name="final_norm",
    )(x2, gain.reshape(1, d))


MM_SUB = 256


def _mm_kernel(*refs, n_a, n_w, pairs, n_epi, epilogue, sub):
    a_refs = refs[:n_a]
    w_refs = refs[n_a:n_a + n_w]
    e_refs = refs[n_a + n_w:n_a + n_w + n_epi]
    o_ref = refs[n_a + n_w + n_epi]
    for c in range(o_ref.shape[1] // sub):
        cols = slice(c * sub, (c + 1) * sub)
        accs = [jnp.dot(a_refs[ai][...], w_refs[wi][:, cols], preferred_element_type=F32) for ai, wi in pairs]
        o_ref[:, cols] = epilogue(accs, e_refs, cols).astype(o_ref.dtype)


def _matmul(a_list, w_list, pairs, epilogue, epi_inputs, n_out, out_dtype, *, tm, tn, name):
    m = a_list[0].shape[0]
    assert m % tm == 0 and n_out % tn == 0
    sub = MM_SUB if tn % MM_SUB == 0 else tn
    in_specs, est = [], 0
    for a in a_list:
        in_specs.append(pl.BlockSpec((tm, a.shape[1]), lambda i, j: (i, 0)))
        est += 2 * _nbytes((tm, a.shape[1]), a.dtype)
    for w in w_list:
        in_specs.append(pl.BlockSpec((w.shape[0], tn), lambda i, j: (0, j)))
        est += 2 * _nbytes((w.shape[0], tn), w.dtype)
    for arr, bshape, imap in epi_inputs:
        in_specs.append(pl.BlockSpec(bshape, imap))
        est += 2 * _nbytes(bshape, arr.dtype)
    est += 2 * _nbytes((tm, tn), out_dtype) + 4 * (len(pairs) + 3) * _nbytes((tm, sub), F32) + (2 << 20)
    kern = functools.partial(_mm_kernel, n_a=len(a_list), n_w=len(w_list), pairs=tuple(pairs),
                             n_epi=len(epi_inputs), epilogue=epilogue, sub=sub)
    return pl.pallas_call(
        kern,
        grid=(m // tm, n_out // tn),
        in_specs=in_specs,
        out_specs=pl.BlockSpec((tm, tn), lambda i, j: (i, j)),
        out_shape=jax.ShapeDtypeStruct((m, n_out), out_dtype),
        compiler_params=_params(("parallel", "parallel"), est),
        name=name,
    )(*a_list, *w_list, *[e[0] for e in epi_inputs])


def _epi_plain(accs, e_refs, cols):
    return accs[0]


def _epi_sigmoid(accs, e_refs, cols):
    return jax.nn.sigmoid(accs[0])


def _epi_merge(accs, e_refs, cols):
    return e_refs[0][:, cols].astype(F32) * accs[0] + e_refs[1][:, cols].astype(F32) * accs[1]


def _epi_residual(accs, e_refs, cols, *, coef):
    return e_refs[0][:, cols] + (coef * e_refs[1][0][:, cols]) * accs[0]


def _rope_apply(a, cos, sin_lo, sin_hi, half):
    width = a.shape[-1]
    reps = (1, width // cos.shape[-1])
    return (a * jnp.tile(cos, reps) + pltpu.roll(a, width - half, 1) * jnp.tile(sin_lo, reps)
            + pltpu.roll(a, half, 1) * jnp.tile(sin_hi, reps))


def _epi_rope(accs, e_refs, cols, *, half):
    out = _rope_apply(accs[0], e_refs[0][...], e_refs[1][...], e_refs[2][...], half)
    return out * e_refs[3][:, cols]


def _rope_tables(pos, dk, head_w):
    r = dk // ROPE_FRACTION
    half = r // 2
    n = pos.shape[0]
    inv = ROPE_THETA ** (-jnp.arange(0, r, 2, dtype=F32) / r)
    ang = pos.astype(F32)[:, None] * inv[None, :]
    cos, sin = jnp.cos(ang), jnp.sin(ang)
    cos_t = jnp.concatenate([cos, cos, jnp.ones((n, head_w - r), F32)], axis=1)
    sin_lo = jnp.concatenate([-sin, jnp.zeros((n, head_w - half), F32)], axis=1)
    sin_hi = jnp.concatenate([jnp.zeros((n, half), F32), sin, jnp.zeros((n, head_w - r), F32)], axis=1)
    return cos_t, sin_lo, sin_hi, half


def _project_rope(h, w, colscale, dk, head_w, cfg, name):
    n = w.shape[1]
    tm = _tile(cfg.seq, 1024)
    tn = _tile(n, 1024)
    nsb = cfg.seq // tm
    cos_t, sin_lo, sin_hi, half = _rope_tables(jnp.arange(cfg.seq), dk, head_w)
    epi = [(tb, (tm, head_w), lambda i, j: (i % nsb, 0)) for tb in (cos_t, sin_lo, sin_hi)]
    epi.append((colscale.reshape(1, n), (1, tn), lambda i, j: (0, j)))
    return _matmul([h], [w], [(0, 0)], functools.partial(_epi_rope, half=half), epi, n, BF16,
                   tm=tm, tn=tn, name=name)


def _project(h, w, epilogue, out_dtype, cfg, name):
    n = w.shape[1]
    return _matmul([h], [w], [(0, 0)], epilogue, [], n, out_dtype,
                   tm=_tile(cfg.seq, 1024), tn=_tile(n, 1024), name=name)


def _gelu_tanh(x):
    return 0.5 * x * (1.0 + jnp.tanh(math.sqrt(2.0 / math.pi) * (x + 0.044715 * (x * x * x))))


def _compress_kernel(x_ref, pe_ref, w1_ref, w2_ref, *rest, half):
    o_ref = rest[-1]
    x = x_ref[0]
    n_rows, width = x.shape
    a_lo = (x + pe_ref[0:1, :]).astype(BF16)
    a_hi = (x + pe_ref[1:2, :]).astype(BF16)
    p = jnp.dot(a_lo, w1_ref[0:width, :], preferred_element_type=F32)
    q = jnp.dot(a_hi, w1_ref[width:2 * width, :], preferred_element_type=F32)
    hid = _gelu_tanh(p + pltpu.roll(q, n_rows - 1, 0))
    y = jnp.dot(hid.astype(BF16), w2_ref[...], preferred_element_type=F32)
    if half:
        y = _rope_apply(y, rest[0][...], rest[1][...], rest[2][...], half)
    o_ref[0] = y.astype(o_ref.dtype)


def _compress(x_chunks, pe, w1, w2, d, d_out, rope_pos, name):
    bg, n_rows, width = x_chunks.shape
    hid_w = ((d + V7X_LANES - 1) // V7X_LANES) * V7X_LANES
    pe2 = pe.reshape(2, width)
    w1p = jnp.pad(w1.astype(BF16), ((0, 0), (0, hid_w - d)))
    w2p = jnp.pad(w2.astype(BF16), ((0, hid_w - d), (0, d_out - d)))
    ins = [x_chunks, pe2, w1p, w2p]
    in_specs = [
        pl.BlockSpec((1, n_rows, width), lambda i: (i, 0, 0)),
        pl.BlockSpec((2, width), lambda i: (0, 0)),
        pl.BlockSpec((2 * width, hid_w), lambda i: (0, 0)),
        pl.BlockSpec((hid_w, d_out), lambda i: (0, 0)),
    ]
    half = 0
    if rope_pos is not None:
        cos_t, sin_lo, sin_hi, half = _rope_tables(rope_pos, d, d_out)
        ins += [cos_t, sin_lo, sin_hi]
        in_specs += [pl.BlockSpec((n_rows, d_out), lambda i: (0, 0))] * 3
    est = (2 * _nbytes((n_rows, width), F32) + 2 * _nbytes((n_rows, width), BF16)
           + 2 * _nbytes((2 * width, hid_w), BF16) + (8 << 20))
    return pl.pallas_call(
        functools.partial(_compress_kernel, half=half),
        grid=(bg,),
        in_specs=in_specs,
        out_specs=pl.BlockSpec((1, n_rows, d_out), lambda i: (i, 0, 0)),
        out_shape=jax.ShapeDtypeStruct((bg, n_rows, d_out), BF16),
        compiler_params=_params(("parallel",), est),
        name=name,
    )(*ins)


def _stack_heads(ref, first, count, width):
    return jnp.concatenate([ref[:, (first + h) * width:(first + h + 1) * width] for h in range(count)], axis=0)


def _mask_bias(mask):
    return jnp.where(mask, 0.0, NEG_INF)


def _with_ones(v):
    return jnp.concatenate([v, jnp.ones((v.shape[0], V7X_LANES), v.dtype)], axis=1)


def _exp2_masked(s3, bias2):
    z = s3 + bias2[None]
    m = jnp.max(z, axis=-1, keepdims=True)
    m = jnp.where(m > NEG_INF, m, 0.0)
    return jnp.exp2(z - m)


def _normalize(pv, dv):
    return pv[:, :dv] / jnp.maximum(pv[:, dv:dv + V7X_LANES], TINY)


def _flash_init(m_ref, acc_ref):
    m_ref[...] = jnp.full(m_ref.shape, NEG_INF, F32)
    acc_ref[...] = jnp.zeros(acc_ref.shape, F32)


def _flash_step(s, bias2, v_ones, heads, m_ref, acc_ref):
    rows, kc = s.shape
    z = (s.reshape(heads, rows // heads, kc) + bias2[None]).reshape(rows, kc)
    m_old = m_ref[...]
    m_new = jnp.maximum(m_old, jnp.max(z, axis=-1, keepdims=True).astype(F32))
    m_safe = jnp.where(m_new > NEG_INF, m_new, 0.0)
    alpha = jnp.exp2(m_old - m_safe)
    p = jnp.exp2(z - jnp.tile(m_safe.astype(z.dtype), (1, kc // V7X_LANES)))
    pv = jnp.dot(p.astype(BF16), v_ones, preferred_element_type=F32)
    acc_ref[...] = jnp.tile(alpha, (1, acc_ref.shape[-1] // V7X_LANES)) * acc_ref[...] + pv
    m_ref[...] = m_new


def _chunk_loop(n_chunks, process):
    def pair_body(j, carry):
        process([2 * j, 2 * j + 1])
        return carry

    lax.fori_loop(0, n_chunks // 2, pair_body, 0)

    @pl.when(n_chunks % 2 == 1)
    def _():
        process([n_chunks - 1])


def _nsa_kernel(q_ref, gate_ref, kc_ref, vc_ref, ks_ref, vs_ref, kw_ref, vw_ref, aggt_ref, o_ref,
                m_ref, acc_ref, *, seq, kc, blk_pad):
    n_cmp = (seq - CMP_BLOCK) // CMP_STRIDE + 1
    n_blk = seq // SEL_BLOCK
    n_sel = min(SEL_TOP, n_blk)
    cmp_pad = aggt_ref.shape[1]
    q0 = pl.program_id(1) * Q_BLOCK
    t_col = q0 + lax.broadcasted_iota(jnp.int32, (Q_BLOCK, 1), 0)
    t_row = q0 + lax.broadcasted_iota(jnp.int32, (1, Q_BLOCK), 1)
    gates = jax.nn.sigmoid(gate_ref[...])

    cidx = lax.broadcasted_iota(jnp.int32, (Q_BLOCK, cmp_pad), 1)
    bias_c = _mask_bias((cidx * CMP_STRIDE + (CMP_BLOCK - 1) <= t_col) & (cidx < n_cmp))

    blk = lax.broadcasted_iota(jnp.int32, (n_blk, Q_BLOCK), 0)
    cur = lax.shift_right_logical(t_row, int(math.log2(SEL_BLOCK)))
    forced = (blk == 0) | (blk == cur) | (blk == cur - 1)
    valid = blk * SEL_BLOCK <= t_row

    win_k = WINDOW + Q_BLOCK
    ws = pl.multiple_of(jnp.maximum(q0 - WINDOW, 0), Q_BLOCK)
    diff = t_col - (ws + lax.broadcasted_iota(jnp.int32, (1, win_k), 1))
    bias_w = _mask_bias((diff >= 0) & (diff < WINDOW)).astype(BF16)
    n_chunks = (q0 + Q_BLOCK + kc - 1) // kc
    rows = NSA_HPG * Q_BLOCK

    q_groups = [_stack_heads(q_ref, g * NSA_HPG, NSA_HPG, NSA_DKP) for g in range(NSA_GROUPS)]
    o_cmp, sels = [], []
    for g in range(NSA_GROUPS):
        qg = q_groups[g]

        s = lax.dot_general(qg, kc_ref[0, g], DN_T, preferred_element_type=F32)
        e_c = _exp2_masked(s.reshape(NSA_HPG, Q_BLOCK, cmp_pad), bias_c)
        p_c = e_c / jnp.maximum(jnp.sum(e_c, axis=-1, keepdims=True), TINY)
        o_c = jnp.dot(p_c.reshape(rows, cmp_pad).astype(BF16), vc_ref[0, g], preferred_element_type=F32)

        p_sum = jnp.sum(p_c, axis=0)
        hi = p_sum.astype(BF16)
        r1 = p_sum - hi.astype(F32)
        mid = r1.astype(BF16)
        lo = (r1 - mid.astype(F32)).astype(BF16)
        aggt = aggt_ref[...]
        imp = (lax.dot_general(aggt, hi, DN_T, preferred_element_type=F32)
               + lax.dot_general(aggt, mid, DN_T, preferred_element_type=F32)
               + lax.dot_general(aggt, lo, DN_T, preferred_element_type=F32))
        val = jnp.where(forced, jnp.inf, jnp.where(valid, imp, NEG_INF))
        rank = jnp.zeros((n_blk, Q_BLOCK), F32)
        for ii in range(n_blk):
            row = val[ii:ii + 1, :]
            beats = (row > val) | ((row == val) & (blk > ii))
            rank = rank + jnp.where(beats, 1.0, 0.0)
        sel_t = jnp.where((rank < n_sel) & (val > NEG_INF), 1.0, 0.0)
        if blk_pad > n_blk:
            sel_t = jnp.concatenate([sel_t, jnp.zeros((blk_pad - n_blk, Q_BLOCK), F32)], axis=0)
        sels.append(jnp.transpose(sel_t).astype(BF16))
        o_cmp.append(o_c)

    _flash_init(m_ref, acc_ref)

    def sel_chunks(chunks):
        scores = []
        for c in chunks:
            k0 = pl.multiple_of(c * kc, kc)
            scores.append([
                lax.dot_general(q_groups[g], ks_ref[pl.ds(k0, kc), g * NSA_DKP:(g + 1) * NSA_DKP], DN_T,
                                preferred_element_type=F32).astype(BF16) for g in range(NSA_GROUPS)])
        for c, s_c in zip(chunks, scores):
            k0 = pl.multiple_of(c * kc, kc)
            kpos_b = k0 + lax.broadcasted_iota(jnp.int32, (blk_pad, kc), 1)
            expand = jnp.where(
                lax.shift_right_logical(kpos_b, int(math.log2(SEL_BLOCK)))
                == lax.broadcasted_iota(jnp.int32, (blk_pad, kc), 0), 1.0, 0.0).astype(BF16)
            causal = k0 + lax.broadcasted_iota(jnp.int32, (1, kc), 1) <= t_col
            for g in range(NSA_GROUPS):
                v_blk = vs_ref[pl.ds(k0, kc), g * NSA_DV:(g + 1) * NSA_DV]
                sel_keys = jnp.dot(sels[g], expand, preferred_element_type=F32)
                bias = _mask_bias((sel_keys > 0.5) & causal).astype(BF16)
                _flash_step(s_c[g], bias, _with_ones(v_blk), NSA_HPG, m_ref.at[g], acc_ref.at[g])

    _chunk_loop(n_chunks, sel_chunks)

    for g in range(NSA_GROUPS):
        qg, o_c = q_groups[g], o_cmp[g]
        o_s = _normalize(acc_ref[g], NSA_DV)

        k_win = kw_ref[pl.ds(ws, win_k), g * NSA_DKP:(g + 1) * NSA_DKP]
        v_win = vw_ref[pl.ds(ws, win_k), g * NSA_DV:(g + 1) * NSA_DV]
        s_w = lax.dot_general(qg, k_win, DN_T, preferred_element_type=F32).astype(BF16)
        e_w = _exp2_masked(s_w.reshape(NSA_HPG, Q_BLOCK, win_k), bias_w)
        o_w = _normalize(jnp.dot(e_w.reshape(rows, win_k).astype(BF16), _with_ones(v_win),
                                 preferred_element_type=F32), NSA_DV)

        for h in range(NSA_HPG):
            c0 = (g * NSA_HPG + h) * 3
            hr = slice(h * Q_BLOCK, (h + 1) * Q_BLOCK)
            o = (gates[:, c0:c0 + 1] * o_c[hr] + gates[:, c0 + 1:c0 + 2] * o_s[hr]
                 + gates[:, c0 + 2:c0 + 3] * o_w[hr])
            head = g * NSA_HPG + h
            o_ref[:, head * NSA_DV:(head + 1) * NSA_DV] = o.astype(o_ref.dtype)


def _agg_t(seq, cmp_pad):
    n_cmp = (seq - CMP_BLOCK) // CMP_STRIDE + 1
    n_blk = seq // SEL_BLOCK
    r = SEL_BLOCK // CMP_STRIDE
    c = CMP_BLOCK // CMP_STRIDE
    j = np.arange(n_blk)[:, None, None, None]
    i = np.arange(cmp_pad)[None, :, None, None]
    m = np.arange(r)[None, None, :, None]
    n = np.arange(c)[None, None, None, :]
    a = np.sum(i == r * j + m - n, axis=(2, 3)).astype(np.float32)
    a = a * (np.arange(cmp_pad)[None, :] < n_cmp)
    return jnp.asarray(a, dtype=BF16)


def _nsa_attention(p192, pv, small, k_cmp, v_cmp, cfg):
    b, s = cfg.batch, cfg.seq
    n_qb = s // Q_BLOCK
    kc = _tile(s, 512)
    cmp_pad = k_cmp.shape[2]
    blk_pad = max(V7X_LANES, s // SEL_BLOCK)
    qw = NSA_HEADS * NSA_DKP
    kw = NSA_GROUPS * NSA_DKP
    vw = NSA_GROUPS * NSA_DV
    rows = NSA_HPG * Q_BLOCK
    est = (2 * (_nbytes((Q_BLOCK, qw), BF16) + 2 * _nbytes((s, kw), BF16) + 2 * _nbytes((s, vw), BF16))
           + 3 * _nbytes((rows, V7X_LANES), F32) + 10 * _nbytes((rows, WINDOW + Q_BLOCK), F32) + (4 << 20))
    return pl.pallas_call(
        functools.partial(_nsa_kernel, seq=s, kc=kc, blk_pad=blk_pad),
        grid=(b, n_qb),
        in_specs=[
            pl.BlockSpec((Q_BLOCK, qw), lambda bi, i: (bi * n_qb + i, 0)),
            pl.BlockSpec((Q_BLOCK, V7X_LANES), lambda bi, i: (bi * n_qb + i, 0)),
            pl.BlockSpec((1, NSA_GROUPS, cmp_pad, NSA_DKP), lambda bi, i: (bi, 0, 0, 0)),
            pl.BlockSpec((1, NSA_GROUPS, cmp_pad, NSA_DV), lambda bi, i: (bi, 0, 0, 0)),
            pl.BlockSpec((s, kw), lambda bi, i: (bi, qw // kw)),
            pl.BlockSpec((s, vw), lambda bi, i: (bi, 0)),
            pl.BlockSpec((s, kw), lambda bi, i: (bi, qw // kw + 1)),
            pl.BlockSpec((s, vw), lambda bi, i: (bi, 1)),
            pl.BlockSpec((s // SEL_BLOCK, cmp_pad), lambda bi, i: (0, 0)),
        ],
        out_specs=pl.BlockSpec((Q_BLOCK, NSA_HEADS * NSA_DV), lambda bi, i: (bi * n_qb + i, 0)),
        out_shape=jax.ShapeDtypeStruct((b * s, NSA_HEADS * NSA_DV), BF16),
        scratch_shapes=[pltpu.VMEM((NSA_GROUPS, rows, V7X_LANES), F32),
                        pltpu.VMEM((NSA_GROUPS, rows, NSA_DV + V7X_LANES), F32)],
        compiler_params=_params(("parallel", "arbitrary"), est),
        name="nsa_attention",
    )(p192, small, k_cmp, v_cmp, p192, pv, p192, pv, _agg_t(s, cmp_pad))


def _dsa_kernel(qi_ref, w_ref, ki_ref, qb_ref, kb_ref, vb_ref, o_ref, key_ref, m_ref, acc_ref, *, kc, k_top):
    q0 = pl.program_id(1) * Q_BLOCK
    t_row = q0 + lax.broadcasted_iota(jnp.int32, (1, Q_BLOCK), 1)
    n_chunks = (q0 + Q_BLOCK + kc - 1) // kc
    w_t = jnp.transpose(w_ref[...] * (IDX_HEADS ** -0.5))
    hg = 8
    cnt_rows = 64

    def causal_t(c):
        return c * kc + lax.broadcasted_iota(jnp.int32, (kc, 1), 0) <= t_row

    def score_body(c, carry):
        k0 = pl.multiple_of(c * kc, kc)
        k_i = ki_ref[pl.ds(k0, kc), :]
        acc = jnp.zeros((kc, Q_BLOCK), F32)
        for g in range(IDX_HEADS // hg):
            q_h = _stack_heads(qi_ref, g * hg, hg, IDX_DIM)
            rel = jnp.maximum(lax.dot_general(k_i, q_h, DN_T, preferred_element_type=F32), 0.0)
            for h in range(hg):
                col = GATE_COLS + g * hg + h
                acc = acc + rel[:, h * Q_BLOCK:(h + 1) * Q_BLOCK] * w_t[col:col + 1, :]
        bits = lax.bitcast_convert_type(acc, jnp.int32)
        skey = bits ^ (lax.shift_right_arithmetic(bits, 31) & np.int32(0x7FFFFFFF))
        key_ref[c] = jnp.where(causal_t(c), skey, INT_MIN)
        return carry

    lax.fori_loop(0, n_chunks, score_body, 0)

    def select_body(it, thr_u):
        bit = lax.shift_left(jnp.int32(1), 31 - it)
        cand = thr_u | bit
        cand_key = cand ^ INT_MIN

        def count_body(c, cnt):
            hit = jnp.where(key_ref[c] >= cand_key, 1.0, 0.0)
            return cnt + jnp.sum(hit.reshape(kc // cnt_rows, cnt_rows, Q_BLOCK), axis=0)

        cnt = lax.fori_loop(0, n_chunks, count_body, jnp.zeros((cnt_rows, Q_BLOCK), F32))
        return jnp.where(jnp.sum(cnt, axis=0, keepdims=True) >= k_top, cand, thr_u)

    thr_u = lax.fori_loop(0, 32, select_body, jnp.zeros((1, Q_BLOCK), jnp.int32))
    thr = thr_u ^ INT_MIN

    _flash_init(m_ref, acc_ref)
    q_groups = [_stack_heads(qb_ref, g * DSA_HPG, DSA_HPG, DSA_DK) for g in range(DSA_KV_HEADS)]

    def attn_chunks(chunks):
        scores = []
        for c in chunks:
            k0 = pl.multiple_of(c * kc, kc)
            scores.append([
                lax.dot_general(q_groups[g], kb_ref[pl.ds(k0, kc), g * DSA_DK:(g + 1) * DSA_DK], DN_T,
                                preferred_element_type=F32).astype(BF16) for g in range(DSA_KV_HEADS)])
        for c, s_c in zip(chunks, scores):
            k0 = pl.multiple_of(c * kc, kc)
            bias = jnp.transpose(_mask_bias((key_ref[c] >= thr) & causal_t(c))).astype(BF16)
            for g in range(DSA_KV_HEADS):
                v_blk = vb_ref[pl.ds(k0, kc), g * DSA_DV:(g + 1) * DSA_DV]
                _flash_step(s_c[g], bias, _with_ones(v_blk), DSA_HPG, m_ref.at[g], acc_ref.at[g])

    _chunk_loop(n_chunks, attn_chunks)
    for g in range(DSA_KV_HEADS):
        o = _normalize(acc_ref[g], DSA_DV)
        for h in range(DSA_HPG):
            head = g * DSA_HPG + h
            o_ref[:, head * DSA_DV:(head + 1) * DSA_DV] = o[h * Q_BLOCK:(h + 1) * Q_BLOCK].astype(o_ref.dtype)


def _dsa_attention(p128, pv, small, cfg):
    b, s = cfg.batch, cfg.seq
    n_qb = s // Q_BLOCK
    kc = _tile(s, 512)
    k_top = min(DSA_TOPK, s // 4)
    qiw = IDX_HEADS * IDX_DIM
    qbw = DSA_HEADS * DSA_DK
    kbw = DSA_KV_HEADS * DSA_DK
    rows = DSA_HPG * Q_BLOCK
    est = (2 * (_nbytes((Q_BLOCK, qiw + qbw), BF16) + _nbytes((s, IDX_DIM + 2 * kbw), BF16))
           + _nbytes((Q_BLOCK, s), jnp.int32) + 12 * _nbytes((8 * Q_BLOCK, kc), F32) + (4 << 20))
    return pl.pallas_call(
        functools.partial(_dsa_kernel, kc=kc, k_top=k_top),
        grid=(b, n_qb),
        in_specs=[
            pl.BlockSpec((Q_BLOCK, qiw), lambda bi, i: (bi * n_qb + i, 0)),
            pl.BlockSpec((Q_BLOCK, V7X_LANES), lambda bi, i: (bi * n_qb + i, 0)),
            pl.BlockSpec((s, IDX_DIM), lambda bi, i: (bi, (qiw + qbw + kbw) // IDX_DIM)),
            pl.BlockSpec((Q_BLOCK, qbw), lambda bi, i: (bi * n_qb + i, qiw // qbw)),
            pl.BlockSpec((s, kbw), lambda bi, i: (bi, (qiw + qbw) // kbw)),
            pl.BlockSpec((s, kbw), lambda bi, i: (bi, 1)),
        ],
        out_specs=pl.BlockSpec((Q_BLOCK, DSA_HEADS * DSA_DV), lambda bi, i: (bi * n_qb + i, 0)),
        out_shape=jax.ShapeDtypeStruct((b * s, DSA_HEADS * DSA_DV), BF16),
        scratch_shapes=[
            pltpu.VMEM((s // kc, kc, Q_BLOCK), jnp.int32),
            pltpu.VMEM((DSA_KV_HEADS, rows, V7X_LANES), F32),
            pltpu.VMEM((DSA_KV_HEADS, rows, DSA_DV + V7X_LANES), F32),
        ],
        compiler_params=_params(("parallel", "arbitrary"), est),
        name="dsa_attention",
    )(p128, small, p128, p128, p128, pv)


def _cast_kernel(x_ref, o_ref):
    o_ref[...] = x_ref[...].astype(o_ref.dtype)


def _cast_bf16(w):
    k, n = w.shape
    tk, tn = _tile(k, 512), 2048
    est = 2 * _nbytes((tk, tn), F32) + 2 * _nbytes((tk, tn), BF16) + (2 << 20)
    return pl.pallas_call(
        _cast_kernel,
        grid=(k // tk, pl.cdiv(n, tn)),
        in_specs=[pl.BlockSpec((tk, tn), lambda i, j: (i, j))],
        out_specs=pl.BlockSpec((tk, tn), lambda i, j: (i, j)),
        out_shape=jax.ShapeDtypeStruct((k, n), BF16),
        compiler_params=_params(("parallel", "parallel"), est),
        name="cast_w_in",
    )(w)


def _pad_heads(w, heads, dk, dkp):
    d = w.shape[0]
    return jnp.pad(w.reshape(d, heads, dk), ((0, 0), (0, 0), (0, dkp - dk))).reshape(d, heads * dkp)


def _split_kv(w, heads, dk, dv):
    d = w.shape[0]
    w3 = w.reshape(d, heads, dk + dv)
    return w3[:, :, :dk].reshape(d, heads * dk), w3[:, :, dk:].reshape(d, heads * dv)


def _mixer_weights(w_in, d_model):
    sizes = (
        NSA_HEADS * NSA_DK, NSA_GROUPS * (NSA_DK + NSA_DV), NSA_GROUPS * (NSA_DK + NSA_DV),
        NSA_GROUPS * (NSA_DK + NSA_DV), 3 * NSA_HEADS, DSA_HEADS * DSA_DK, DSA_KV_HEADS * (DSA_DK + DSA_DV),
        IDX_HEADS * IDX_DIM, IDX_DIM, IDX_HEADS, 2 * d_model,
    )
    offs = np.cumsum((0,) + sizes)
    assert offs[-1] == w_in.shape[1]
    w_bf = _cast_bf16(w_in)
    q_a, kv_c, kv_s, kv_w, gate_a, q_b, kv_b, q_i, k_i, w_i, merge = (
        w_bf[:, offs[n]:offs[n + 1]] for n in range(len(sizes)))
    k_c, v_c = _split_kv(kv_c, NSA_GROUPS, NSA_DK, NSA_DV)
    k_s, v_s = _split_kv(kv_s, NSA_GROUPS, NSA_DK, NSA_DV)
    k_w, v_w = _split_kv(kv_w, NSA_GROUPS, NSA_DK, NSA_DV)
    k_b, v_b = _split_kv(kv_b, DSA_KV_HEADS, DSA_DK, DSA_DV)
    w192 = jnp.concatenate([
        _pad_heads(q_a, NSA_HEADS, NSA_DK, NSA_DKP), _pad_heads(k_s, NSA_GROUPS, NSA_DK, NSA_DKP),
        _pad_heads(k_w, NSA_GROUPS, NSA_DK, NSA_DKP)], axis=1)
    scale192 = jnp.concatenate([
        jnp.full((NSA_HEADS * NSA_DKP,), NSA_DK ** -0.5 * LOG2E, F32), jnp.ones((2 * NSA_GROUPS * NSA_DKP,), F32)])
    n128 = IDX_HEADS * IDX_DIM + DSA_HEADS * DSA_DK + DSA_KV_HEADS * DSA_DK + IDX_DIM
    n128_pad = -(-n128 // 1024) * 1024
    w128 = jnp.pad(jnp.concatenate([q_i, q_b, k_b, k_i], axis=1), ((0, 0), (0, n128_pad - n128)))
    scale128 = jnp.concatenate([
        jnp.full((IDX_HEADS * IDX_DIM,), IDX_DIM ** -0.5, F32),
        jnp.full((DSA_HEADS * DSA_DK,), DSA_DK ** -0.5 * LOG2E, F32),
        jnp.ones((n128_pad - IDX_HEADS * IDX_DIM - DSA_HEADS * DSA_DK,), F32)])
    wv = jnp.concatenate([v_s, v_w, v_b], axis=1)
    wc = jnp.concatenate([k_c, v_c], axis=1)
    small_n = GATE_COLS + IDX_HEADS
    wsmall = jnp.pad(jnp.concatenate([gate_a, w_i], axis=1), ((0, 0), (0, V7X_LANES - small_n)))
    return w192, scale192, w128, scale128, wv, wc, wsmall, merge


def _ffn_up_kernel(h_ref, w1_ref, w3_ref, o_ref, w1b_ref, w3b_ref):
    @pl.when(pl.program_id(1) == 0)
    def _():
        w1b_ref[...] = w1_ref[...].astype(BF16)
        w3b_ref[...] = w3_ref[...].astype(BF16)

    h = h_ref[...]
    a = jnp.dot(h, w1b_ref[...], preferred_element_type=F32)
    b = jnp.dot(h, w3b_ref[...], preferred_element_type=F32)
    o_ref[...] = (jax.nn.silu(a) * b).astype(o_ref.dtype)


def _ffn_up(h, w1, w3, cfg):
    t, d = h.shape
    ff = w1.shape[1]
    tm = _tile(cfg.seq, 1024)
    tn = _tile(ff, 256)
    est = (2 * _nbytes((tm, d), BF16) + 4 * _nbytes((d, tn), F32) + 3 * _nbytes((d, tn), BF16)
           + 2 * _nbytes((tm, tn), BF16) + 4 * _nbytes((tm, tn), F32) + (2 << 20))
    return pl.pallas_call(
        _ffn_up_kernel,
        grid=(ff // tn, t // tm),
        in_specs=[
            pl.BlockSpec((tm, d), lambda j, i: (i, 0)),
            pl.BlockSpec((d, tn), lambda j, i: (0, j)),
            pl.BlockSpec((d, tn), lambda j, i: (0, j)),
        ],
        out_specs=pl.BlockSpec((tm, tn), lambda j, i: (i, j)),
        out_shape=jax.ShapeDtypeStruct((t, ff), BF16),
        scratch_shapes=[pltpu.VMEM((d, tn), BF16), pltpu.VMEM((d, tn), BF16)],
        compiler_params=_params(("parallel", "arbitrary"), est),
        name="ffn_up",
    )(h, w1, w3)


def _ffn(x2, gain, mod3, idx, w1, w3, w2, cfg):
    t, d = x2.shape
    ff = w1.shape[1]
    h = _norm_mod(x2, gain, mod3, idx, idx + 1, cfg)
    u = _ffn_up(h, w1, w3, cfg)
    tm = _tile(cfg.seq, 512)
    nsb = cfg.seq // tm
    tn = _tile(d, 512)
    epi = [(x2, (tm, tn), lambda i, j: (i, j)),
           (mod3, (1, 1, tn), lambda i, j: ((i // nsb) * N_MOD + idx + 2, 0, j))]
    return _matmul([u], [w2.astype(BF16)], [(0, 0)], functools.partial(_epi_residual, coef=0.5), epi, d, F32,
                   tm=tm, tn=tn, name="ffn_down")


def _mixer(x2, gain, mod3, w_in, cmp_pe_k, cmp_w1_k, cmp_w2_k, cmp_pe_v, cmp_w1_v, cmp_w2_v,
           proj_a, proj_b, w_out, cfg):
    b, s, d = cfg.batch, cfg.seq, cfg.d_model
    w192, scale192, w128, scale128, wv, wc, wsmall, wmerge = _mixer_weights(w_in, d)
    h = _norm_mod(x2, gain, mod3, 3, 4, cfg)
    p192 = _project_rope(h, w192, scale192, NSA_DK, NSA_DKP, cfg, "proj_rope192")
    p128 = _project_rope(h, w128, scale128, DSA_DK, DSA_DK, cfg, "proj_rope128")
    pv = _project(h, wv, _epi_plain, BF16, cfg, "proj_values")
    pc = _project(h, wc, _epi_plain, F32, cfg, "proj_cmp")
    small = _project(h, wsmall, _epi_plain, F32, cfg, "proj_small")
    gates_m = _project(h, wmerge, _epi_sigmoid, BF16, cfg, "proj_merge")

    n_rows = s // CMP_STRIDE
    kdim = NSA_GROUPS * NSA_DK

    def chunks(cols, dd):
        xg = cols.reshape(b, n_rows, CMP_STRIDE, NSA_GROUPS, dd)
        return jnp.transpose(xg, (0, 3, 1, 2, 4)).reshape(b * NSA_GROUPS, n_rows, CMP_STRIDE * dd)

    cmp_pos = jnp.arange(n_rows) * CMP_STRIDE + CMP_BLOCK - 1
    k_cmp = _compress(chunks(pc[:, :kdim], NSA_DK), cmp_pe_k, cmp_w1_k, cmp_w2_k, NSA_DK, NSA_DKP, cmp_pos,
                      "compress_k").reshape(b, NSA_GROUPS, n_rows, NSA_DKP)
    v_cmp = _compress(chunks(pc[:, kdim:], NSA_DV), cmp_pe_v, cmp_w1_v, cmp_w2_v, NSA_DV, NSA_DV, None,
                      "compress_v").reshape(b, NSA_GROUPS, n_rows, NSA_DV)

    o_a = _nsa_attention(p192, pv, small, k_cmp, v_cmp, cfg)
    o_b = _dsa_attention(p128, pv, small, cfg)

    tm = _tile(s, 1024)
    tn = _tile(d, 1024)
    nsb = s // tm
    epi = [(gates_m, (tm, tn), lambda i, j: (i, j)), (gates_m, (tm, tn), lambda i, j: (i, j + d // tn))]
    y = _matmul([o_a, o_b], [proj_a.astype(BF16), proj_b.astype(BF16)], [(0, 0), (1, 1)], _epi_merge, epi,
                d, BF16, tm=tm, tn=tn, name="merge_proj")
    epi = [(x2, (tm, tn), lambda i, j: (i, j)),
           (mod3, (1, 1, tn), lambda i, j: ((i // nsb) * N_MOD + 5, 0, j))]
    return _matmul([y], [w_out.astype(BF16)], [(0, 0)], functools.partial(_epi_residual, coef=1.0), epi,
                   d, F32, tm=tm, tn=tn, name="out_proj")


def _layer(x, c, ada_w, ada_b, norm_ffn1, ffn1_w1, ffn1_w3, ffn1_w2, norm_mix, w_in,
           cmp_pe_k, cmp_w1_k, cmp_w2_k, cmp_pe_v, cmp_w1_v, cmp_w2_v, proj_a, proj_b, w_out,
           norm_ffn2, ffn2_w1, ffn2_w3, ffn2_w2, norm_final):
    b, s, d = x.shape
    cfg = Cfg(b, s, d, ffn1_w1.shape[-1])
    assert s % Q_BLOCK == 0 and s >= WINDOW + Q_BLOCK and d % V7X_LANES == 0
    x2 = x.reshape(b * s, d)
    c_pad = jnp.pad(c, ((0, 8 - b), (0, 0)))
    for l in range(ada_w.shape[0]):
        mod = _adaln(c_pad, ada_w[l], ada_b[l].reshape(1, -1))
        mod3 = mod[:b].reshape(b * N_MOD, 1, d)
        x2 = _ffn(x2, norm_ffn1[l], mod3, 0, ffn1_w1[l], ffn1_w3[l], ffn1_w2[l], cfg)
        x2 = _mixer(x2, norm_mix[l], mod3, w_in[l], cmp_pe_k[l], cmp_w1_k[l], cmp_w2_k[l],
                    cmp_pe_v[l], cmp_w1_v[l], cmp_w2_v[l], proj_a[l], proj_b[l], w_out[l], cfg)
        x2 = _ffn(x2, norm_ffn2[l], mod3, 6, ffn2_w1[l], ffn2_w3[l], ffn2_w2[l], cfg)
    return _final_norm(x2, norm_final).reshape(b, s, d)


def kernel(x, c, ada_w, ada_b, norm_ffn1, ffn1_w1, ffn1_w3, ffn1_w2, norm_mix, w_in, cmp_pe_k, cmp_w1_k, cmp_w2_k, cmp_pe_v, cmp_w1_v, cmp_w2_v, proj_a, proj_b, w_out, norm_ffn2, ffn2_w1, ffn2_w3, ffn2_w2, norm_final):
    return _layer(x, c, ada_w, ada_b, norm_ffn1, ffn1_w1, ffn1_w3, ffn1_w2, norm_mix, w_in,
                  cmp_pe_k, cmp_w1_k, cmp_w2_k, cmp_pe_v, cmp_w1_v, cmp_w2_v, proj_a, proj_b, w_out,
                  norm_ffn2, ffn2_w1, ffn2_w3, ffn2_w2, norm_final)
```

```python
import functools
import math
from typing import NamedTuple

import numpy as np
import jax
import jax.numpy as jnp
from jax import lax
from jax.experimental import pallas as pl
from jax.experimental.pallas import tpu as pltpu

F32 = jnp.float32
BF16 = jnp.bfloat16

V7X_LANES = 128
V7X_VMEM_BYTES = 64 * 1024 * 1024
V7X_VMEM_BUDGET = 56 * 1024 * 1024

EPS = 1e-6
ROPE_THETA = 500000.0
ROPE_FRACTION = 4
N_MOD = 9
Q_BLOCK = 128

NSA_HEADS = 16
NSA_GROUPS = 2
NSA_HPG = NSA_HEADS // NSA_GROUPS
NSA_DK = 192
NSA_DKP = 256
NSA_DV = 128
CMP_BLOCK = 32
CMP_STRIDE = 16
SEL_BLOCK = 64
SEL_TOP = 16
WINDOW = 512

DSA_HEADS = 16
DSA_KV_HEADS = 4
DSA_HPG = DSA_HEADS // DSA_KV_HEADS
DSA_DK = 128
DSA_DV = 128
IDX_HEADS = 32
IDX_DIM = 128
DSA_TOPK = 256

GATE_COLS = 3 * NSA_HEADS
INT_MIN = np.int32(-(2**31))
NEG_INF = float("-inf")
LOG2E = math.log2(math.e)
TINY = float(np.finfo(np.float32).tiny)
DN_T = (((1,), (1,)), ((), ()))


class Cfg(NamedTuple):
    batch: int
    seq: int
    d_model: int
    d_ff: int


def _tile(n, pref):
    if n <= pref:
        return n
    t = (pref // V7X_LANES) * V7X_LANES
    while t > V7X_LANES and n % t:
        t -= V7X_LANES
    assert n % t == 0, (n, pref)
    return t


def _nbytes(shape, dtype):
    return int(np.prod(shape)) * jnp.dtype(dtype).itemsize


def _params(dims, est_bytes):
    limit = int(min(V7X_VMEM_BUDGET, max(est_bytes, 16 * 1024 * 1024)))
    return pltpu.CompilerParams(dimension_semantics=dims, vmem_limit_bytes=limit)


def _adaln_kernel(c_ref, w_ref, b_ref, o_ref, acc_ref, *, nk):
    k = pl.program_id(1)

    @pl.when(k == 0)
    def _():
        acc_ref[...] = jnp.zeros_like(acc_ref)

    c = c_ref[...]
    c_act = (c * jax.nn.sigmoid(c)).astype(BF16)
    acc_ref[...] += jnp.dot(c_act, w_ref[...].astype(BF16), preferred_element_type=F32)

    @pl.when(k == nk - 1)
    def _():
        o_ref[...] = acc_ref[...] + b_ref[...]


def _adaln(c_pad, w, b):
    m, kdim = c_pad.shape
    n = w.shape[1]
    tn, tk = _tile(n, 2048), _tile(kdim, 1024)
    nk = kdim // tk
    est = 2 * _nbytes((tk, tn), F32) + _nbytes((tk, tn), BF16) + 4 * _nbytes((m, tn), F32) + (2 << 20)
    return pl.pallas_call(
        functools.partial(_adaln_kernel, nk=nk),
        grid=(n // tn, nk),
        in_specs=[
            pl.BlockSpec((m, tk), lambda j, k: (0, k)),
            pl.BlockSpec((tk, tn), lambda j, k: (k, j)),
            pl.BlockSpec((1, tn), lambda j, k: (0, j)),
        ],
        out_specs=pl.BlockSpec((m, tn), lambda j, k: (0, j)),
        out_shape=jax.ShapeDtypeStruct((m, n), F32),
        scratch_shapes=[pltpu.VMEM((m, tn), F32)],
        compiler_params=_params(("parallel", "arbitrary"), est),
        name="adaln",
    )(c_pad, w, b)


def _norm_mod_kernel(x_ref, g_ref, sh_ref, sc_ref, o_ref):
    x = x_ref[...]
    y = x * lax.rsqrt(jnp.mean(x * x, axis=-1, keepdims=True) + EPS)
    y = y * g_ref[...]
    o_ref[...] = (y * (1.0 + sc_ref[0]) + sh_ref[0]).astype(o_ref.dtype)


def _norm_mod(x2, gain, mod3, shift_idx, scale_idx, cfg):
    t, d = x2.shape
    tm = _tile(cfg.seq, 512)
    nsb = cfg.seq // tm
    est = 4 * _nbytes((tm, d), F32) + 2 * _nbytes((tm, d), BF16) + (2 << 20)
    return pl.pallas_call(
        _norm_mod_kernel,
        grid=(t // tm,),
        in_specs=[
            pl.BlockSpec((tm, d), lambda i: (i, 0)),
            pl.BlockSpec((1, d), lambda i: (0, 0)),
            pl.BlockSpec((1, 1, d), lambda i: ((i // nsb) * N_MOD + shift_idx, 0, 0)),
            pl.BlockSpec((1, 1, d), lambda i: ((i // nsb) * N_MOD + scale_idx, 0, 0)),
        ],
        out_specs=pl.BlockSpec((tm, d), lambda i: (i, 0)),
        out_shape=jax.ShapeDtypeStruct((t, d), BF16),
        compiler_params=_params(("parallel",), est),
        name="norm_mod",
    )(x2, gain.reshape(1, d), mod3, mod3)


def _final_norm_kernel(x_ref, g_ref, o_ref):
    x = x_ref[...]
    y = x * lax.rsqrt(jnp.mean(x * x, axis=-1, keepdims=True) + EPS)
    o_ref[...] = y * g_ref[...]


def _final_norm(x2, gain):
    t, d = x2.shape
    tm = _tile(t, 512)
    est = 6 * _nbytes((tm, d), F32) + (2 << 20)
    return pl.pallas_call(
        _final_norm_kernel,
        grid=(t // tm,),
        in_specs=[pl.BlockSpec((tm, d), lambda i: (i, 0)), pl.BlockSpec((1, d), lambda i: (0, 0))],
        out_specs=pl.BlockSpec((tm, d), lambda i: (i, 0)),
        out_shape=jax.ShapeDtypeStruct((t, d), F32),
        compiler_params=_params(("parallel",), est),
        name="final_norm",
    )(x2, gain.reshape(1, d))


MM_SUB = 256


def _mm_kernel(*refs, n_a, n_w, pairs, n_epi, epilogue, sub, w_rows):
    a_refs = refs[:n_a]
    w_refs = refs[n_a:n_a + n_w]
    e_refs = refs[n_a + n_w:n_a + n_w + n_epi]
    o_ref = refs[n_a + n_w + n_epi]
    for c in range(o_ref.shape[1] // sub):
        cols = slice(c * sub, (c + 1) * sub)
        if w_rows:
            accs = [lax.dot_general(a_refs[ai][...], w_refs[wi][cols, :], DN_T, preferred_element_type=F32)
                    for ai, wi in pairs]
        else:
            accs = [jnp.dot(a_refs[ai][...], w_refs[wi][:, cols], preferred_element_type=F32)
                    for ai, wi in pairs]
        o_ref[:, cols] = epilogue(accs, e_refs, cols).astype(o_ref.dtype)


def _matmul(a_list, w_list, pairs, epilogue, epi_inputs, n_out, out_dtype, *, tm, tn, name, w_rows=False,
            w_row0=0):
    m = a_list[0].shape[0]
    assert m % tm == 0
    sub = MM_SUB if tn % MM_SUB == 0 else tn
    in_specs, est = [], 0
    for a in a_list:
        in_specs.append(pl.BlockSpec((tm, a.shape[1]), lambda i, j: (i, 0)))
        est += 2 * _nbytes((tm, a.shape[1]), a.dtype)
    for w in w_list:
        if w_rows:
            kdim = w.shape[1]
            row_tile = 8 * 4 // jnp.dtype(w.dtype).itemsize
            assert w_row0 % row_tile == 0 and tn % row_tile == 0
            in_specs.append(pl.BlockSpec(
                (pl.Element(tn), pl.Element(kdim)),
                lambda i, j, row_tile=row_tile: (pl.multiple_of(w_row0 + j * tn, row_tile), 0)))
        else:
            kdim = w.shape[0]
            in_specs.append(pl.BlockSpec((kdim, tn), lambda i, j: (0, j)))
        est += 2 * _nbytes((kdim, tn), w.dtype)
    for arr, bshape, imap in epi_inputs:
        in_specs.append(pl.BlockSpec(bshape, imap))
        est += 2 * _nbytes(bshape, arr.dtype)
    est += 2 * _nbytes((tm, tn), out_dtype) + 4 * (len(pairs) + 3) * _nbytes((tm, sub), F32) + (2 << 20)
    kern = functools.partial(_mm_kernel, n_a=len(a_list), n_w=len(w_list), pairs=tuple(pairs),
                             n_epi=len(epi_inputs), epilogue=epilogue, sub=sub, w_rows=w_rows)
    return pl.pallas_call(
        kern,
        grid=(m // tm, pl.cdiv(n_out, tn)),
        in_specs=in_specs,
        out_specs=pl.BlockSpec((tm, tn), lambda i, j: (i, j)),
        out_shape=jax.ShapeDtypeStruct((m, n_out), out_dtype),
        compiler_params=_params(("parallel", "parallel"), est),
        name=name,
    )(*a_list, *w_list, *[e[0] for e in epi_inputs])


def _epi_plain(accs, e_refs, cols):
    return accs[0]


def _epi_sigmoid(accs, e_refs, cols):
    return jax.nn.sigmoid(accs[0])


def _epi_merge(accs, e_refs, cols):
    return e_refs[0][:, cols].astype(F32) * accs[0] + e_refs[1][:, cols].astype(F32) * accs[1]


def _epi_residual(accs, e_refs, cols, *, coef):
    return e_refs[0][:, cols] + (coef * e_refs[1][0][:, cols]) * accs[0]


def _rope_apply(a, cos, sin_lo, sin_hi, half):
    width = a.shape[-1]
    reps = (1, width // cos.shape[-1])
    return (a * jnp.tile(cos, reps) + pltpu.roll(a, width - half, 1) * jnp.tile(sin_lo, reps)
            + pltpu.roll(a, half, 1) * jnp.tile(sin_hi, reps))


def _epi_rope(accs, e_refs, cols, *, half):
    cos, sin_lo, sin_hi = e_refs[0][...], e_refs[1][...], e_refs[2][...]
    if len(e_refs) > 4:
        flag = e_refs[4][:, cols]
        reps = (1, flag.shape[-1] // cos.shape[-1])
        cos = 1.0 + flag * (jnp.tile(cos, reps) - 1.0)
        sin_lo, sin_hi = flag * jnp.tile(sin_lo, reps), flag * jnp.tile(sin_hi, reps)
    return _rope_apply(accs[0], cos, sin_lo, sin_hi, half) * e_refs[3][:, cols]


def _rope_tables(pos, dk, head_w):
    r = dk // ROPE_FRACTION
    half = r // 2
    n = pos.shape[0]
    inv = ROPE_THETA ** (-jnp.arange(0, r, 2, dtype=F32) / r)
    ang = pos.astype(F32)[:, None] * inv[None, :]
    cos, sin = jnp.cos(ang), jnp.sin(ang)
    cos_t = jnp.concatenate([cos, cos, jnp.ones((n, head_w - r), F32)], axis=1)
    sin_lo = jnp.concatenate([-sin, jnp.zeros((n, head_w - half), F32)], axis=1)
    sin_hi = jnp.concatenate([jnp.zeros((n, half), F32), sin, jnp.zeros((n, head_w - r), F32)], axis=1)
    return cos_t, sin_lo, sin_hi, half


def _project_rope(h, w_t, row0, colscale, flag, dk, head_w, cfg, name):
    n = colscale.shape[0]
    tm = _tile(cfg.seq, 1024)
    tn = min(n, 1024)
    nsb = cfg.seq // tm
    cos_t, sin_lo, sin_hi, half = _rope_tables(jnp.arange(cfg.seq), dk, head_w)
    epi = [(tb, (tm, head_w), lambda i, j: (i % nsb, 0)) for tb in (cos_t, sin_lo, sin_hi)]
    epi.append((colscale.reshape(1, n), (1, tn), lambda i, j: (0, j)))
    if flag is not None:
        epi.append((flag.reshape(1, n), (1, tn), lambda i, j: (0, j)))
    return _matmul([h], [w_t], [(0, 0)], functools.partial(_epi_rope, half=half), epi, n, BF16,
                   tm=tm, tn=tn, name=name, w_rows=True, w_row0=row0)


def _project(h, w_t, row0, n, epilogue, out_dtype, cfg, name):
    return _matmul([h], [w_t], [(0, 0)], epilogue, [], n, out_dtype,
                   tm=_tile(cfg.seq, 1024), tn=_tile(n, 1024), name=name, w_rows=True, w_row0=row0)


def _gelu_tanh(x):
    return 0.5 * x * (1.0 + jnp.tanh(math.sqrt(2.0 / math.pi) * (x + 0.044715 * (x * x * x))))


def _compress_kernel(x_ref, pe_ref, w1_ref, w2_ref, *rest, half):
    o_ref = rest[-1]
    x = x_ref[0]
    n_rows, width = x.shape
    a_lo = (x + pe_ref[0:1, :]).astype(BF16)
    a_hi = (x + pe_ref[1:2, :]).astype(BF16)
    p = jnp.dot(a_lo, w1_ref[0:width, :], preferred_element_type=F32)
    q = jnp.dot(a_hi, w1_ref[width:2 * width, :], preferred_element_type=F32)
    hid = _gelu_tanh(p + pltpu.roll(q, n_rows - 1, 0))
    y = jnp.dot(hid.astype(BF16), w2_ref[...], preferred_element_type=F32)
    if half:
        y = _rope_apply(y, rest[0][...], rest[1][...], rest[2][...], half)
    o_ref[0] = y.astype(o_ref.dtype)


def _compress(x_chunks, pe, w1, w2, d, d_out, rope_pos, name):
    bg, n_rows, width = x_chunks.shape
    hid_w = ((d + V7X_LANES - 1) // V7X_LANES) * V7X_LANES
    pe2 = pe.reshape(2, width)
    w1p = jnp.pad(w1.astype(BF16), ((0, 0), (0, hid_w - d)))
    w2p = jnp.pad(w2.astype(BF16), ((0, hid_w - d), (0, d_out - d)))
    ins = [x_chunks, pe2, w1p, w2p]
    in_specs = [
        pl.BlockSpec((1, n_rows, width), lambda i: (i, 0, 0)),
        pl.BlockSpec((2, width), lambda i: (0, 0)),
        pl.BlockSpec((2 * width, hid_w), lambda i: (0, 0)),
        pl.BlockSpec((hid_w, d_out), lambda i: (0, 0)),
    ]
    half = 0
    if rope_pos is not None:
        cos_t, sin_lo, sin_hi, half = _rope_tables(rope_pos, d, d_out)
        ins += [cos_t, sin_lo, sin_hi]
        in_specs += [pl.BlockSpec((n_rows, d_out), lambda i: (0, 0))] * 3
    est = (2 * _nbytes((n_rows, width), F32) + 2 * _nbytes((n_rows, width), BF16)
           + 2 * _nbytes((2 * width, hid_w), BF16) + (8 << 20))
    return pl.pallas_call(
        functools.partial(_compress_kernel, half=half),
        grid=(bg,),
        in_specs=in_specs,
        out_specs=pl.BlockSpec((1, n_rows, d_out), lambda i: (i, 0, 0)),
        out_shape=jax.ShapeDtypeStruct((bg, n_rows, d_out), BF16),
        compiler_params=_params(("parallel",), est),
        name=name,
    )(*ins)


def _stack_heads(ref, first, count, width):
    return jnp.concatenate([ref[:, (first + h) * width:(first + h + 1) * width] for h in range(count)], axis=0)


def _mask_bias(mask):
    return jnp.where(mask, 0.0, NEG_INF)


def _with_ones(v):
    return jnp.concatenate([v, jnp.ones((v.shape[0], V7X_LANES), v.dtype)], axis=1)


def _exp2_masked(s3, bias2):
    z = s3 + bias2[None]
    m = jnp.max(z, axis=-1, keepdims=True)
    m = jnp.where(m > NEG_INF, m, 0.0)
    return jnp.exp2(z - m)


def _normalize(pv, dv):
    return pv[:, :dv] / jnp.maximum(pv[:, dv:dv + V7X_LANES], TINY)


def _flash_init(m_ref, acc_ref):
    m_ref[...] = jnp.full(m_ref.shape, NEG_INF, F32)
    acc_ref[...] = jnp.zeros(acc_ref.shape, F32)


def _flash_step(s, bias2, v_ones, heads, m_ref, acc_ref):
    rows, kc = s.shape
    z = (s.reshape(heads, rows // heads, kc) + bias2[None]).reshape(rows, kc)
    m_old = m_ref[...]
    m_new = jnp.maximum(m_old, jnp.max(z, axis=-1, keepdims=True).astype(F32))
    m_safe = jnp.where(m_new > NEG_INF, m_new, 0.0)
    alpha = jnp.exp2(m_old - m_safe)
    p = jnp.exp2(z - jnp.tile(m_safe.astype(z.dtype), (1, kc // V7X_LANES)))
    pv = jnp.dot(p.astype(BF16), v_ones, preferred_element_type=F32)
    acc_ref[...] = jnp.tile(alpha, (1, acc_ref.shape[-1] // V7X_LANES)) * acc_ref[...] + pv
    m_ref[...] = m_new


def _chunk_loop(n_chunks, process):
    def pair_body(j, carry):
        process([2 * j, 2 * j + 1])
        return carry

    lax.fori_loop(0, n_chunks // 2, pair_body, 0)

    @pl.when(n_chunks % 2 == 1)
    def _():
        process([n_chunks - 1])


def _nsa_kernel(q_ref, gate_ref, kc_ref, vc_ref, ks_ref, vs_ref, kw_ref, vw_ref, aggt_ref, o_ref,
                m_ref, acc_ref, *, seq, kc, blk_pad):
    n_cmp = (seq - CMP_BLOCK) // CMP_STRIDE + 1
    n_blk = seq // SEL_BLOCK
    n_sel = min(SEL_TOP, n_blk)
    cmp_pad = aggt_ref.shape[1]
    q0 = pl.program_id(1) * Q_BLOCK
    t_col = q0 + lax.broadcasted_iota(jnp.int32, (Q_BLOCK, 1), 0)
    t_row = q0 + lax.broadcasted_iota(jnp.int32, (1, Q_BLOCK), 1)
    gates = jax.nn.sigmoid(gate_ref[...])

    cidx = lax.broadcasted_iota(jnp.int32, (Q_BLOCK, cmp_pad), 1)
    bias_c = _mask_bias((cidx * CMP_STRIDE + (CMP_BLOCK - 1) <= t_col) & (cidx < n_cmp))

    blk = lax.broadcasted_iota(jnp.int32, (n_blk, Q_BLOCK), 0)
    cur = lax.shift_right_logical(t_row, int(math.log2(SEL_BLOCK)))
    forced = (blk == 0) | (blk == cur) | (blk == cur - 1)
    valid = blk * SEL_BLOCK <= t_row

    win_k = WINDOW + Q_BLOCK
    ws = pl.multiple_of(jnp.maximum(q0 - WINDOW, 0), Q_BLOCK)
    diff = t_col - (ws + lax.broadcasted_iota(jnp.int32, (1, win_k), 1))
    bias_w = _mask_bias((diff >= 0) & (diff < WINDOW)).astype(BF16)
    n_chunks = (q0 + Q_BLOCK + kc - 1) // kc
    rows = NSA_HPG * Q_BLOCK

    q_groups = [_stack_heads(q_ref, g * NSA_HPG, NSA_HPG, NSA_DKP) for g in range(NSA_GROUPS)]
    o_cmp, sels = [], []
    for g in range(NSA_GROUPS):
        qg = q_groups[g]

        s = lax.dot_general(qg, kc_ref[0, g], DN_T, preferred_element_type=F32)
        e_c = _exp2_masked(s.reshape(NSA_HPG, Q_BLOCK, cmp_pad), bias_c)
        p_c = e_c / jnp.maximum(jnp.sum(e_c, axis=-1, keepdims=True), TINY)
        o_c = jnp.dot(p_c.reshape(rows, cmp_pad).astype(BF16), vc_ref[0, g], preferred_element_type=F32)

        p_sum = jnp.sum(p_c, axis=0)
        hi = p_sum.astype(BF16)
        r1 = p_sum - hi.astype(F32)
        mid = r1.astype(BF16)
        lo = (r1 - mid.astype(F32)).astype(BF16)
        aggt = aggt_ref[...]
        imp = (lax.dot_general(aggt, hi, DN_T, preferred_element_type=F32)
               + lax.dot_general(aggt, mid, DN_T, preferred_element_type=F32)
               + lax.dot_general(aggt, lo, DN_T, preferred_element_type=F32))
        val = jnp.where(forced, jnp.inf, jnp.where(valid, imp, NEG_INF))
        rank = jnp.zeros((n_blk, Q_BLOCK), F32)
        for ii in range(n_blk):
            row = val[ii:ii + 1, :]
            beats = (row > val) | ((row == val) & (blk > ii))
            rank = rank + jnp.where(beats, 1.0, 0.0)
        sel_t = jnp.where((rank < n_sel) & (val > NEG_INF), 1.0, 0.0)
        if blk_pad > n_blk:
            sel_t = jnp.concatenate([sel_t, jnp.zeros((blk_pad - n_blk, Q_BLOCK), F32)], axis=0)
        sels.append(jnp.transpose(sel_t).astype(BF16))
        o_cmp.append(o_c)

    _flash_init(m_ref, acc_ref)

    def sel_chunks(chunks):
        scores = []
        for c in chunks:
            k0 = pl.multiple_of(c * kc, kc)
            scores.append([
                lax.dot_general(q_groups[g], ks_ref[pl.ds(k0, kc), g * NSA_DKP:(g + 1) * NSA_DKP], DN_T,
                                preferred_element_type=F32).astype(BF16) for g in range(NSA_GROUPS)])
        for c, s_c in zip(chunks, scores):
            k0 = pl.multiple_of(c * kc, kc)
            kpos_b = k0 + lax.broadcasted_iota(jnp.int32, (blk_pad, kc), 1)
            expand = jnp.where(
                lax.shift_right_logical(kpos_b, int(math.log2(SEL_BLOCK)))
                == lax.broadcasted_iota(jnp.int32, (blk_pad, kc), 0), 1.0, 0.0).astype(BF16)
            causal = k0 + lax.broadcasted_iota(jnp.int32, (1, kc), 1) <= t_col
            for g in range(NSA_GROUPS):
                v_blk = vs_ref[pl.ds(k0, kc), g * NSA_DV:(g + 1) * NSA_DV]
                sel_keys = jnp.dot(sels[g], expand, preferred_element_type=F32)
                bias = _mask_bias((sel_keys > 0.5) & causal).astype(BF16)
                _flash_step(s_c[g], bias, _with_ones(v_blk), NSA_HPG, m_ref.at[g], acc_ref.at[g])

    _chunk_loop(n_chunks, sel_chunks)

    for g in range(NSA_GROUPS):
        qg, o_c = q_groups[g], o_cmp[g]
        o_s = _normalize(acc_ref[g], NSA_DV)

        k_win = kw_ref[pl.ds(ws, win_k), g * NSA_DKP:(g + 1) * NSA_DKP]
        v_win = vw_ref[pl.ds(ws, win_k), g * NSA_DV:(g + 1) * NSA_DV]
        s_w = lax.dot_general(qg, k_win, DN_T, preferred_element_type=F32).astype(BF16)
        e_w = _exp2_masked(s_w.reshape(NSA_HPG, Q_BLOCK, win_k), bias_w)
        o_w = _normalize(jnp.dot(e_w.reshape(rows, win_k).astype(BF16), _with_ones(v_win),
                                 preferred_element_type=F32), NSA_DV)

        for h in range(NSA_HPG):
            c0 = (g * NSA_HPG + h) * 3
            hr = slice(h * Q_BLOCK, (h + 1) * Q_BLOCK)
            o = (gates[:, c0:c0 + 1] * o_c[hr] + gates[:, c0 + 1:c0 + 2] * o_s[hr]
                 + gates[:, c0 + 2:c0 + 3] * o_w[hr])
            head = g * NSA_HPG + h
            o_ref[:, head * NSA_DV:(head + 1) * NSA_DV] = o.astype(o_ref.dtype)


def _agg_t(seq, cmp_pad):
    n_cmp = (seq - CMP_BLOCK) // CMP_STRIDE + 1
    n_blk = seq // SEL_BLOCK
    r = SEL_BLOCK // CMP_STRIDE
    c = CMP_BLOCK // CMP_STRIDE
    j = np.arange(n_blk)[:, None, None, None]
    i = np.arange(cmp_pad)[None, :, None, None]
    m = np.arange(r)[None, None, :, None]
    n = np.arange(c)[None, None, None, :]
    a = np.sum(i == r * j + m - n, axis=(2, 3)).astype(np.float32)
    a = a * (np.arange(cmp_pad)[None, :] < n_cmp)
    return jnp.asarray(a, dtype=BF16)


def _nsa_attention(p192, pv, small, k_cmp, v_cmp, cfg):
    b, s = cfg.batch, cfg.seq
    n_qb = s // Q_BLOCK
    kc = _tile(s, 512)
    cmp_pad = k_cmp.shape[2]
    blk_pad = max(V7X_LANES, s // SEL_BLOCK)
    qw = NSA_HEADS * NSA_DKP
    kw = NSA_GROUPS * NSA_DKP
    vw = NSA_GROUPS * NSA_DV
    rows = NSA_HPG * Q_BLOCK
    est = (2 * (_nbytes((Q_BLOCK, qw), BF16) + 2 * _nbytes((s, kw), BF16) + 2 * _nbytes((s, vw), BF16))
           + 3 * _nbytes((rows, V7X_LANES), F32) + 10 * _nbytes((rows, WINDOW + Q_BLOCK), F32) + (4 << 20))
    return pl.pallas_call(
        functools.partial(_nsa_kernel, seq=s, kc=kc, blk_pad=blk_pad),
        grid=(b, n_qb),
        in_specs=[
            pl.BlockSpec((Q_BLOCK, qw), lambda bi, i: (bi * n_qb + i, 0)),
            pl.BlockSpec((Q_BLOCK, V7X_LANES), lambda bi, i: (bi * n_qb + i, 0)),
            pl.BlockSpec((1, NSA_GROUPS, cmp_pad, NSA_DKP), lambda bi, i: (bi, 0, 0, 0)),
            pl.BlockSpec((1, NSA_GROUPS, cmp_pad, NSA_DV), lambda bi, i: (bi, 0, 0, 0)),
            pl.BlockSpec((s, kw), lambda bi, i: (bi, qw // kw)),
            pl.BlockSpec((s, vw), lambda bi, i: (bi, 0)),
            pl.BlockSpec((s, kw), lambda bi, i: (bi, qw // kw + 1)),
            pl.BlockSpec((s, vw), lambda bi, i: (bi, 1)),
            pl.BlockSpec((s // SEL_BLOCK, cmp_pad), lambda bi, i: (0, 0)),
        ],
        out_specs=pl.BlockSpec((Q_BLOCK, NSA_HEADS * NSA_DV), lambda bi, i: (bi * n_qb + i, 0)),
        out_shape=jax.ShapeDtypeStruct((b * s, NSA_HEADS * NSA_DV), BF16),
        scratch_shapes=[pltpu.VMEM((NSA_GROUPS, rows, V7X_LANES), F32),
                        pltpu.VMEM((NSA_GROUPS, rows, NSA_DV + V7X_LANES), F32)],
        compiler_params=_params(("parallel", "arbitrary"), est),
        name="nsa_attention",
    )(p192, small, k_cmp, v_cmp, p192, pv, p192, pv, _agg_t(s, cmp_pad))


def _dsa_kernel(*refs, kc, k_top):
    hg = 8
    n_qi = IDX_HEADS // hg
    qi_refs = refs[:n_qi]
    w_ref, ki_ref, qb_ref = refs[n_qi:n_qi + 3]
    kb_refs = refs[n_qi + 3:n_qi + 3 + DSA_KV_HEADS]
    vb_refs = refs[n_qi + 3 + DSA_KV_HEADS:n_qi + 3 + 2 * DSA_KV_HEADS]
    o_ref, key_ref, m_ref, acc_ref = refs[n_qi + 3 + 2 * DSA_KV_HEADS:]
    q0 = pl.program_id(1) * Q_BLOCK
    t_row = q0 + lax.broadcasted_iota(jnp.int32, (1, Q_BLOCK), 1)
    n_chunks = (q0 + Q_BLOCK + kc - 1) // kc
    w_t = jnp.transpose(w_ref[...] * (IDX_HEADS ** -0.5))
    cnt_rows = 64

    def causal_t(c):
        return c * kc + lax.broadcasted_iota(jnp.int32, (kc, 1), 0) <= t_row

    def score_body(c, carry):
        k0 = pl.multiple_of(c * kc, kc)
        k_i = ki_ref[pl.ds(k0, kc), :]
        acc = jnp.zeros((kc, Q_BLOCK), F32)
        for g in range(IDX_HEADS // hg):
            q_h = _stack_heads(qi_refs[g], 0, hg, IDX_DIM)
            rel = jnp.maximum(lax.dot_general(k_i, q_h, DN_T, preferred_element_type=F32), 0.0)
            for h in range(hg):
                col = GATE_COLS + g * hg + h
                acc = acc + rel[:, h * Q_BLOCK:(h + 1) * Q_BLOCK] * w_t[col:col + 1, :]
        bits = lax.bitcast_convert_type(acc, jnp.int32)
        skey = bits ^ (lax.shift_right_arithmetic(bits, 31) & np.int32(0x7FFFFFFF))
        key_ref[c] = jnp.where(causal_t(c), skey, INT_MIN)
        return carry

    lax.fori_loop(0, n_chunks, score_body, 0)

    def select_body(it, thr_u):
        bit = lax.shift_left(jnp.int32(1), 31 - it)
        cand = thr_u | bit
        cand_key = cand ^ INT_MIN

        def count_body(c, cnt):
            hit = jnp.where(key_ref[c] >= cand_key, 1.0, 0.0)
            return cnt + jnp.sum(hit.reshape(kc // cnt_rows, cnt_rows, Q_BLOCK), axis=0)

        cnt = lax.fori_loop(0, n_chunks, count_body, jnp.zeros((cnt_rows, Q_BLOCK), F32))
        return jnp.where(jnp.sum(cnt, axis=0, keepdims=True) >= k_top, cand, thr_u)

    thr_u = lax.fori_loop(0, 32, select_body, jnp.zeros((1, Q_BLOCK), jnp.int32))
    thr = thr_u ^ INT_MIN

    _flash_init(m_ref, acc_ref)
    q_groups = [_stack_heads(qb_ref, g * DSA_HPG, DSA_HPG, DSA_DK) for g in range(DSA_KV_HEADS)]

    def attn_chunks(chunks):
        scores = []
        for c in chunks:
            k0 = pl.multiple_of(c * kc, kc)
            scores.append([
                lax.dot_general(q_groups[g], kb_refs[g][pl.ds(k0, kc), :], DN_T,
                                preferred_element_type=F32).astype(BF16) for g in range(DSA_KV_HEADS)])
        for c, s_c in zip(chunks, scores):
            k0 = pl.multiple_of(c * kc, kc)
            bias = jnp.transpose(_mask_bias((key_ref[c] >= thr) & causal_t(c))).astype(BF16)
            for g in range(DSA_KV_HEADS):
                v_blk = vb_refs[g][pl.ds(k0, kc), :]
                _flash_step(s_c[g], bias, _with_ones(v_blk), DSA_HPG, m_ref.at[g], acc_ref.at[g])

    _chunk_loop(n_chunks, attn_chunks)
    for g in range(DSA_KV_HEADS):
        o = _normalize(acc_ref[g], DSA_DV)
        for h in range(DSA_HPG):
            head = g * DSA_HPG + h
            o_ref[:, head * DSA_DV:(head + 1) * DSA_DV] = o[h * Q_BLOCK:(h + 1) * Q_BLOCK].astype(o_ref.dtype)


def _dsa_attention(p128, small, cfg):
    b, s = cfg.batch, cfg.seq
    n_qb = s // Q_BLOCK
    kc = _tile(s, 512)
    k_top = min(DSA_TOPK, s // 4)
    qiw = IDX_HEADS * IDX_DIM
    qbw = DSA_HEADS * DSA_DK
    kvw = DSA_KV_HEADS * (DSA_DK + DSA_DV)
    rows = DSA_HPG * Q_BLOCK
    hg = 8
    qi_w = hg * IDX_DIM
    assert (qbw + kvw) % qi_w == 0 and DSA_DK == DSA_DV == IDX_DIM
    kv0 = qbw // DSA_DK
    est = (2 * (_nbytes((Q_BLOCK, qiw + qbw), BF16) + _nbytes((s, IDX_DIM + kvw), BF16))
           + _nbytes((Q_BLOCK, s), jnp.int32) + 12 * _nbytes((8 * Q_BLOCK, kc), F32) + (4 << 20))

    def q_spec(width, col_block):
        return pl.BlockSpec((Q_BLOCK, width), lambda bi, i: (bi * n_qb + i, col_block))

    def k_spec(col_block):
        return pl.BlockSpec((s, DSA_DK), lambda bi, i: (bi, col_block))

    in_specs = [q_spec(qi_w, (qbw + kvw) // qi_w + g) for g in range(IDX_HEADS // hg)]
    in_specs += [q_spec(V7X_LANES, 0), k_spec((qbw + kvw + qiw) // IDX_DIM), q_spec(qbw, 0)]
    in_specs += [k_spec(kv0 + 2 * g) for g in range(DSA_KV_HEADS)]
    in_specs += [k_spec(kv0 + 2 * g + 1) for g in range(DSA_KV_HEADS)]
    n_p128 = IDX_HEADS // hg
    operands = [p128] * n_p128 + [small, p128, p128] + [p128] * (2 * DSA_KV_HEADS)
    return pl.pallas_call(
        functools.partial(_dsa_kernel, kc=kc, k_top=k_top),
        grid=(b, n_qb),
        in_specs=in_specs,
        out_specs=pl.BlockSpec((Q_BLOCK, DSA_HEADS * DSA_DV), lambda bi, i: (bi * n_qb + i, 0)),
        out_shape=jax.ShapeDtypeStruct((b * s, DSA_HEADS * DSA_DV), BF16),
        scratch_shapes=[
            pltpu.VMEM((s // kc, kc, Q_BLOCK), jnp.int32),
            pltpu.VMEM((DSA_KV_HEADS, rows, V7X_LANES), F32),
            pltpu.VMEM((DSA_KV_HEADS, rows, DSA_DV + V7X_LANES), F32),
        ],
        compiler_params=_params(("parallel", "arbitrary"), est),
        name="dsa_attention",
    )(*operands)


def _pad_heads(w_t, heads, dk, dkp):
    k = w_t.shape[1]
    return jnp.pad(w_t.reshape(heads, dk, k), ((0, 0), (0, dkp - dk), (0, 0))).reshape(heads * dkp, k)


def _split_kv(w_t, heads, dk, dv):
    k = w_t.shape[1]
    w3 = w_t.reshape(heads, dk + dv, k)
    return w3[:, :dk].reshape(heads * dk, k), w3[:, dk:].reshape(heads * dv, k)


def _mixer_weights(w_in, d_model):
    sizes = (
        NSA_HEADS * NSA_DK, NSA_GROUPS * (NSA_DK + NSA_DV), NSA_GROUPS * (NSA_DK + NSA_DV),
        NSA_GROUPS * (NSA_DK + NSA_DV), 3 * NSA_HEADS, DSA_HEADS * DSA_DK, DSA_KV_HEADS * (DSA_DK + DSA_DV),
        IDX_HEADS * IDX_DIM, IDX_DIM, IDX_HEADS, 2 * d_model,
    )
    offs = np.cumsum((0,) + sizes)
    assert offs[-1] == w_in.shape[1]
    w_t = jnp.swapaxes(w_in, 0, 1).astype(BF16)
    q_a, _, kv_s, kv_w, gate_a, _, _, _, _, w_i, _ = (w_t[offs[n]:offs[n + 1]] for n in range(len(sizes)))
    k_s, v_s = _split_kv(kv_s, NSA_GROUPS, NSA_DK, NSA_DV)
    k_w, v_w = _split_kv(kv_w, NSA_GROUPS, NSA_DK, NSA_DV)
    w192 = jnp.concatenate([
        _pad_heads(q_a, NSA_HEADS, NSA_DK, NSA_DKP), _pad_heads(k_s, NSA_GROUPS, NSA_DK, NSA_DKP),
        _pad_heads(k_w, NSA_GROUPS, NSA_DK, NSA_DKP)], axis=0)
    scale192 = jnp.concatenate([
        jnp.full((NSA_HEADS * NSA_DKP,), NSA_DK ** -0.5 * LOG2E, F32), jnp.ones((2 * NSA_GROUPS * NSA_DKP,), F32)])
    kv_flag = jnp.tile(jnp.concatenate([jnp.ones((DSA_DK,), F32), jnp.zeros((DSA_DV,), F32)]), DSA_KV_HEADS)
    scale128 = jnp.concatenate([
        jnp.full((DSA_HEADS * DSA_DK,), DSA_DK ** -0.5 * LOG2E, F32), jnp.ones_like(kv_flag),
        jnp.full((IDX_HEADS * IDX_DIM,), IDX_DIM ** -0.5, F32), jnp.ones((IDX_DIM,), F32)])
    flag128 = jnp.concatenate([
        jnp.ones((DSA_HEADS * DSA_DK,), F32), kv_flag, jnp.ones((IDX_HEADS * IDX_DIM + IDX_DIM,), F32)])
    wv = jnp.concatenate([v_s, v_w], axis=0)
    small_n = GATE_COLS + IDX_HEADS
    wsmall = jnp.pad(jnp.concatenate([gate_a, w_i], axis=0), ((0, V7X_LANES - small_n), (0, 0)))
    return w_t, offs, w192, scale192, scale128, flag128, wv, wsmall


def _ffn_up_kernel(h_ref, w1_ref, w3_ref, o_ref, wb_ref):
    tn = o_ref.shape[1]

    @pl.when(pl.program_id(1) == 0)
    def _():
        wb_ref[:, :tn] = w1_ref[...].astype(BF16)
        wb_ref[:, tn:] = w3_ref[...].astype(BF16)

    ab = jnp.dot(h_ref[...], wb_ref[...], preferred_element_type=F32)
    o_ref[...] = (jax.nn.silu(ab[:, :tn]) * ab[:, tn:]).astype(o_ref.dtype)


def _ffn_up(h, w1, w3, cfg):
    t, d = h.shape
    ff = w1.shape[1]
    tm = _tile(cfg.seq, 1024)
    tn = _tile(ff, 256)
    est = (2 * _nbytes((tm, d), BF16) + 4 * _nbytes((d, tn), F32) + 3 * _nbytes((d, tn), BF16)
           + 2 * _nbytes((tm, tn), BF16) + 4 * _nbytes((tm, tn), F32) + (2 << 20))
    return pl.pallas_call(
        _ffn_up_kernel,
        grid=(ff // tn, t // tm),
        in_specs=[
            pl.BlockSpec((tm, d), lambda j, i: (i, 0)),
            pl.BlockSpec((d, tn), lambda j, i: (0, j)),
            pl.BlockSpec((d, tn), lambda j, i: (0, j)),
        ],
        out_specs=pl.BlockSpec((tm, tn), lambda j, i: (i, j)),
        out_shape=jax.ShapeDtypeStruct((t, ff), BF16),
        scratch_shapes=[pltpu.VMEM((d, 2 * tn), BF16)],
        compiler_params=_params(("parallel", "arbitrary"), est),
        name="ffn_up",
    )(h, w1, w3)


def _ffn(x2, gain, mod3, idx, w1, w3, w2, cfg):
    t, d = x2.shape
    ff = w1.shape[1]
    h = _norm_mod(x2, gain, mod3, idx, idx + 1, cfg)
    u = _ffn_up(h, w1, w3, cfg)
    tm = _tile(cfg.seq, 512)
    nsb = cfg.seq // tm
    tn = _tile(d, 512)
    epi = [(x2, (tm, tn), lambda i, j: (i, j)),
           (mod3, (1, 1, tn), lambda i, j: ((i // nsb) * N_MOD + idx + 2, 0, j))]
    return _matmul([u], [w2.astype(BF16)], [(0, 0)], functools.partial(_epi_residual, coef=0.5), epi, d, F32,
                   tm=tm, tn=tn, name="ffn_down")


def _mixer(x2, gain, mod3, w_in, cmp_pe_k, cmp_w1_k, cmp_w2_k, cmp_pe_v, cmp_w1_v, cmp_w2_v,
           proj_a, proj_b, w_out, cfg):
    b, s, d = cfg.batch, cfg.seq, cfg.d_model
    w_t, offs, w192, scale192, scale128, flag128, wv, wsmall = _mixer_weights(w_in, d)
    h = _norm_mod(x2, gain, mod3, 3, 4, cfg)
    p192 = _project_rope(h, w192, 0, scale192, None, NSA_DK, NSA_DKP, cfg, "proj_rope192")
    p128 = _project_rope(h, w_t, int(offs[5]), scale128, flag128, DSA_DK, DSA_DK, cfg, "proj_rope128")
    pv = _project(h, wv, 0, wv.shape[0], _epi_plain, BF16, cfg, "proj_values")
    pc = _project(h, w_t, int(offs[1]), NSA_GROUPS * (NSA_DK + NSA_DV), _epi_plain, F32, cfg, "proj_cmp")
    small = _project(h, wsmall, 0, V7X_LANES, _epi_plain, F32, cfg, "proj_small")
    gates_m = _project(h, w_t, int(offs[10]), 2 * d, _epi_sigmoid, BF16, cfg, "proj_merge")

    n_rows = s // CMP_STRIDE
    pc5 = pc.reshape(b, n_rows, CMP_STRIDE, NSA_GROUPS, NSA_DK + NSA_DV)

    def chunks(x5):
        return jnp.transpose(x5, (0, 3, 1, 2, 4)).reshape(b * NSA_GROUPS, n_rows, CMP_STRIDE * x5.shape[-1])

    cmp_pos = jnp.arange(n_rows) * CMP_STRIDE + CMP_BLOCK - 1
    k_cmp = _compress(chunks(pc5[..., :NSA_DK]), cmp_pe_k, cmp_w1_k, cmp_w2_k, NSA_DK, NSA_DKP, cmp_pos,
                      "compress_k").reshape(b, NSA_GROUPS, n_rows, NSA_DKP)
    v_cmp = _compress(chunks(pc5[..., NSA_DK:]), cmp_pe_v, cmp_w1_v, cmp_w2_v, NSA_DV, NSA_DV, None,
                      "compress_v").reshape(b, NSA_GROUPS, n_rows, NSA_DV)

    o_a = _nsa_attention(p192, pv, small, k_cmp, v_cmp, cfg)
    o_b = _dsa_attention(p128, small, cfg)

    tm = _tile(s, 1024)
    tn = _tile(d, 1024)
    nsb = s // tm
    epi = [(gates_m, (tm, tn), lambda i, j: (i, j)), (gates_m, (tm, tn), lambda i, j: (i, j + d // tn))]
    y = _matmul([o_a, o_b], [proj_a.astype(BF16), proj_b.astype(BF16)], [(0, 0), (1, 1)], _epi_merge, epi,
                d, BF16, tm=tm, tn=tn, name="merge_proj")
    epi = [(x2, (tm, tn), lambda i, j: (i, j)),
           (mod3, (1, 1, tn), lambda i, j: ((i // nsb) * N_MOD + 5, 0, j))]
    return _matmul([y], [w_out.astype(BF16)], [(0, 0)], functools.partial(_epi_residual, coef=1.0), epi,
                   d, F32, tm=tm, tn=tn, name="out_proj")


def _layer(x, c, ada_w, ada_b, norm_ffn1, ffn1_w1, ffn1_w3, ffn1_w2, norm_mix, w_in,
           cmp_pe_k, cmp_w1_k, cmp_w2_k, cmp_pe_v, cmp_w1_v, cmp_w2_v, proj_a, proj_b, w_out,
           norm_ffn2, ffn2_w1, ffn2_w3, ffn2_w2, norm_final):
    b, s, d = x.shape
    cfg = Cfg(b, s, d, ffn1_w1.shape[-1])
    assert s % Q_BLOCK == 0 and s >= WINDOW + Q_BLOCK and d % V7X_LANES == 0
    x2 = x.reshape(b * s, d)
    c_pad = jnp.pad(c, ((0, 8 - b), (0, 0)))
    for l in range(ada_w.shape[0]):
        mod = _adaln(c_pad, ada_w[l], ada_b[l].reshape(1, -1))
        mod3 = mod[:b].reshape(b * N_MOD, 1, d)
        x2 = _ffn(x2, norm_ffn1[l], mod3, 0, ffn1_w1[l], ffn1_w3[l], ffn1_w2[l], cfg)
        x2 = _mixer(x2, norm_mix[l], mod3, w_in[l], cmp_pe_k[l], cmp_w1_k[l], cmp_w2_k[l],
                    cmp_pe_v[l], cmp_w1_v[l], cmp_w2_v[l], proj_a[l], proj_b[l], w_out[l], cfg)
        x2 = _ffn(x2, norm_ffn2[l], mod3, 6, ffn2_w1[l], ffn2_w3[l], ffn2_w2[l], cfg)
    return _final_norm(x2, norm_final).reshape(b, s, d)


def kernel(x, c, ada_w, ada_b, norm_ffn1, ffn1_w1, ffn1_w3, ffn1_w2, norm_mix, w_in, cmp_pe_k, cmp_w1_k, cmp_w2_k, cmp_pe_v, cmp_w1_v, cmp_w2_v, proj_a, proj_b, w_out, norm_ffn2, ffn2_w1, ffn2_w3, ffn2_w2, norm_final):
    return _layer(x, c, ada_w, ada_b, norm_ffn1, ffn1_w1, ffn1_w3, ffn1_w2, norm_mix, w_in,
                  cmp_pe_k, cmp_w1_k, cmp_w2_k, cmp_pe_v, cmp_w1_v, cmp_w2_v, proj_a, proj_b, w_out,
                  norm_ffn2, ffn2_w1, ffn2_w3, ffn2_w2, norm_final)
```

```python
import functools
import math
from typing import NamedTuple

import numpy as np
import jax
import jax.numpy as jnp
from jax import lax
from jax.experimental import pallas as pl
from jax.experimental.pallas import tpu as pltpu

F32 = jnp.float32
BF16 = jnp.bfloat16

V7X_LANES = 128
V7X_VMEM_BYTES = 64 * 1024 * 1024
V7X_VMEM_BUDGET = 56 * 1024 * 1024

EPS = 1e-6
ROPE_THETA = 500000.0
ROPE_FRACTION = 4
N_MOD = 9
Q_BLOCK = 128

NSA_HEADS = 16
NSA_GROUPS = 2
NSA_HPG = NSA_HEADS // NSA_GROUPS
NSA_DK = 192
NSA_DKP = 256
NSA_DV = 128
CMP_BLOCK = 32
CMP_STRIDE = 16
SEL_BLOCK = 64
SEL_TOP = 16
WINDOW = 512

DSA_HEADS = 16
DSA_KV_HEADS = 4
DSA_HPG = DSA_HEADS // DSA_KV_HEADS
DSA_DK = 128
DSA_DV = 128
IDX_HEADS = 32
IDX_DIM = 128
DSA_TOPK = 256

GATE_COLS = 3 * NSA_HEADS
INT_MIN = np.int32(-(2**31))
NEG_INF = float("-inf")
LOG2E = math.log2(math.e)
TINY = float(np.finfo(np.float32).tiny)
DN_T = (((1,), (1,)), ((), ()))


class Cfg(NamedTuple):
    batch: int
    seq: int
    d_model: int
    d_ff: int


def _tile(n, pref):
    if n <= pref:
        return n
    t = (pref // V7X_LANES) * V7X_LANES
    while t > V7X_LANES and n % t:
        t -= V7X_LANES
    assert n % t == 0, (n, pref)
    return t


def _nbytes(shape, dtype):
    return int(np.prod(shape)) * jnp.dtype(dtype).itemsize


def _params(dims, est_bytes):
    limit = int(min(V7X_VMEM_BUDGET, max(est_bytes, 16 * 1024 * 1024)))
    return pltpu.CompilerParams(dimension_semantics=dims, vmem_limit_bytes=limit)


def _adaln_kernel(c_ref, w_ref, b_ref, o_ref, acc_ref, *, nk):
    k = pl.program_id(1)

    @pl.when(k == 0)
    def _():
        acc_ref[...] = jnp.zeros_like(acc_ref)

    c = c_ref[...]
    c_act = (c * jax.nn.sigmoid(c)).astype(BF16)
    acc_ref[...] += jnp.dot(c_act, w_ref[...].astype(BF16), preferred_element_type=F32)

    @pl.when(k == nk - 1)
    def _():
        o_ref[...] = acc_ref[...] + b_ref[...]


def _adaln(c_pad, w, b):
    m, kdim = c_pad.shape
    n = w.shape[1]
    tn, tk = _tile(n, 2048), _tile(kdim, 1024)
    nk = kdim // tk
    est = 2 * _nbytes((tk, tn), F32) + _nbytes((tk, tn), BF16) + 4 * _nbytes((m, tn), F32) + (2 << 20)
    return pl.pallas_call(
        functools.partial(_adaln_kernel, nk=nk),
        grid=(n // tn, nk),
        in_specs=[
            pl.BlockSpec((m, tk), lambda j, k: (0, k)),
            pl.BlockSpec((tk, tn), lambda j, k: (k, j)),
            pl.BlockSpec((1, tn), lambda j, k: (0, j)),
        ],
        out_specs=pl.BlockSpec((m, tn), lambda j, k: (0, j)),
        out_shape=jax.ShapeDtypeStruct((m, n), F32),
        scratch_shapes=[pltpu.VMEM((m, tn), F32)],
        compiler_params=_params(("parallel", "arbitrary"), est),
        name="adaln",
    )(c_pad, w, b)


def _norm_mod_kernel(x_ref, g_ref, sh_ref, sc_ref, o_ref):
    x = x_ref[...]
    y = x * lax.rsqrt(jnp.mean(x * x, axis=-1, keepdims=True) + EPS)
    y = y * g_ref[...]
    o_ref[...] = (y * (1.0 + sc_ref[0]) + sh_ref[0]).astype(o_ref.dtype)


def _norm_mod(x2, gain, mod3, shift_idx, scale_idx, cfg):
    t, d = x2.shape
    tm = _tile(cfg.seq, 512)
    nsb = cfg.seq // tm
    est = 4 * _nbytes((tm, d), F32) + 2 * _nbytes((tm, d), BF16) + (2 << 20)
    return pl.pallas_call(
        _norm_mod_kernel,
        grid=(t // tm,),
        in_specs=[
            pl.BlockSpec((tm, d), lambda i: (i, 0)),
            pl.BlockSpec((1, d), lambda i: (0, 0)),
            pl.BlockSpec((1, 1, d), lambda i: ((i // nsb) * N_MOD + shift_idx, 0, 0)),
            pl.BlockSpec((1, 1, d), lambda i: ((i // nsb) * N_MOD + scale_idx, 0, 0)),
        ],
        out_specs=pl.BlockSpec((tm, d), lambda i: (i, 0)),
        out_shape=jax.ShapeDtypeStruct((t, d), BF16),
        compiler_params=_params(("parallel",), est),
        name="norm_mod",
    )(x2, gain.reshape(1, d), mod3, mod3)


def _final_norm_kernel(x_ref, g_ref, o_ref):
    x = x_ref[...]
    y = x * lax.rsqrt(jnp.mean(x * x, axis=-1, keepdims=True) + EPS)
    o_ref[...] = y * g_ref[...]


def _final_norm(x2, gain):
    t, d = x2.shape
    tm = _tile(t, 512)
    est = 6 * _nbytes((tm, d), F32) + (2 << 20)
    return pl.pallas_call(
        _final_norm_kernel,
        grid=(t // tm,),
        in_specs=[pl.BlockSpec((tm, d), lambda i: (i, 0)), pl.BlockSpec((1, d), lambda i: (0, 0))],
        out_specs=pl.BlockSpec((tm, d), lambda i: (i, 0)),
        out_shape=jax.ShapeDtypeStruct((t, d), F32),
        compiler_params=_params(("parallel",), est),
        name="final_norm",
    )(x2, gain.reshape(1, d))


MM_SUB = 256


def _mm_kernel(*refs, n_a, n_w, pairs, n_epi, epilogue, sub, w_rows):
    a_refs = refs[:n_a]
    w_refs = refs[n_a:n_a + n_w]
    e_refs = refs[n_a + n_w:n_a + n_w + n_epi]
    o_ref = refs[n_a + n_w + n_epi]
    for c in range(o_ref.shape[1] // sub):
        cols = slice(c * sub, (c + 1) * sub)
        if w_rows:
            accs = [lax.dot_general(a_refs[ai][...], w_refs[wi][cols, :], DN_T, preferred_element_type=F32)
                    for ai, wi in pairs]
        else:
            accs = [jnp.dot(a_refs[ai][...], w_refs[wi][:, cols], preferred_element_type=F32)
                    for ai, wi in pairs]
        o_ref[:, cols] = epilogue(accs, e_refs, cols).astype(o_ref.dtype)


def _matmul(a_list, w_list, pairs, epilogue, epi_inputs, n_out, out_dtype, *, tm, tn, name, w_rows=False,
            w_row0=0):
    m = a_list[0].shape[0]
    assert m % tm == 0
    sub = MM_SUB if tn % MM_SUB == 0 else tn
    in_specs, est = [], 0
    for a in a_list:
        in_specs.append(pl.BlockSpec((tm, a.shape[1]), lambda i, j: (i, 0)))
        est += 2 * _nbytes((tm, a.shape[1]), a.dtype)
    for w in w_list:
        if w_rows:
            kdim = w.shape[1]
            row_tile = 8 * 4 // jnp.dtype(w.dtype).itemsize
            assert w_row0 % row_tile == 0 and tn % row_tile == 0
            in_specs.append(pl.BlockSpec(
                (pl.Element(tn), pl.Element(kdim)),
                lambda i, j, row_tile=row_tile: (pl.multiple_of(w_row0 + j * tn, row_tile), 0)))
        else:
            kdim = w.shape[0]
            in_specs.append(pl.BlockSpec((kdim, tn), lambda i, j: (0, j)))
        est += 2 * _nbytes((kdim, tn), w.dtype)
    for arr, bshape, imap in epi_inputs:
        in_specs.append(pl.BlockSpec(bshape, imap))
        est += 2 * _nbytes(bshape, arr.dtype)
    est += 2 * _nbytes((tm, tn), out_dtype) + 4 * (len(pairs) + 3) * _nbytes((tm, sub), F32) + (2 << 20)
    kern = functools.partial(_mm_kernel, n_a=len(a_list), n_w=len(w_list), pairs=tuple(pairs),
                             n_epi=len(epi_inputs), epilogue=epilogue, sub=sub, w_rows=w_rows)
    return pl.pallas_call(
        kern,
        grid=(m // tm, pl.cdiv(n_out, tn)),
        in_specs=in_specs,
        out_specs=pl.BlockSpec((tm, tn), lambda i, j: (i, j)),
        out_shape=jax.ShapeDtypeStruct((m, n_out), out_dtype),
        compiler_params=_params(("parallel", "parallel"), est),
        name=name,
    )(*a_list, *w_list, *[e[0] for e in epi_inputs])


def _epi_plain(accs, e_refs, cols):
    return accs[0]


def _epi_sigmoid(accs, e_refs, cols):
    return jax.nn.sigmoid(accs[0])


def _epi_merge(accs, e_refs, cols):
    return e_refs[0][:, cols].astype(F32) * accs[0] + e_refs[1][:, cols].astype(F32) * accs[1]


def _epi_residual(accs, e_refs, cols, *, coef):
    return e_refs[0][:, cols] + (coef * e_refs[1][0][:, cols]) * accs[0]


def _rope_apply(a, cos, sin_lo, sin_hi, half):
    width = a.shape[-1]
    reps = (1, width // cos.shape[-1])
    return (a * jnp.tile(cos, reps) + pltpu.roll(a, width - half, 1) * jnp.tile(sin_lo, reps)
            + pltpu.roll(a, half, 1) * jnp.tile(sin_hi, reps))


def _epi_rope(accs, e_refs, cols, *, half):
    cos, sin_lo, sin_hi = e_refs[0][...], e_refs[1][...], e_refs[2][...]
    if len(e_refs) > 4:
        flag = e_refs[4][:, cols]
        reps = (1, flag.shape[-1] // cos.shape[-1])
        cos = 1.0 + flag * (jnp.tile(cos, reps) - 1.0)
        sin_lo, sin_hi = flag * jnp.tile(sin_lo, reps), flag * jnp.tile(sin_hi, reps)
    return _rope_apply(accs[0], cos, sin_lo, sin_hi, half) * e_refs[3][:, cols]


def _rope_tables(pos, dk, head_w):
    r = dk // ROPE_FRACTION
    half = r // 2
    n = pos.shape[0]
    inv = ROPE_THETA ** (-jnp.arange(0, r, 2, dtype=F32) / r)
    ang = pos.astype(F32)[:, None] * inv[None, :]
    cos, sin = jnp.cos(ang), jnp.sin(ang)
    cos_t = jnp.concatenate([cos, cos, jnp.ones((n, head_w - r), F32)], axis=1)
    sin_lo = jnp.concatenate([-sin, jnp.zeros((n, head_w - half), F32)], axis=1)
    sin_hi = jnp.concatenate([jnp.zeros((n, half), F32), sin, jnp.zeros((n, head_w - r), F32)], axis=1)
    return cos_t, sin_lo, sin_hi, half


def _project_rope(h, w_t, row0, colscale, flag, dk, head_w, cfg, name):
    n = colscale.shape[0]
    tm = _tile(cfg.seq, 1024)
    tn = min(n, 1024)
    nsb = cfg.seq // tm
    cos_t, sin_lo, sin_hi, half = _rope_tables(jnp.arange(cfg.seq), dk, head_w)
    epi = [(tb, (tm, head_w), lambda i, j: (i % nsb, 0)) for tb in (cos_t, sin_lo, sin_hi)]
    epi.append((colscale.reshape(1, n), (1, tn), lambda i, j: (0, j)))
    if flag is not None:
        epi.append((flag.reshape(1, n), (1, tn), lambda i, j: (0, j)))
    return _matmul([h], [w_t], [(0, 0)], functools.partial(_epi_rope, half=half), epi, n, BF16,
                   tm=tm, tn=tn, name=name, w_rows=True, w_row0=row0)


def _project(h, w_t, row0, n, epilogue, out_dtype, cfg, name):
    return _matmul([h], [w_t], [(0, 0)], epilogue, [], n, out_dtype,
                   tm=_tile(cfg.seq, 1024), tn=_tile(n, 1024), name=name, w_rows=True, w_row0=row0)


def _gelu_tanh(x):
    return 0.5 * x * (1.0 + jnp.tanh(math.sqrt(2.0 / math.pi) * (x + 0.044715 * (x * x * x))))


def _compress_kernel(x_ref, pe_ref, w1_ref, w2_ref, *rest, half):
    o_ref = rest[-1]
    x = x_ref[0]
    n_rows, width = x.shape
    a_lo = (x + pe_ref[0:1, :]).astype(BF16)
    a_hi = (x + pe_ref[1:2, :]).astype(BF16)
    p = jnp.dot(a_lo, w1_ref[0:width, :], preferred_element_type=F32)
    q = jnp.dot(a_hi, w1_ref[width:2 * width, :], preferred_element_type=F32)
    hid = _gelu_tanh(p + pltpu.roll(q, n_rows - 1, 0))
    y = jnp.dot(hid.astype(BF16), w2_ref[...], preferred_element_type=F32)
    if half:
        y = _rope_apply(y, rest[0][...], rest[1][...], rest[2][...], half)
    o_ref[0] = y.astype(o_ref.dtype)


def _compress(x_chunks, pe, w1, w2, d, d_out, rope_pos, name):
    bg, n_rows, width = x_chunks.shape
    hid_w = ((d + V7X_LANES - 1) // V7X_LANES) * V7X_LANES
    pe2 = pe.reshape(2, width)
    w1p = jnp.pad(w1.astype(BF16), ((0, 0), (0, hid_w - d)))
    w2p = jnp.pad(w2.astype(BF16), ((0, hid_w - d), (0, d_out - d)))
    ins = [x_chunks, pe2, w1p, w2p]
    in_specs = [
        pl.BlockSpec((1, n_rows, width), lambda i: (i, 0, 0)),
        pl.BlockSpec((2, width), lambda i: (0, 0)),
        pl.BlockSpec((2 * width, hid_w), lambda i: (0, 0)),
        pl.BlockSpec((hid_w, d_out), lambda i: (0, 0)),
    ]
    half = 0
    if rope_pos is not None:
        cos_t, sin_lo, sin_hi, half = _rope_tables(rope_pos, d, d_out)
        ins += [cos_t, sin_lo, sin_hi]
        in_specs += [pl.BlockSpec((n_rows, d_out), lambda i: (0, 0))] * 3
    est = (2 * _nbytes((n_rows, width), F32) + 2 * _nbytes((n_rows, width), BF16)
           + 2 * _nbytes((2 * width, hid_w), BF16) + (8 << 20))
    return pl.pallas_call(
        functools.partial(_compress_kernel, half=half),
        grid=(bg,),
        in_specs=in_specs,
        out_specs=pl.BlockSpec((1, n_rows, d_out), lambda i: (i, 0, 0)),
        out_shape=jax.ShapeDtypeStruct((bg, n_rows, d_out), BF16),
        compiler_params=_params(("parallel",), est),
        name=name,
    )(*ins)


def _stack_heads(ref, first, count, width):
    return jnp.concatenate([ref[:, (first + h) * width:(first + h + 1) * width] for h in range(count)], axis=0)


def _mask_bias(mask):
    return jnp.where(mask, 0.0, NEG_INF)


def _with_ones(v):
    return jnp.concatenate([v, jnp.ones((v.shape[0], V7X_LANES), v.dtype)], axis=1)


def _exp2_masked(s3, bias2):
    z = s3 + bias2[None]
    m = jnp.max(z, axis=-1, keepdims=True)
    m = jnp.where(m > NEG_INF, m, 0.0)
    return jnp.exp2(z - m)


def _normalize(pv, dv):
    return pv[:, :dv] / jnp.maximum(pv[:, dv:dv + V7X_LANES], TINY)


def _flash_init(m_ref, acc_ref):
    m_ref[...] = jnp.full(m_ref.shape, NEG_INF, F32)
    acc_ref[...] = jnp.zeros(acc_ref.shape, F32)


def _flash_step(s, bias2, v_ones, heads, m_ref, acc_ref):
    rows, kc = s.shape
    z = (s.reshape(heads, rows // heads, kc) + bias2[None]).reshape(rows, kc)
    m_old = m_ref[...]
    m_new = jnp.maximum(m_old, jnp.max(z, axis=-1, keepdims=True).astype(F32))
    m_safe = jnp.where(m_new > NEG_INF, m_new, 0.0)
    alpha = jnp.exp2(m_old - m_safe)
    p = jnp.exp2(z - jnp.tile(m_safe.astype(z.dtype), (1, kc // V7X_LANES)))
    pv = jnp.dot(p.astype(BF16), v_ones, preferred_element_type=F32)
    acc_ref[...] = jnp.tile(alpha, (1, acc_ref.shape[-1] // V7X_LANES)) * acc_ref[...] + pv
    m_ref[...] = m_new


def _chunk_loop(n_chunks, process):
    def pair_body(j, carry):
        process([2 * j, 2 * j + 1])
        return carry

    lax.fori_loop(0, n_chunks // 2, pair_body, 0)

    @pl.when(n_chunks % 2 == 1)
    def _():
        process([n_chunks - 1])


def _nsa_kernel(q_ref, gate_ref, kc_ref, vc_ref, ks_ref, vs_ref, kw_ref, vw_ref, aggt_ref, o_ref,
                m_ref, acc_ref, *, seq, kc, blk_pad):
    n_cmp = (seq - CMP_BLOCK) // CMP_STRIDE + 1
    n_blk = seq // SEL_BLOCK
    n_sel = min(SEL_TOP, n_blk)
    cmp_pad = aggt_ref.shape[1]
    q0 = pl.program_id(1) * Q_BLOCK
    t_col = q0 + lax.broadcasted_iota(jnp.int32, (Q_BLOCK, 1), 0)
    t_row = q0 + lax.broadcasted_iota(jnp.int32, (1, Q_BLOCK), 1)
    gates = jax.nn.sigmoid(gate_ref[...])

    cidx = lax.broadcasted_iota(jnp.int32, (Q_BLOCK, cmp_pad), 1)
    bias_c = _mask_bias((cidx * CMP_STRIDE + (CMP_BLOCK - 1) <= t_col) & (cidx < n_cmp))

    blk = lax.broadcasted_iota(jnp.int32, (n_blk, Q_BLOCK), 0)
    cur = lax.shift_right_logical(t_row, int(math.log2(SEL_BLOCK)))
    forced = (blk == 0) | (blk == cur) | (blk == cur - 1)
    valid = blk * SEL_BLOCK <= t_row

    win_k = WINDOW + Q_BLOCK
    ws = pl.multiple_of(jnp.maximum(q0 - WINDOW, 0), Q_BLOCK)
    diff = t_col - (ws + lax.broadcasted_iota(jnp.int32, (1, win_k), 1))
    bias_w = _mask_bias((diff >= 0) & (diff < WINDOW)).astype(BF16)
    n_chunks = (q0 + Q_BLOCK + kc - 1) // kc
    rows = NSA_HPG * Q_BLOCK

    q_groups = [_stack_heads(q_ref, g * NSA_HPG, NSA_HPG, NSA_DKP) for g in range(NSA_GROUPS)]
    o_cmp, sels = [], []
    for g in range(NSA_GROUPS):
        qg = q_groups[g]

        s = lax.dot_general(qg, kc_ref[0, g], DN_T, preferred_element_type=F32)
        e_c = _exp2_masked(s.reshape(NSA_HPG, Q_BLOCK, cmp_pad), bias_c)
        p_c = e_c / jnp.maximum(jnp.sum(e_c, axis=-1, keepdims=True), TINY)
        o_c = jnp.dot(p_c.reshape(rows, cmp_pad).astype(BF16), vc_ref[0, g], preferred_element_type=F32)

        p_sum = jnp.sum(p_c, axis=0)
        hi = p_sum.astype(BF16)
        r1 = p_sum - hi.astype(F32)
        mid = r1.astype(BF16)
        lo = (r1 - mid.astype(F32)).astype(BF16)
        aggt = aggt_ref[...]
        imp = (lax.dot_general(aggt, hi, DN_T, preferred_element_type=F32)
               + lax.dot_general(aggt, mid, DN_T, preferred_element_type=F32)
               + lax.dot_general(aggt, lo, DN_T, preferred_element_type=F32))
        val = jnp.where(forced, jnp.inf, jnp.where(valid, imp, NEG_INF))
        rank = jnp.zeros((n_blk, Q_BLOCK), F32)
        for ii in range(n_blk):
            row = val[ii:ii + 1, :]
            beats = (row > val) | ((row == val) & (blk > ii))
            rank = rank + jnp.where(beats, 1.0, 0.0)
        sel_t = jnp.where((rank < n_sel) & (val > NEG_INF), 1.0, 0.0)
        if blk_pad > n_blk:
            sel_t = jnp.concatenate([sel_t, jnp.zeros((blk_pad - n_blk, Q_BLOCK), F32)], axis=0)
        sels.append(jnp.transpose(sel_t).astype(BF16))
        o_cmp.append(o_c)

    _flash_init(m_ref, acc_ref)

    def sel_chunks(chunks):
        scores = []
        for c in chunks:
            k0 = pl.multiple_of(c * kc, kc)
            scores.append([
                lax.dot_general(q_groups[g], ks_ref[pl.ds(k0, kc), g * NSA_DKP:(g + 1) * NSA_DKP], DN_T,
                                preferred_element_type=F32).astype(BF16) for g in range(NSA_GROUPS)])
        for c, s_c in zip(chunks, scores):
            k0 = pl.multiple_of(c * kc, kc)
            kpos_b = k0 + lax.broadcasted_iota(jnp.int32, (blk_pad, kc), 1)
            expand = jnp.where(
                lax.shift_right_logical(kpos_b, int(math.log2(SEL_BLOCK)))
                == lax.broadcasted_iota(jnp.int32, (blk_pad, kc), 0), 1.0, 0.0).astype(BF16)
            causal = k0 + lax.broadcasted_iota(jnp.int32, (1, kc), 1) <= t_col
            for g in range(NSA_GROUPS):
                v_blk = vs_ref[pl.ds(k0, kc), g * NSA_DV:(g + 1) * NSA_DV]
                sel_keys = jnp.dot(sels[g], expand, preferred_element_type=F32)
                bias = _mask_bias((sel_keys > 0.5) & causal).astype(BF16)
                _flash_step(s_c[g], bias, _with_ones(v_blk), NSA_HPG, m_ref.at[g], acc_ref.at[g])

    _chunk_loop(n_chunks, sel_chunks)

    for g in range(NSA_GROUPS):
        qg, o_c = q_groups[g], o_cmp[g]
        o_s = _normalize(acc_ref[g], NSA_DV)

        k_win = kw_ref[pl.ds(ws, win_k), g * NSA_DKP:(g + 1) * NSA_DKP]
        v_win = vw_ref[pl.ds(ws, win_k), g * NSA_DV:(g + 1) * NSA_DV]
        s_w = lax.dot_general(qg, k_win, DN_T, preferred_element_type=F32).astype(BF16)
        e_w = _exp2_masked(s_w.reshape(NSA_HPG, Q_BLOCK, win_k), bias_w)
        o_w = _normalize(jnp.dot(e_w.reshape(rows, win_k).astype(BF16), _with_ones(v_win),
                                 preferred_element_type=F32), NSA_DV)

        for h in range(NSA_HPG):
            c0 = (g * NSA_HPG + h) * 3
            hr = slice(h * Q_BLOCK, (h + 1) * Q_BLOCK)
            o = (gates[:, c0:c0 + 1] * o_c[hr] + gates[:, c0 + 1:c0 + 2] * o_s[hr]
                 + gates[:, c0 + 2:c0 + 3] * o_w[hr])
            head = g * NSA_HPG + h
            o_ref[:, head * NSA_DV:(head + 1) * NSA_DV] = o.astype(o_ref.dtype)


def _agg_t(seq, cmp_pad):
    n_cmp = (seq - CMP_BLOCK) // CMP_STRIDE + 1
    n_blk = seq // SEL_BLOCK
    r = SEL_BLOCK // CMP_STRIDE
    c = CMP_BLOCK // CMP_STRIDE
    j = np.arange(n_blk)[:, None, None, None]
    i = np.arange(cmp_pad)[None, :, None, None]
    m = np.arange(r)[None, None, :, None]
    n = np.arange(c)[None, None, None, :]
    a = np.sum(i == r * j + m - n, axis=(2, 3)).astype(np.float32)
    a = a * (np.arange(cmp_pad)[None, :] < n_cmp)
    return jnp.asarray(a, dtype=BF16)


def _nsa_attention(p192, pv, small, k_cmp, v_cmp, cfg):
    b, s = cfg.batch, cfg.seq
    n_qb = s // Q_BLOCK
    kc = _tile(s, 512)
    cmp_pad = k_cmp.shape[2]
    blk_pad = max(V7X_LANES, s // SEL_BLOCK)
    qw = NSA_HEADS * NSA_DKP
    kw = NSA_GROUPS * NSA_DKP
    vw = NSA_GROUPS * NSA_DV
    rows = NSA_HPG * Q_BLOCK
    est = (2 * (_nbytes((Q_BLOCK, qw), BF16) + 2 * _nbytes((s, kw), BF16) + 2 * _nbytes((s, vw), BF16))
           + 3 * _nbytes((rows, V7X_LANES), F32) + 10 * _nbytes((rows, WINDOW + Q_BLOCK), F32) + (4 << 20))
    return pl.pallas_call(
        functools.partial(_nsa_kernel, seq=s, kc=kc, blk_pad=blk_pad),
        grid=(b, n_qb),
        in_specs=[
            pl.BlockSpec((Q_BLOCK, qw), lambda bi, i: (bi * n_qb + i, 0)),
            pl.BlockSpec((Q_BLOCK, V7X_LANES), lambda bi, i: (bi * n_qb + i, 0)),
            pl.BlockSpec((1, NSA_GROUPS, cmp_pad, NSA_DKP), lambda bi, i: (bi, 0, 0, 0)),
            pl.BlockSpec((1, NSA_GROUPS, cmp_pad, NSA_DV), lambda bi, i: (bi, 0, 0, 0)),
            pl.BlockSpec((s, kw), lambda bi, i: (bi, qw // kw)),
            pl.BlockSpec((s, vw), lambda bi, i: (bi, 0)),
            pl.BlockSpec((s, kw), lambda bi, i: (bi, qw // kw + 1)),
            pl.BlockSpec((s, vw), lambda bi, i: (bi, 1)),
            pl.BlockSpec((s // SEL_BLOCK, cmp_pad), lambda bi, i: (0, 0)),
        ],
        out_specs=pl.BlockSpec((Q_BLOCK, NSA_HEADS * NSA_DV), lambda bi, i: (bi * n_qb + i, 0)),
        out_shape=jax.ShapeDtypeStruct((b * s, NSA_HEADS * NSA_DV), BF16),
        scratch_shapes=[pltpu.VMEM((NSA_GROUPS, rows, V7X_LANES), F32),
                        pltpu.VMEM((NSA_GROUPS, rows, NSA_DV + V7X_LANES), F32)],
        compiler_params=_params(("parallel", "arbitrary"), est),
        name="nsa_attention",
    )(p192, small, k_cmp, v_cmp, p192, pv, p192, pv, _agg_t(s, cmp_pad))


def _dsa_kernel(*refs, kc, k_top):
    hg = 8
    n_qi = IDX_HEADS // hg
    qi_refs = refs[:n_qi]
    w_ref, ki_ref, qb_ref = refs[n_qi:n_qi + 3]
    kb_refs = refs[n_qi + 3:n_qi + 3 + DSA_KV_HEADS]
    vb_refs = refs[n_qi + 3 + DSA_KV_HEADS:n_qi + 3 + 2 * DSA_KV_HEADS]
    o_ref, key_ref, m_ref, acc_ref = refs[n_qi + 3 + 2 * DSA_KV_HEADS:]
    q0 = pl.program_id(1) * Q_BLOCK
    t_row = q0 + lax.broadcasted_iota(jnp.int32, (1, Q_BLOCK), 1)
    n_chunks = (q0 + Q_BLOCK + kc - 1) // kc
    w_t = jnp.transpose(w_ref[...] * (IDX_HEADS ** -0.5))
    cnt_rows = 64

    def causal_t(c):
        return c * kc + lax.broadcasted_iota(jnp.int32, (kc, 1), 0) <= t_row

    def score_body(c, carry):
        k0 = pl.multiple_of(c * kc, kc)
        k_i = ki_ref[pl.ds(k0, kc), :]
        acc = jnp.zeros((kc, Q_BLOCK), F32)
        for g in range(IDX_HEADS // hg):
            q_h = _stack_heads(qi_refs[g], 0, hg, IDX_DIM)
            rel = jnp.maximum(lax.dot_general(k_i, q_h, DN_T, preferred_element_type=F32), 0.0)
            for h in range(hg):
                col = GATE_COLS + g * hg + h
                acc = acc + rel[:, h * Q_BLOCK:(h + 1) * Q_BLOCK] * w_t[col:col + 1, :]
        bits = lax.bitcast_convert_type(acc, jnp.int32)
        skey = bits ^ (lax.shift_right_arithmetic(bits, 31) & np.int32(0x7FFFFFFF))
        key_ref[c] = jnp.where(causal_t(c), skey, INT_MIN)
        return carry

    lax.fori_loop(0, n_chunks, score_body, 0)

    def select_body(it, thr_u):
        bit = lax.shift_left(jnp.int32(1), 31 - it)
        cand = thr_u | bit
        cand_key = cand ^ INT_MIN

        def count_body(c, cnt):
            hit = jnp.where(key_ref[c] >= cand_key, 1.0, 0.0)
            return cnt + jnp.sum(hit.reshape(kc // cnt_rows, cnt_rows, Q_BLOCK), axis=0)

        cnt = lax.fori_loop(0, n_chunks, count_body, jnp.zeros((cnt_rows, Q_BLOCK), F32))
        return jnp.where(jnp.sum(cnt, axis=0, keepdims=True) >= k_top, cand, thr_u)

    thr_u = lax.fori_loop(0, 32, select_body, jnp.zeros((1, Q_BLOCK), jnp.int32))
    thr = thr_u ^ INT_MIN

    _flash_init(m_ref, acc_ref)
    q_groups = [_stack_heads(qb_ref, g * DSA_HPG, DSA_HPG, DSA_DK) for g in range(DSA_KV_HEADS)]

    def attn_chunks(chunks):
        scores = []
        for c in chunks:
            k0 = pl.multiple_of(c * kc, kc)
            scores.append([
                lax.dot_general(q_groups[g], kb_refs[g][pl.ds(k0, kc), :], DN_T,
                                preferred_element_type=F32).astype(BF16) for g in range(DSA_KV_HEADS)])
        for c, s_c in zip(chunks, scores):
            k0 = pl.multiple_of(c * kc, kc)
            bias = jnp.transpose(_mask_bias((key_ref[c] >= thr) & causal_t(c))).astype(BF16)
            for g in range(DSA_KV_HEADS):
                v_blk = vb_refs[g][pl.ds(k0, kc), :]
                _flash_step(s_c[g], bias, _with_ones(v_blk), DSA_HPG, m_ref.at[g], acc_ref.at[g])

    _chunk_loop(n_chunks, attn_chunks)
    for g in range(DSA_KV_HEADS):
        o = _normalize(acc_ref[g], DSA_DV)
        for h in range(DSA_HPG):
            head = g * DSA_HPG + h
            o_ref[:, head * DSA_DV:(head + 1) * DSA_DV] = o[h * Q_BLOCK:(h + 1) * Q_BLOCK].astype(o_ref.dtype)


def _dsa_attention(p128, k_idx, small, cfg):
    b, s = cfg.batch, cfg.seq
    n_qb = s // Q_BLOCK
    kc = _tile(s, 512)
    k_top = min(DSA_TOPK, s // 4)
    qiw = IDX_HEADS * IDX_DIM
    qbw = DSA_HEADS * DSA_DK
    kvw = DSA_KV_HEADS * (DSA_DK + DSA_DV)
    rows = DSA_HPG * Q_BLOCK
    hg = 8
    qi_w = hg * IDX_DIM
    assert (qbw + kvw) % qi_w == 0 and DSA_DK == DSA_DV == IDX_DIM
    kv0 = qbw // DSA_DK
    est = (2 * (_nbytes((Q_BLOCK, qiw + qbw), BF16) + _nbytes((s, IDX_DIM + kvw), BF16))
           + _nbytes((Q_BLOCK, s), jnp.int32) + 12 * _nbytes((8 * Q_BLOCK, kc), F32) + (4 << 20))

    def q_spec(width, col_block):
        return pl.BlockSpec((Q_BLOCK, width), lambda bi, i: (bi * n_qb + i, col_block))

    def k_spec(col_block):
        return pl.BlockSpec((s, DSA_DK), lambda bi, i: (bi, col_block))

    in_specs = [q_spec(qi_w, (qbw + kvw) // qi_w + g) for g in range(IDX_HEADS // hg)]
    in_specs += [q_spec(V7X_LANES, 0), k_spec(0), q_spec(qbw, 0)]
    in_specs += [k_spec(kv0 + 2 * g) for g in range(DSA_KV_HEADS)]
    in_specs += [k_spec(kv0 + 2 * g + 1) for g in range(DSA_KV_HEADS)]
    n_p128 = IDX_HEADS // hg
    operands = [p128] * n_p128 + [small, k_idx, p128] + [p128] * (2 * DSA_KV_HEADS)
    return pl.pallas_call(
        functools.partial(_dsa_kernel, kc=kc, k_top=k_top),
        grid=(b, n_qb),
        in_specs=in_specs,
        out_specs=pl.BlockSpec((Q_BLOCK, DSA_HEADS * DSA_DV), lambda bi, i: (bi * n_qb + i, 0)),
        out_shape=jax.ShapeDtypeStruct((b * s, DSA_HEADS * DSA_DV), BF16),
        scratch_shapes=[
            pltpu.VMEM((s // kc, kc, Q_BLOCK), jnp.int32),
            pltpu.VMEM((DSA_KV_HEADS, rows, V7X_LANES), F32),
            pltpu.VMEM((DSA_KV_HEADS, rows, DSA_DV + V7X_LANES), F32),
        ],
        compiler_params=_params(("parallel", "arbitrary"), est),
        name="dsa_attention",
    )(*operands)


def _pad_heads(w_t, heads, dk, dkp):
    k = w_t.shape[1]
    return jnp.pad(w_t.reshape(heads, dk, k), ((0, 0), (0, dkp - dk), (0, 0))).reshape(heads * dkp, k)


def _split_kv(w_t, heads, dk, dv):
    k = w_t.shape[1]
    w3 = w_t.reshape(heads, dk + dv, k)
    return w3[:, :dk].reshape(heads * dk, k), w3[:, dk:].reshape(heads * dv, k)


def _mixer_weights(w_in, d_model):
    sizes = (
        NSA_HEADS * NSA_DK, NSA_GROUPS * (NSA_DK + NSA_DV), NSA_GROUPS * (NSA_DK + NSA_DV),
        NSA_GROUPS * (NSA_DK + NSA_DV), 3 * NSA_HEADS, DSA_HEADS * DSA_DK, DSA_KV_HEADS * (DSA_DK + DSA_DV),
        IDX_HEADS * IDX_DIM, IDX_DIM, IDX_HEADS, 2 * d_model,
    )
    offs = np.cumsum((0,) + sizes)
    assert offs[-1] == w_in.shape[1]
    w_t = jnp.swapaxes(w_in, 0, 1).astype(BF16)
    q_a, _, kv_s, kv_w, gate_a, _, _, _, _, w_i, _ = (w_t[offs[n]:offs[n + 1]] for n in range(len(sizes)))
    k_s, v_s = _split_kv(kv_s, NSA_GROUPS, NSA_DK, NSA_DV)
    k_w, v_w = _split_kv(kv_w, NSA_GROUPS, NSA_DK, NSA_DV)
    w192 = jnp.concatenate([
        _pad_heads(q_a, NSA_HEADS, NSA_DK, NSA_DKP), _pad_heads(k_s, NSA_GROUPS, NSA_DK, NSA_DKP),
        _pad_heads(k_w, NSA_GROUPS, NSA_DK, NSA_DKP)], axis=0)
    scale192 = jnp.concatenate([
        jnp.full((NSA_HEADS * NSA_DKP,), NSA_DK ** -0.5 * LOG2E, F32), jnp.ones((2 * NSA_GROUPS * NSA_DKP,), F32)])
    kv_flag = jnp.tile(jnp.concatenate([jnp.ones((DSA_DK,), F32), jnp.zeros((DSA_DV,), F32)]), DSA_KV_HEADS)
    scale128 = jnp.concatenate([
        jnp.full((DSA_HEADS * DSA_DK,), DSA_DK ** -0.5 * LOG2E, F32), jnp.ones_like(kv_flag),
        jnp.full((IDX_HEADS * IDX_DIM,), IDX_DIM ** -0.5, F32)])
    flag128 = jnp.concatenate([
        jnp.ones((DSA_HEADS * DSA_DK,), F32), kv_flag, jnp.ones((IDX_HEADS * IDX_DIM,), F32)])
    wv = jnp.concatenate([v_s, v_w], axis=0)
    small_n = GATE_COLS + IDX_HEADS
    wsmall = jnp.pad(jnp.concatenate([gate_a, w_i], axis=0), ((0, V7X_LANES - small_n), (0, 0)))
    return w_t, offs, w192, scale192, scale128, flag128, wv, wsmall


def _ffn_up_kernel(h_ref, w1_ref, w3_ref, o_ref, wb_ref):
    tn = o_ref.shape[1]
    half = tn // 2 if tn % (2 * V7X_LANES) == 0 else tn

    @pl.when(pl.program_id(1) == 0)
    def _():
        for c in range(tn // half):
            cols = slice(c * half, (c + 1) * half)
            wb_ref[:, 2 * c * half:(2 * c + 1) * half] = w1_ref[:, cols].astype(BF16)
            wb_ref[:, (2 * c + 1) * half:(2 * c + 2) * half] = w3_ref[:, cols].astype(BF16)

    h = h_ref[...]
    for c in range(tn // half):
        ab = jnp.dot(h, wb_ref[:, 2 * c * half:(2 * c + 2) * half], preferred_element_type=F32)
        o_ref[:, c * half:(c + 1) * half] = (jax.nn.silu(ab[:, :half]) * ab[:, half:]).astype(o_ref.dtype)


def _ffn_up(h, w1, w3, col0, n_cols, cfg):
    t, d = h.shape
    tm = _tile(cfg.seq, 1024)
    tn = _tile(n_cols, 256)
    assert col0 % tn == 0
    j0 = col0 // tn
    est = (2 * _nbytes((tm, d), BF16) + 4 * _nbytes((d, tn), F32) + 3 * _nbytes((d, tn), BF16)
           + 2 * _nbytes((tm, tn), BF16) + 4 * _nbytes((tm, tn), F32) + (2 << 20))
    return pl.pallas_call(
        _ffn_up_kernel,
        grid=(n_cols // tn, t // tm),
        in_specs=[
            pl.BlockSpec((tm, d), lambda j, i: (i, 0)),
            pl.BlockSpec((d, tn), lambda j, i: (0, j0 + j)),
            pl.BlockSpec((d, tn), lambda j, i: (0, j0 + j)),
        ],
        out_specs=pl.BlockSpec((tm, tn), lambda j, i: (i, j)),
        out_shape=jax.ShapeDtypeStruct((t, n_cols), BF16),
        scratch_shapes=[pltpu.VMEM((d, 2 * tn), BF16)],
        compiler_params=_params(("parallel", "arbitrary"), est),
        name="ffn_up_tail",
    )(h, w1, w3)


FFN_TN = 512


def _ffn_up_wide_kernel(h_ref, w1_hbm, w3_hbm, o_ref, stage_ref, wb_ref, sem, *, n_j, n_i):
    j, i = pl.program_id(0), pl.program_id(1)
    tn = o_ref.shape[1]
    slot = lax.rem(j, 2)

    def fetch(w_hbm, jj):
        return pltpu.make_async_copy(w_hbm.at[:, pl.ds(pl.multiple_of(jj * tn, tn), tn)], stage_ref, sem)

    @pl.when((j == 0) & (i == 0))
    def _():
        fetch(w1_hbm, 0).start()
        fetch(w1_hbm, 0).wait()
        wb_ref[0, :, :tn] = stage_ref[...].astype(BF16)
        fetch(w3_hbm, 0).start()
        fetch(w3_hbm, 0).wait()
        wb_ref[0, :, tn:] = stage_ref[...].astype(BF16)

    has_next = j + 1 < n_j

    @pl.when(has_next & (i == 0))
    def _():
        fetch(w1_hbm, j + 1).start()

    @pl.when(has_next & (i == n_i // 2 - 1))
    def _():
        fetch(w1_hbm, j + 1).wait()
        wb_ref[1 - slot, :, :tn] = stage_ref[...].astype(BF16)
        fetch(w3_hbm, j + 1).start()

    @pl.when(has_next & (i == n_i - 1))
    def _():
        fetch(w3_hbm, j + 1).wait()
        wb_ref[1 - slot, :, tn:] = stage_ref[...].astype(BF16)

    ab = jnp.dot(h_ref[...], wb_ref[slot], preferred_element_type=F32)
    o_ref[...] = (jax.nn.silu(ab[:, :tn]) * ab[:, tn:]).astype(o_ref.dtype)


def _ffn_up_wide(h, w1, w3, n_cols, cfg):
    t, d = h.shape
    tm, tn = _tile(cfg.seq, 1024), FFN_TN
    n_j, n_i = n_cols // tn, t // tm
    assert n_cols % tn == 0 and n_i >= 4 and n_i % 2 == 0
    est = (2 * _nbytes((tm, d), BF16) + _nbytes((d, tn), F32) + 2 * _nbytes((d, 2 * tn), BF16)
           + 2 * _nbytes((tm, tn), BF16) + 3 * _nbytes((tm, 2 * tn), F32) + (2 << 20))
    return pl.pallas_call(
        functools.partial(_ffn_up_wide_kernel, n_j=n_j, n_i=n_i),
        grid=(n_j, n_i),
        in_specs=[
            pl.BlockSpec((tm, d), lambda j, i: (i, 0)),
            pl.BlockSpec(memory_space=pl.ANY),
            pl.BlockSpec(memory_space=pl.ANY),
        ],
        out_specs=pl.BlockSpec((tm, tn), lambda j, i: (i, j)),
        out_shape=jax.ShapeDtypeStruct((t, n_cols), BF16),
        scratch_shapes=[pltpu.VMEM((d, tn), F32), pltpu.VMEM((2, d, 2 * tn), BF16), pltpu.SemaphoreType.DMA],
        compiler_params=_params(("arbitrary", "arbitrary"), est),
        name="ffn_up",
    )(h, w1, w3)


def _epi_residual_sum(accs, e_refs, cols, *, coef):
    return e_refs[0][:, cols] + (coef * e_refs[1][0][:, cols]) * sum(accs[1:], accs[0])


def _ffn(x2, gain, mod3, idx, w1, w3, w2, cfg):
    t, d = x2.shape
    ff = w1.shape[1]
    h = _norm_mod(x2, gain, mod3, idx, idx + 1, cfg)
    n_main = (ff // FFN_TN) * FFN_TN
    us = [_ffn_up_wide(h, w1, w3, n_main, cfg)]
    w2s = [w2[:n_main].astype(BF16)]
    if n_main < ff:
        us.append(_ffn_up(h, w1, w3, n_main, ff - n_main, cfg))
        w2s.append(w2[n_main:].astype(BF16))
    tm = _tile(cfg.seq, 512)
    nsb = cfg.seq // tm
    tn = _tile(d, 512)
    epi = [(x2, (tm, tn), lambda i, j: (i, j)),
           (mod3, (1, 1, tn), lambda i, j: ((i // nsb) * N_MOD + idx + 2, 0, j))]
    return _matmul(us, w2s, [(n, n) for n in range(len(us))], functools.partial(_epi_residual_sum, coef=0.5),
                   epi, d, F32, tm=tm, tn=tn, name="ffn_down")


def _mixer(x2, gain, mod3, w_in, cmp_pe_k, cmp_w1_k, cmp_w2_k, cmp_pe_v, cmp_w1_v, cmp_w2_v,
           proj_a, proj_b, w_out, cfg):
    b, s, d = cfg.batch, cfg.seq, cfg.d_model
    w_t, offs, w192, scale192, scale128, flag128, wv, wsmall = _mixer_weights(w_in, d)
    h = _norm_mod(x2, gain, mod3, 3, 4, cfg)
    p192 = _project_rope(h, w192, 0, scale192, None, NSA_DK, NSA_DKP, cfg, "proj_rope192")
    p128 = _project_rope(h, w_t, int(offs[5]), scale128, flag128, DSA_DK, DSA_DK, cfg, "proj_rope128")
    k_idx = _project_rope(h, w_t, int(offs[8]), jnp.ones((IDX_DIM,), F32), None, IDX_DIM, IDX_DIM, cfg,
                          "proj_kidx")
    pv = _project(h, wv, 0, wv.shape[0], _epi_plain, BF16, cfg, "proj_values")
    pc = _project(h, w_t, int(offs[1]), NSA_GROUPS * (NSA_DK + NSA_DV), _epi_plain, F32, cfg, "proj_cmp")
    small = _project(h, wsmall, 0, V7X_LANES, _epi_plain, F32, cfg, "proj_small")
    gates_m = _project(h, w_t, int(offs[10]), 2 * d, _epi_sigmoid, BF16, cfg, "proj_merge")

    n_rows = s // CMP_STRIDE
    pc5 = pc.reshape(b, n_rows, CMP_STRIDE, NSA_GROUPS, NSA_DK + NSA_DV)

    def chunks(x5):
        return jnp.transpose(x5, (0, 3, 1, 2, 4)).reshape(b * NSA_GROUPS, n_rows, CMP_STRIDE * x5.shape[-1])

    cmp_pos = jnp.arange(n_rows) * CMP_STRIDE + CMP_BLOCK - 1
    k_cmp = _compress(chunks(pc5[..., :NSA_DK]), cmp_pe_k, cmp_w1_k, cmp_w2_k, NSA_DK, NSA_DKP, cmp_pos,
                      "compress_k").reshape(b, NSA_GROUPS, n_rows, NSA_DKP)
    v_cmp = _compress(chunks(pc5[..., NSA_DK:]), cmp_pe_v, cmp_w1_v, cmp_w2_v, NSA_DV, NSA_DV, None,
                      "compress_v").reshape(b, NSA_GROUPS, n_rows, NSA_DV)

    o_a = _nsa_attention(p192, pv, small, k_cmp, v_cmp, cfg)
    o_b = _dsa_attention(p128, k_idx, small, cfg)

    tm = _tile(s, 1024)
    tn = _tile(d, 1024)
    nsb = s // tm
    epi = [(gates_m, (tm, tn), lambda i, j: (i, j)), (gates_m, (tm, tn), lambda i, j: (i, j + d // tn))]
    y = _matmul([o_a, o_b], [proj_a.astype(BF16), proj_b.astype(BF16)], [(0, 0), (1, 1)], _epi_merge, epi,
                d, BF16, tm=tm, tn=tn, name="merge_proj")
    epi = [(x2, (tm, tn), lambda i, j: (i, j)),
           (mod3, (1, 1, tn), lambda i, j: ((i // nsb) * N_MOD + 5, 0, j))]
    return _matmul([y], [w_out.astype(BF16)], [(0, 0)], functools.partial(_epi_residual, coef=1.0), epi,
                   d, F32, tm=tm, tn=tn, name="out_proj")


def _layer(x, c, ada_w, ada_b, norm_ffn1, ffn1_w1, ffn1_w3, ffn1_w2, norm_mix, w_in,
           cmp_pe_k, cmp_w1_k, cmp_w2_k, cmp_pe_v, cmp_w1_v, cmp_w2_v, proj_a, proj_b, w_out,
           norm_ffn2, ffn2_w1, ffn2_w3, ffn2_w2, norm_final):
    b, s, d = x.shape
    cfg = Cfg(b, s, d, ffn1_w1.shape[-1])
    assert s % Q_BLOCK == 0 and s >= WINDOW + Q_BLOCK and d % V7X_LANES == 0
    x2 = x.reshape(b * s, d)
    c_pad = jnp.pad(c, ((0, 8 - b), (0, 0)))
    for l in range(ada_w.shape[0]):
        mod = _adaln(c_pad, ada_w[l], ada_b[l].reshape(1, -1))
        mod3 = mod[:b].reshape(b * N_MOD, 1, d)
        x2 = _ffn(x2, norm_ffn1[l], mod3, 0, ffn1_w1[l], ffn1_w3[l], ffn1_w2[l], cfg)
        x2 = _mixer(x2, norm_mix[l], mod3, w_in[l], cmp_pe_k[l], cmp_w1_k[l], cmp_w2_k[l],
                    cmp_pe_v[l], cmp_w1_v[l], cmp_w2_v[l], proj_a[l], proj_b[l], w_out[l], cfg)
        x2 = _ffn(x2, norm_ffn2[l], mod3, 6, ffn2_w1[l], ffn2_w3[l], ffn2_w2[l], cfg)
    return _final_norm(x2, norm_final).reshape(b, s, d)


def kernel(x, c, ada_w, ada_b, norm_ffn1, ffn1_w1, ffn1_w3, ffn1_w2, norm_mix, w_in, cmp_pe_k, cmp_w1_k, cmp_w2_k, cmp_pe_v, cmp_w1_v, cmp_w2_v, proj_a, proj_b, w_out, norm_ffn2, ffn2_w1, ffn2_w3, ffn2_w2, norm_final):
    return _layer(x, c, ada_w, ada_b, norm_ffn1, ffn1_w1, ffn1_w3, ffn1_w2, norm_mix, w_in,
                  cmp_pe_k, cmp_w1_k, cmp_w2_k, cmp_pe_v, cmp_w1_v, cmp_w2_v, proj_a, proj_b, w_out,
                  norm_ffn2, ffn2_w1, ffn2_w3, ffn2_w2, norm_final)
```

```python
import functools
import math
from typing import NamedTuple

import numpy as np
import jax
import jax.numpy as jnp
from jax import lax
from jax.experimental import pallas as pl
from jax.experimental.pallas import tpu as pltpu

F32 = jnp.float32
BF16 = jnp.bfloat16

V7X_LANES = 128
V7X_VMEM_BYTES = 64 * 1024 * 1024
V7X_VMEM_BUDGET = 56 * 1024 * 1024

EPS = 1e-6
ROPE_THETA = 500000.0
ROPE_FRACTION = 4
N_MOD = 9
Q_BLOCK = 128

NSA_HEADS = 16
NSA_GROUPS = 2
NSA_HPG = NSA_HEADS // NSA_GROUPS
NSA_DK = 192
NSA_DKP = 256
NSA_DV = 128
CMP_BLOCK = 32
CMP_STRIDE = 16
SEL_BLOCK = 64
SEL_TOP = 16
WINDOW = 512

DSA_HEADS = 16
DSA_KV_HEADS = 4
DSA_HPG = DSA_HEADS // DSA_KV_HEADS
DSA_DK = 128
DSA_DV = 128
IDX_HEADS = 32
IDX_DIM = 128
DSA_TOPK = 256

GATE_COLS = 3 * NSA_HEADS
INT_MIN = np.int32(-(2**31))
NEG_INF = float("-inf")
LOG2E = math.log2(math.e)
TINY = float(np.finfo(np.float32).tiny)
DN_T = (((1,), (1,)), ((), ()))


class Cfg(NamedTuple):
    batch: int
    seq: int
    d_model: int
    d_ff: int


def _tile(n, pref):
    if n <= pref:
        return n
    t = (pref // V7X_LANES) * V7X_LANES
    while t > V7X_LANES and n % t:
        t -= V7X_LANES
    assert n % t == 0, (n, pref)
    return t


def _nbytes(shape, dtype):
    return int(np.prod(shape)) * jnp.dtype(dtype).itemsize


def _params(dims, est_bytes):
    limit = int(min(V7X_VMEM_BUDGET, max(est_bytes, 16 * 1024 * 1024)))
    return pltpu.CompilerParams(dimension_semantics=dims, vmem_limit_bytes=limit)


def _adaln_kernel(c_ref, w_ref, b_ref, o_ref, acc_ref, *, nk):
    k = pl.program_id(1)

    @pl.when(k == 0)
    def _():
        acc_ref[...] = jnp.zeros_like(acc_ref)

    c = c_ref[...]
    c_act = (c * jax.nn.sigmoid(c)).astype(BF16)
    acc_ref[...] += jnp.dot(c_act, w_ref[...].astype(BF16), preferred_element_type=F32)

    @pl.when(k == nk - 1)
    def _():
        o_ref[...] = acc_ref[...] + b_ref[...]


def _adaln(c_pad, w, b):
    m, kdim = c_pad.shape
    n = w.shape[1]
    tn, tk = _tile(n, 2048), _tile(kdim, 1024)
    nk = kdim // tk
    est = 2 * _nbytes((tk, tn), F32) + _nbytes((tk, tn), BF16) + 4 * _nbytes((m, tn), F32) + (2 << 20)
    return pl.pallas_call(
        functools.partial(_adaln_kernel, nk=nk),
        grid=(n // tn, nk),
        in_specs=[
            pl.BlockSpec((m, tk), lambda j, k: (0, k)),
            pl.BlockSpec((tk, tn), lambda j, k: (k, j)),
            pl.BlockSpec((1, tn), lambda j, k: (0, j)),
        ],
        out_specs=pl.BlockSpec((m, tn), lambda j, k: (0, j)),
        out_shape=jax.ShapeDtypeStruct((m, n), F32),
        scratch_shapes=[pltpu.VMEM((m, tn), F32)],
        compiler_params=_params(("parallel", "arbitrary"), est),
        name="adaln",
    )(c_pad, w, b)


def _norm_mod_kernel(x_ref, g_ref, sh_ref, sc_ref, *rest):
    x = x_ref[...]
    y = x * lax.rsqrt(jnp.mean(x * x, axis=-1, keepdims=True) + EPS)
    y = y * g_ref[...]
    h = (y * (1.0 + sc_ref[0]) + sh_ref[0]).astype(BF16)
    if len(rest) == 1:
        rest[0][...] = h
        return
    wt_ref, o_ref, ut_ref = rest
    o_ref[...] = h
    tn = ut_ref.shape[1]
    ab = jnp.dot(h, wt_ref[...], preferred_element_type=F32)
    ut_ref[...] = (jax.nn.silu(ab[:, :tn]) * ab[:, tn:]).astype(ut_ref.dtype)


def _norm_mod(x2, gain, mod3, shift_idx, scale_idx, cfg, tail_w=None):
    t, d = x2.shape
    tm = _tile(cfg.seq, 512)
    nsb = cfg.seq // tm
    est = 4 * _nbytes((tm, d), F32) + 2 * _nbytes((tm, d), BF16) + (2 << 20)
    ins = [x2, gain.reshape(1, d), mod3, mod3]
    in_specs = [
        pl.BlockSpec((tm, d), lambda i: (i, 0)),
        pl.BlockSpec((1, d), lambda i: (0, 0)),
        pl.BlockSpec((1, 1, d), lambda i: ((i // nsb) * N_MOD + shift_idx, 0, 0)),
        pl.BlockSpec((1, 1, d), lambda i: ((i // nsb) * N_MOD + scale_idx, 0, 0)),
    ]
    out_specs = pl.BlockSpec((tm, d), lambda i: (i, 0))
    out_shape = jax.ShapeDtypeStruct((t, d), BF16)
    if tail_w is not None:
        n_tail = tail_w.shape[1] // 2
        ins.append(tail_w)
        in_specs.append(pl.BlockSpec(tail_w.shape, lambda i: (0, 0)))
        out_specs = [out_specs, pl.BlockSpec((tm, n_tail), lambda i: (i, 0))]
        out_shape = [out_shape, jax.ShapeDtypeStruct((t, n_tail), BF16)]
        est += 2 * _nbytes(tail_w.shape, BF16) + 4 * _nbytes((tm, 2 * n_tail), F32)
    return pl.pallas_call(
        _norm_mod_kernel,
        grid=(t // tm,),
        in_specs=in_specs,
        out_specs=out_specs,
        out_shape=out_shape,
        compiler_params=_params(("parallel",), est),
        name="norm_mod",
    )(*ins)


def _final_norm_kernel(x_ref, g_ref, o_ref):
    x = x_ref[...]
    y = x * lax.rsqrt(jnp.mean(x * x, axis=-1, keepdims=True) + EPS)
    o_ref[...] = y * g_ref[...]


def _final_norm(x2, gain):
    t, d = x2.shape
    tm = _tile(t, 512)
    est = 6 * _nbytes((tm, d), F32) + (2 << 20)
    return pl.pallas_call(
        _final_norm_kernel,
        grid=(t // tm,),
        in_specs=[pl.BlockSpec((tm, d), lambda i: (i, 0)), pl.BlockSpec((1, d), lambda i: (0, 0))],
        out_specs=pl.BlockSpec((tm, d), lambda i: (i, 0)),
        out_shape=jax.ShapeDtypeStruct((t, d), F32),
        compiler_params=_params(("parallel",), est),
        name="final_norm",
    )(x2, gain.reshape(1, d))


MM_SUB = 256


def _mm_kernel(*refs, n_a, n_w, pairs, n_epi, epilogue, sub, w_rows):
    a_refs = refs[:n_a]
    w_refs = refs[n_a:n_a + n_w]
    e_refs = refs[n_a + n_w:n_a + n_w + n_epi]
    o_ref = refs[n_a + n_w + n_epi]
    for c in range(o_ref.shape[1] // sub):
        cols = slice(c * sub, (c + 1) * sub)
        if w_rows:
            accs = [lax.dot_general(a_refs[ai][...], w_refs[wi][cols, :], DN_T, preferred_element_type=F32)
                    for ai, wi in pairs]
        else:
            accs = [jnp.dot(a_refs[ai][...], w_refs[wi][:, cols], preferred_element_type=F32)
                    for ai, wi in pairs]
        o_ref[:, cols] = epilogue(accs, e_refs, cols).astype(o_ref.dtype)


def _matmul(a_list, w_list, pairs, epilogue, epi_inputs, n_out, out_dtype, *, tm, tn, name, w_rows=False,
            w_row0=0, w_row_blocks=None):
    m = a_list[0].shape[0]
    assert m % tm == 0
    sub = MM_SUB if tn % MM_SUB == 0 else tn
    in_specs, est = [], 0
    for a in a_list:
        in_specs.append(pl.BlockSpec((tm, a.shape[1]), lambda i, j: (i, 0)))
        est += 2 * _nbytes((tm, a.shape[1]), a.dtype)
    for w in w_list:
        if w_rows:
            kdim = w.shape[1]
            row_tile = 8 * 4 // jnp.dtype(w.dtype).itemsize
            assert w_row0 % row_tile == 0 and tn % row_tile == 0
            in_specs.append(pl.BlockSpec(
                (pl.Element(tn), pl.Element(kdim)),
                lambda i, j, row_tile=row_tile: (pl.multiple_of(w_row0 + j * tn, row_tile), 0)))
        else:
            kdim, row_block = (w.shape[0], 0) if w_row_blocks is None else w_row_blocks[len(in_specs) - len(a_list)]
            in_specs.append(pl.BlockSpec((kdim, tn), lambda i, j, row_block=row_block: (row_block, j)))
        est += 2 * _nbytes((kdim, tn), w.dtype)
    for arr, bshape, imap in epi_inputs:
        in_specs.append(pl.BlockSpec(bshape, imap))
        est += 2 * _nbytes(bshape, arr.dtype)
    est += 2 * _nbytes((tm, tn), out_dtype) + 4 * (len(pairs) + 3) * _nbytes((tm, sub), F32) + (2 << 20)
    kern = functools.partial(_mm_kernel, n_a=len(a_list), n_w=len(w_list), pairs=tuple(pairs),
                             n_epi=len(epi_inputs), epilogue=epilogue, sub=sub, w_rows=w_rows)
    return pl.pallas_call(
        kern,
        grid=(m // tm, pl.cdiv(n_out, tn)),
        in_specs=in_specs,
        out_specs=pl.BlockSpec((tm, tn), lambda i, j: (i, j)),
        out_shape=jax.ShapeDtypeStruct((m, n_out), out_dtype),
        compiler_params=_params(("parallel", "parallel"), est),
        name=name,
    )(*a_list, *w_list, *[e[0] for e in epi_inputs])


def _epi_plain(accs, e_refs, cols):
    return accs[0]


def _epi_sigmoid(accs, e_refs, cols):
    return jax.nn.sigmoid(accs[0])


def _epi_merge(accs, e_refs, cols):
    return e_refs[0][:, cols].astype(F32) * accs[0] + e_refs[1][:, cols].astype(F32) * accs[1]


def _epi_residual(accs, e_refs, cols, *, coef):
    return e_refs[0][:, cols] + (coef * e_refs[1][0][:, cols]) * accs[0]


def _rope_apply(a, cos, sin_lo, sin_hi, half):
    width = a.shape[-1]
    reps = (1, width // cos.shape[-1])
    return (a * jnp.tile(cos, reps) + pltpu.roll(a, width - half, 1) * jnp.tile(sin_lo, reps)
            + pltpu.roll(a, half, 1) * jnp.tile(sin_hi, reps))


def _epi_rope(accs, e_refs, cols, *, half):
    cos, sin_lo, sin_hi = e_refs[0][...], e_refs[1][...], e_refs[2][...]
    if len(e_refs) > 4:
        flag = e_refs[4][:, cols]
        reps = (1, flag.shape[-1] // cos.shape[-1])
        cos = 1.0 + flag * (jnp.tile(cos, reps) - 1.0)
        sin_lo, sin_hi = flag * jnp.tile(sin_lo, reps), flag * jnp.tile(sin_hi, reps)
    return _rope_apply(accs[0], cos, sin_lo, sin_hi, half) * e_refs[3][:, cols]


def _rope_tables(pos, dk, head_w):
    r = dk // ROPE_FRACTION
    half = r // 2
    n = pos.shape[0]
    inv = ROPE_THETA ** (-jnp.arange(0, r, 2, dtype=F32) / r)
    ang = pos.astype(F32)[:, None] * inv[None, :]
    cos, sin = jnp.cos(ang), jnp.sin(ang)
    cos_t = jnp.concatenate([cos, cos, jnp.ones((n, head_w - r), F32)], axis=1)
    sin_lo = jnp.concatenate([-sin, jnp.zeros((n, head_w - half), F32)], axis=1)
    sin_hi = jnp.concatenate([jnp.zeros((n, half), F32), sin, jnp.zeros((n, head_w - r), F32)], axis=1)
    return cos_t, sin_lo, sin_hi, half


def _project_rope(h, w_t, row0, colscale, flag, dk, head_w, cfg, name):
    n = colscale.shape[0]
    tm = _tile(cfg.seq, 1024)
    tn = min(n, 1024)
    nsb = cfg.seq // tm
    cos_t, sin_lo, sin_hi, half = _rope_tables(jnp.arange(cfg.seq), dk, head_w)
    epi = [(tb, (tm, head_w), lambda i, j: (i % nsb, 0)) for tb in (cos_t, sin_lo, sin_hi)]
    epi.append((colscale.reshape(1, n), (1, tn), lambda i, j: (0, j)))
    if flag is not None:
        epi.append((flag.reshape(1, n), (1, tn), lambda i, j: (0, j)))
    return _matmul([h], [w_t], [(0, 0)], functools.partial(_epi_rope, half=half), epi, n, BF16,
                   tm=tm, tn=tn, name=name, w_rows=True, w_row0=row0)


def _project(h, w_t, row0, n, epilogue, out_dtype, cfg, name):
    return _matmul([h], [w_t], [(0, 0)], epilogue, [], n, out_dtype,
                   tm=_tile(cfg.seq, 1024), tn=_tile(n, 1024), name=name, w_rows=True, w_row0=row0)


def _gelu_tanh(x):
    return 0.5 * x * (1.0 + jnp.tanh(math.sqrt(2.0 / math.pi) * (x + 0.044715 * (x * x * x))))


def _compress_kernel(x_ref, pe_ref, w1_ref, w2_ref, *rest, half):
    o_ref = rest[-1]
    x = x_ref[0]
    n_rows, width = x.shape
    a_lo = (x + pe_ref[0:1, :]).astype(BF16)
    a_hi = (x + pe_ref[1:2, :]).astype(BF16)
    p = jnp.dot(a_lo, w1_ref[0:width, :], preferred_element_type=F32)
    q = jnp.dot(a_hi, w1_ref[width:2 * width, :], preferred_element_type=F32)
    hid = _gelu_tanh(p + pltpu.roll(q, n_rows - 1, 0))
    y = jnp.dot(hid.astype(BF16), w2_ref[...], preferred_element_type=F32)
    if half:
        y = _rope_apply(y, rest[0][...], rest[1][...], rest[2][...], half)
    o_ref[0] = y.astype(o_ref.dtype)


def _compress(x_chunks, pe, w1, w2, d, d_out, rope_pos, name):
    bg, n_rows, width = x_chunks.shape
    hid_w = ((d + V7X_LANES - 1) // V7X_LANES) * V7X_LANES
    pe2 = pe.reshape(2, width)
    w1p = jnp.pad(w1.astype(BF16), ((0, 0), (0, hid_w - d)))
    w2p = jnp.pad(w2.astype(BF16), ((0, hid_w - d), (0, d_out - d)))
    ins = [x_chunks, pe2, w1p, w2p]
    in_specs = [
        pl.BlockSpec((1, n_rows, width), lambda i: (i, 0, 0)),
        pl.BlockSpec((2, width), lambda i: (0, 0)),
        pl.BlockSpec((2 * width, hid_w), lambda i: (0, 0)),
        pl.BlockSpec((hid_w, d_out), lambda i: (0, 0)),
    ]
    half = 0
    if rope_pos is not None:
        cos_t, sin_lo, sin_hi, half = _rope_tables(rope_pos, d, d_out)
        ins += [cos_t, sin_lo, sin_hi]
        in_specs += [pl.BlockSpec((n_rows, d_out), lambda i: (0, 0))] * 3
    est = (2 * _nbytes((n_rows, width), F32) + 2 * _nbytes((n_rows, width), BF16)
           + 2 * _nbytes((2 * width, hid_w), BF16) + (8 << 20))
    return pl.pallas_call(
        functools.partial(_compress_kernel, half=half),
        grid=(bg,),
        in_specs=in_specs,
        out_specs=pl.BlockSpec((1, n_rows, d_out), lambda i: (i, 0, 0)),
        out_shape=jax.ShapeDtypeStruct((bg, n_rows, d_out), BF16),
        compiler_params=_params(("parallel",), est),
        name=name,
    )(*ins)


def _stack_heads(ref, first, count, width):
    return jnp.concatenate([ref[:, (first + h) * width:(first + h + 1) * width] for h in range(count)], axis=0)


def _mask_bias(mask):
    return jnp.where(mask, 0.0, NEG_INF)


def _with_ones(v):
    return jnp.concatenate([v, jnp.ones((v.shape[0], V7X_LANES), v.dtype)], axis=1)


def _exp2_masked(s3, bias2):
    z = s3 + bias2[None]
    m = jnp.max(z, axis=-1, keepdims=True)
    m = jnp.where(m > NEG_INF, m, 0.0)
    return jnp.exp2(z - m)


def _normalize(pv, dv):
    return pv[:, :dv] / jnp.maximum(pv[:, dv:dv + V7X_LANES], TINY)


def _flash_init(m_ref, acc_ref):
    m_ref[...] = jnp.full(m_ref.shape, NEG_INF, F32)
    acc_ref[...] = jnp.zeros(acc_ref.shape, F32)


def _flash_step(s, bias2, v_ones, heads, m_ref, acc_ref):
    rows, kc = s.shape
    z = (s.reshape(heads, rows // heads, kc) + bias2[None]).reshape(rows, kc)
    m_old = m_ref[...]
    m_new = jnp.maximum(m_old, jnp.max(z, axis=-1, keepdims=True).astype(F32))
    m_safe = jnp.where(m_new > NEG_INF, m_new, 0.0)
    alpha = jnp.exp2(m_old - m_safe)
    p = jnp.exp2(z - jnp.tile(m_safe.astype(z.dtype), (1, kc // V7X_LANES)))
    pv = jnp.dot(p.astype(BF16), v_ones, preferred_element_type=F32)
    acc_ref[...] = jnp.tile(alpha, (1, acc_ref.shape[-1] // V7X_LANES)) * acc_ref[...] + pv
    m_ref[...] = m_new


def _chunk_loop(n_chunks, process):
    def pair_body(j, carry):
        process([2 * j, 2 * j + 1])
        return carry

    lax.fori_loop(0, n_chunks // 2, pair_body, 0)

    @pl.when(n_chunks % 2 == 1)
    def _():
        process([n_chunks - 1])


def _nsa_kernel(q_ref, gate_ref, kc_ref, vc_ref, ks_ref, vs_ref, kw_ref, vw_ref, aggt_ref, o_ref,
                m_ref, acc_ref, *, seq, kc, blk_pad):
    n_cmp = (seq - CMP_BLOCK) // CMP_STRIDE + 1
    n_blk = seq // SEL_BLOCK
    n_sel = min(SEL_TOP, n_blk)
    cmp_pad = aggt_ref.shape[1]
    q0 = pl.program_id(1) * Q_BLOCK
    t_col = q0 + lax.broadcasted_iota(jnp.int32, (Q_BLOCK, 1), 0)
    t_row = q0 + lax.broadcasted_iota(jnp.int32, (1, Q_BLOCK), 1)
    gates = jax.nn.sigmoid(gate_ref[...])

    cidx = lax.broadcasted_iota(jnp.int32, (Q_BLOCK, cmp_pad), 1)
    bias_c = _mask_bias((cidx * CMP_STRIDE + (CMP_BLOCK - 1) <= t_col) & (cidx < n_cmp))

    blk = lax.broadcasted_iota(jnp.int32, (n_blk, Q_BLOCK), 0)
    cur = lax.shift_right_logical(t_row, int(math.log2(SEL_BLOCK)))
    forced = (blk == 0) | (blk == cur) | (blk == cur - 1)
    valid = blk * SEL_BLOCK <= t_row

    win_k = WINDOW + Q_BLOCK
    ws = pl.multiple_of(jnp.maximum(q0 - WINDOW, 0), Q_BLOCK)
    diff = t_col - (ws + lax.broadcasted_iota(jnp.int32, (1, win_k), 1))
    bias_w = _mask_bias((diff >= 0) & (diff < WINDOW)).astype(BF16)
    n_chunks = (q0 + Q_BLOCK + kc - 1) // kc
    rows = NSA_HPG * Q_BLOCK

    q_groups = [_stack_heads(q_ref, g * NSA_HPG, NSA_HPG, NSA_DKP) for g in range(NSA_GROUPS)]
    o_cmp, sels = [], []
    for g in range(NSA_GROUPS):
        qg = q_groups[g]

        s = lax.dot_general(qg, kc_ref[0, g], DN_T, preferred_element_type=F32)
        e_c = _exp2_masked(s.reshape(NSA_HPG, Q_BLOCK, cmp_pad), bias_c)
        p_c = e_c / jnp.maximum(jnp.sum(e_c, axis=-1, keepdims=True), TINY)
        o_c = jnp.dot(p_c.reshape(rows, cmp_pad).astype(BF16), vc_ref[0, g], preferred_element_type=F32)

        p_sum = jnp.sum(p_c, axis=0)
        hi = p_sum.astype(BF16)
        r1 = p_sum - hi.astype(F32)
        mid = r1.astype(BF16)
        lo = (r1 - mid.astype(F32)).astype(BF16)
        aggt = aggt_ref[...]
        imp = (lax.dot_general(aggt, hi, DN_T, preferred_element_type=F32)
               + lax.dot_general(aggt, mid, DN_T, preferred_element_type=F32)
               + lax.dot_general(aggt, lo, DN_T, preferred_element_type=F32))
        val = jnp.where(forced, jnp.inf, jnp.where(valid, imp, NEG_INF))
        rank = jnp.zeros((n_blk, Q_BLOCK), F32)
        for ii in range(n_blk):
            row = val[ii:ii + 1, :]
            beats = (row > val) | ((row == val) & (blk > ii))
            rank = rank + jnp.where(beats, 1.0, 0.0)
        sel_t = jnp.where((rank < n_sel) & (val > NEG_INF), 1.0, 0.0)
        if blk_pad > n_blk:
            sel_t = jnp.concatenate([sel_t, jnp.zeros((blk_pad - n_blk, Q_BLOCK), F32)], axis=0)
        sels.append(jnp.transpose(sel_t).astype(BF16))
        o_cmp.append(o_c)

    _flash_init(m_ref, acc_ref)

    def sel_chunks(chunks):
        scores = []
        for c in chunks:
            k0 = pl.multiple_of(c * kc, kc)
            scores.append([
                lax.dot_general(q_groups[g], ks_ref[pl.ds(k0, kc), g * NSA_DKP:(g + 1) * NSA_DKP], DN_T,
                                preferred_element_type=F32).astype(BF16) for g in range(NSA_GROUPS)])
        for c, s_c in zip(chunks, scores):
            k0 = pl.multiple_of(c * kc, kc)
            kpos_b = k0 + lax.broadcasted_iota(jnp.int32, (blk_pad, kc), 1)
            expand = jnp.where(
                lax.shift_right_logical(kpos_b, int(math.log2(SEL_BLOCK)))
                == lax.broadcasted_iota(jnp.int32, (blk_pad, kc), 0), 1.0, 0.0).astype(BF16)
            causal = k0 + lax.broadcasted_iota(jnp.int32, (1, kc), 1) <= t_col
            for g in range(NSA_GROUPS):
                v_blk = vs_ref[pl.ds(k0, kc), g * NSA_DV:(g + 1) * NSA_DV]
                sel_keys = jnp.dot(sels[g], expand, preferred_element_type=F32)
                bias = _mask_bias((sel_keys > 0.5) & causal).astype(BF16)
                _flash_step(s_c[g], bias, _with_ones(v_blk), NSA_HPG, m_ref.at[g], acc_ref.at[g])

    _chunk_loop(n_chunks, sel_chunks)

    for g in range(NSA_GROUPS):
        qg, o_c = q_groups[g], o_cmp[g]
        o_s = _normalize(acc_ref[g], NSA_DV)

        k_win = kw_ref[pl.ds(ws, win_k), g * NSA_DKP:(g + 1) * NSA_DKP]
        v_win = vw_ref[pl.ds(ws, win_k), g * NSA_DV:(g + 1) * NSA_DV]
        s_w = lax.dot_general(qg, k_win, DN_T, preferred_element_type=F32).astype(BF16)
        e_w = _exp2_masked(s_w.reshape(NSA_HPG, Q_BLOCK, win_k), bias_w)
        o_w = _normalize(jnp.dot(e_w.reshape(rows, win_k).astype(BF16), _with_ones(v_win),
                                 preferred_element_type=F32), NSA_DV)

        for h in range(NSA_HPG):
            c0 = (g * NSA_HPG + h) * 3
            hr = slice(h * Q_BLOCK, (h + 1) * Q_BLOCK)
            o = (gates[:, c0:c0 + 1] * o_c[hr] + gates[:, c0 + 1:c0 + 2] * o_s[hr]
                 + gates[:, c0 + 2:c0 + 3] * o_w[hr])
            head = g * NSA_HPG + h
            o_ref[:, head * NSA_DV:(head + 1) * NSA_DV] = o.astype(o_ref.dtype)


def _agg_t(seq, cmp_pad):
    n_cmp = (seq - CMP_BLOCK) // CMP_STRIDE + 1
    n_blk = seq // SEL_BLOCK
    r = SEL_BLOCK // CMP_STRIDE
    c = CMP_BLOCK // CMP_STRIDE
    j = np.arange(n_blk)[:, None, None, None]
    i = np.arange(cmp_pad)[None, :, None, None]
    m = np.arange(r)[None, None, :, None]
    n = np.arange(c)[None, None, None, :]
    a = np.sum(i == r * j + m - n, axis=(2, 3)).astype(np.float32)
    a = a * (np.arange(cmp_pad)[None, :] < n_cmp)
    return jnp.asarray(a, dtype=BF16)


def _nsa_attention(p192, pv, small, k_cmp, v_cmp, cfg):
    b, s = cfg.batch, cfg.seq
    n_qb = s // Q_BLOCK
    kc = _tile(s, 512)
    cmp_pad = k_cmp.shape[2]
    blk_pad = max(V7X_LANES, s // SEL_BLOCK)
    qw = NSA_HEADS * NSA_DKP
    kw = NSA_GROUPS * NSA_DKP
    vw = NSA_GROUPS * NSA_DV
    rows = NSA_HPG * Q_BLOCK
    est = (2 * (_nbytes((Q_BLOCK, qw), BF16) + 2 * _nbytes((s, kw), BF16) + 2 * _nbytes((s, vw), BF16))
           + 3 * _nbytes((rows, V7X_LANES), F32) + 10 * _nbytes((rows, WINDOW + Q_BLOCK), F32) + (4 << 20))
    return pl.pallas_call(
        functools.partial(_nsa_kernel, seq=s, kc=kc, blk_pad=blk_pad),
        grid=(b, n_qb),
        in_specs=[
            pl.BlockSpec((Q_BLOCK, qw), lambda bi, i: (bi * n_qb + i, 0)),
            pl.BlockSpec((Q_BLOCK, V7X_LANES), lambda bi, i: (bi * n_qb + i, 0)),
            pl.BlockSpec((1, NSA_GROUPS, cmp_pad, NSA_DKP), lambda bi, i: (bi, 0, 0, 0)),
            pl.BlockSpec((1, NSA_GROUPS, cmp_pad, NSA_DV), lambda bi, i: (bi, 0, 0, 0)),
            pl.BlockSpec((s, kw), lambda bi, i: (bi, qw // kw)),
            pl.BlockSpec((s, vw), lambda bi, i: (bi, 0)),
            pl.BlockSpec((s, kw), lambda bi, i: (bi, qw // kw + 1)),
            pl.BlockSpec((s, vw), lambda bi, i: (bi, 1)),
            pl.BlockSpec((s // SEL_BLOCK, cmp_pad), lambda bi, i: (0, 0)),
        ],
        out_specs=pl.BlockSpec((Q_BLOCK, NSA_HEADS * NSA_DV), lambda bi, i: (bi * n_qb + i, 0)),
        out_shape=jax.ShapeDtypeStruct((b * s, NSA_HEADS * NSA_DV), BF16),
        scratch_shapes=[pltpu.VMEM((NSA_GROUPS, rows, V7X_LANES), F32),
                        pltpu.VMEM((NSA_GROUPS, rows, NSA_DV + V7X_LANES), F32)],
        compiler_params=_params(("parallel", "arbitrary"), est),
        name="nsa_attention",
    )(p192, small, k_cmp, v_cmp, p192, pv, p192, pv, _agg_t(s, cmp_pad))


def _dsa_kernel(*refs, kc, k_top):
    hg = 8
    n_qi = IDX_HEADS // hg
    qi_refs = refs[:n_qi]
    w_ref, ki_ref, qb_ref = refs[n_qi:n_qi + 3]
    kb_refs = refs[n_qi + 3:n_qi + 3 + DSA_KV_HEADS]
    vb_refs = refs[n_qi + 3 + DSA_KV_HEADS:n_qi + 3 + 2 * DSA_KV_HEADS]
    o_ref, key_ref, m_ref, acc_ref = refs[n_qi + 3 + 2 * DSA_KV_HEADS:]
    q0 = pl.program_id(1) * Q_BLOCK
    t_row = q0 + lax.broadcasted_iota(jnp.int32, (1, Q_BLOCK), 1)
    n_chunks = (q0 + Q_BLOCK + kc - 1) // kc
    w_t = jnp.transpose(w_ref[...] * (IDX_HEADS ** -0.5))
    cnt_rows = 64

    def causal_t(c):
        return c * kc + lax.broadcasted_iota(jnp.int32, (kc, 1), 0) <= t_row

    def score_body(c, carry):
        k0 = pl.multiple_of(c * kc, kc)
        k_i = ki_ref[pl.ds(k0, kc), :]
        acc = jnp.zeros((kc, Q_BLOCK), F32)
        for g in range(IDX_HEADS // hg):
            q_h = _stack_heads(qi_refs[g], 0, hg, IDX_DIM)
            rel = jnp.maximum(lax.dot_general(k_i, q_h, DN_T, preferred_element_type=F32), 0.0)
            for h in range(hg):
                col = GATE_COLS + g * hg + h
                acc = acc + rel[:, h * Q_BLOCK:(h + 1) * Q_BLOCK] * w_t[col:col + 1, :]
        bits = lax.bitcast_convert_type(acc, jnp.int32)
        skey = bits ^ (lax.shift_right_arithmetic(bits, 31) & np.int32(0x7FFFFFFF))
        key_ref[c] = jnp.where(causal_t(c), skey, INT_MIN)
        return carry

    lax.fori_loop(0, n_chunks, score_body, 0)

    def select_body(it, thr_u):
        bit = lax.shift_left(jnp.int32(1), 31 - it)
        cand = thr_u | bit
        cand_key = cand ^ INT_MIN

        def count_body(c, cnt):
            hit = jnp.where(key_ref[c] >= cand_key, 1.0, 0.0)
            return cnt + jnp.sum(hit.reshape(kc // cnt_rows, cnt_rows, Q_BLOCK), axis=0)

        cnt = lax.fori_loop(0, n_chunks, count_body, jnp.zeros((cnt_rows, Q_BLOCK), F32))
        return jnp.where(jnp.sum(cnt, axis=0, keepdims=True) >= k_top, cand, thr_u)

    thr_u = lax.fori_loop(0, 32, select_body, jnp.zeros((1, Q_BLOCK), jnp.int32))
    thr = thr_u ^ INT_MIN

    _flash_init(m_ref, acc_ref)
    q_groups = [_stack_heads(qb_ref, g * DSA_HPG, DSA_HPG, DSA_DK) for g in range(DSA_KV_HEADS)]

    def attn_chunks(chunks):
        scores = []
        for c in chunks:
            k0 = pl.multiple_of(c * kc, kc)
            scores.append([
                lax.dot_general(q_groups[g], kb_refs[g][pl.ds(k0, kc), :], DN_T,
                                preferred_element_type=F32).astype(BF16) for g in range(DSA_KV_HEADS)])
        for c, s_c in zip(chunks, scores):
            k0 = pl.multiple_of(c * kc, kc)
            bias = jnp.transpose(_mask_bias((key_ref[c] >= thr) & causal_t(c))).astype(BF16)
            for g in range(DSA_KV_HEADS):
                v_blk = vb_refs[g][pl.ds(k0, kc), :]
                _flash_step(s_c[g], bias, _with_ones(v_blk), DSA_HPG, m_ref.at[g], acc_ref.at[g])

    _chunk_loop(n_chunks, attn_chunks)
    for g in range(DSA_KV_HEADS):
        o = _normalize(acc_ref[g], DSA_DV)
        for h in range(DSA_HPG):
            head = g * DSA_HPG + h
            o_ref[:, head * DSA_DV:(head + 1) * DSA_DV] = o[h * Q_BLOCK:(h + 1) * Q_BLOCK].astype(o_ref.dtype)


def _dsa_attention(p128, k_idx, small, cfg):
    b, s = cfg.batch, cfg.seq
    n_qb = s // Q_BLOCK
    kc = _tile(s, 512)
    k_top = min(DSA_TOPK, s // 4)
    qiw = IDX_HEADS * IDX_DIM
    qbw = DSA_HEADS * DSA_DK
    kvw = DSA_KV_HEADS * (DSA_DK + DSA_DV)
    rows = DSA_HPG * Q_BLOCK
    hg = 8
    qi_w = hg * IDX_DIM
    assert (qbw + kvw) % qi_w == 0 and DSA_DK == DSA_DV == IDX_DIM
    kv0 = qbw // DSA_DK
    est = (2 * (_nbytes((Q_BLOCK, qiw + qbw), BF16) + _nbytes((s, IDX_DIM + kvw), BF16))
           + _nbytes((Q_BLOCK, s), jnp.int32) + 12 * _nbytes((8 * Q_BLOCK, kc), F32) + (4 << 20))

    def q_spec(width, col_block):
        return pl.BlockSpec((Q_BLOCK, width), lambda bi, i: (bi * n_qb + i, col_block))

    def k_spec(col_block):
        return pl.BlockSpec((s, DSA_DK), lambda bi, i: (bi, col_block))

    in_specs = [q_spec(qi_w, (qbw + kvw) // qi_w + g) for g in range(IDX_HEADS // hg)]
    in_specs += [q_spec(V7X_LANES, 0), k_spec(0), q_spec(qbw, 0)]
    in_specs += [k_spec(kv0 + 2 * g) for g in range(DSA_KV_HEADS)]
    in_specs += [k_spec(kv0 + 2 * g + 1) for g in range(DSA_KV_HEADS)]
    n_p128 = IDX_HEADS // hg
    operands = [p128] * n_p128 + [small, k_idx, p128] + [p128] * (2 * DSA_KV_HEADS)
    return pl.pallas_call(
        functools.partial(_dsa_kernel, kc=kc, k_top=k_top),
        grid=(b, n_qb),
        in_specs=in_specs,
        out_specs=pl.BlockSpec((Q_BLOCK, DSA_HEADS * DSA_DV), lambda bi, i: (bi * n_qb + i, 0)),
        out_shape=jax.ShapeDtypeStruct((b * s, DSA_HEADS * DSA_DV), BF16),
        scratch_shapes=[
            pltpu.VMEM((s // kc, kc, Q_BLOCK), jnp.int32),
            pltpu.VMEM((DSA_KV_HEADS, rows, V7X_LANES), F32),
            pltpu.VMEM((DSA_KV_HEADS, rows, DSA_DV + V7X_LANES), F32),
        ],
        compiler_params=_params(("parallel", "arbitrary"), est),
        name="dsa_attention",
    )(*operands)


def _pad_heads(w_t, heads, dk, dkp):
    k = w_t.shape[1]
    return jnp.pad(w_t.reshape(heads, dk, k), ((0, 0), (0, dkp - dk), (0, 0))).reshape(heads * dkp, k)


def _split_kv(w_t, heads, dk, dv):
    k = w_t.shape[1]
    w3 = w_t.reshape(heads, dk + dv, k)
    return w3[:, :dk].reshape(heads * dk, k), w3[:, dk:].reshape(heads * dv, k)


def _mixer_weights(w_in, d_model):
    sizes = (
        NSA_HEADS * NSA_DK, NSA_GROUPS * (NSA_DK + NSA_DV), NSA_GROUPS * (NSA_DK + NSA_DV),
        NSA_GROUPS * (NSA_DK + NSA_DV), 3 * NSA_HEADS, DSA_HEADS * DSA_DK, DSA_KV_HEADS * (DSA_DK + DSA_DV),
        IDX_HEADS * IDX_DIM, IDX_DIM, IDX_HEADS, 2 * d_model,
    )
    offs = np.cumsum((0,) + sizes)
    assert offs[-1] == w_in.shape[1]
    w_t = jnp.swapaxes(w_in, 0, 1).astype(BF16)
    q_a, _, kv_s, kv_w, gate_a, _, _, _, _, w_i, _ = (w_t[offs[n]:offs[n + 1]] for n in range(len(sizes)))
    k_s, v_s = _split_kv(kv_s, NSA_GROUPS, NSA_DK, NSA_DV)
    k_w, v_w = _split_kv(kv_w, NSA_GROUPS, NSA_DK, NSA_DV)
    w192 = jnp.concatenate([
        _pad_heads(q_a, NSA_HEADS, NSA_DK, NSA_DKP), _pad_heads(k_s, NSA_GROUPS, NSA_DK, NSA_DKP),
        _pad_heads(k_w, NSA_GROUPS, NSA_DK, NSA_DKP)], axis=0)
    scale192 = jnp.concatenate([
        jnp.full((NSA_HEADS * NSA_DKP,), NSA_DK ** -0.5 * LOG2E, F32), jnp.ones((2 * NSA_GROUPS * NSA_DKP,), F32)])
    kv_flag = jnp.tile(jnp.concatenate([jnp.ones((DSA_DK,), F32), jnp.zeros((DSA_DV,), F32)]), DSA_KV_HEADS)
    scale128 = jnp.concatenate([
        jnp.full((DSA_HEADS * DSA_DK,), DSA_DK ** -0.5 * LOG2E, F32), jnp.ones_like(kv_flag),
        jnp.full((IDX_HEADS * IDX_DIM,), IDX_DIM ** -0.5, F32)])
    flag128 = jnp.concatenate([
        jnp.ones((DSA_HEADS * DSA_DK,), F32), kv_flag, jnp.ones((IDX_HEADS * IDX_DIM,), F32)])
    wv = jnp.concatenate([v_s, v_w], axis=0)
    small_n = GATE_COLS + IDX_HEADS
    wsmall = jnp.pad(jnp.concatenate([gate_a, w_i], axis=0), ((0, V7X_LANES - small_n), (0, 0)))
    return w_t, offs, w192, scale192, scale128, flag128, wv, wsmall


FFN_TN = 512


def _ffn_up_wide_kernel(h_ref, w1_hbm, w3_hbm, o_ref, stage_ref, wb_ref, sem, *, n_j, n_i):
    j, i = pl.program_id(0), pl.program_id(1)
    tn = o_ref.shape[1]
    slot = lax.rem(j, 2)

    def fetch(w_hbm, jj):
        return pltpu.make_async_copy(w_hbm.at[:, pl.ds(pl.multiple_of(jj * tn, tn), tn)], stage_ref, sem)

    @pl.when((j == 0) & (i == 0))
    def _():
        fetch(w1_hbm, 0).start()
        fetch(w1_hbm, 0).wait()
        wb_ref[0, :, :tn] = stage_ref[...].astype(BF16)
        fetch(w3_hbm, 0).start()
        fetch(w3_hbm, 0).wait()
        wb_ref[0, :, tn:] = stage_ref[...].astype(BF16)

    has_next = j + 1 < n_j

    @pl.when(has_next & (i == 0))
    def _():
        fetch(w1_hbm, j + 1).start()

    @pl.when(has_next & (i == n_i // 2 - 1))
    def _():
        fetch(w1_hbm, j + 1).wait()
        wb_ref[1 - slot, :, :tn] = stage_ref[...].astype(BF16)
        fetch(w3_hbm, j + 1).start()

    @pl.when(has_next & (i == n_i - 1))
    def _():
        fetch(w3_hbm, j + 1).wait()
        wb_ref[1 - slot, :, tn:] = stage_ref[...].astype(BF16)

    ab = jnp.dot(h_ref[...], wb_ref[slot], preferred_element_type=F32)
    o_ref[...] = (jax.nn.silu(ab[:, :tn]) * ab[:, tn:]).astype(o_ref.dtype)


def _ffn_up_wide(h, w1, w3, n_cols, cfg):
    t, d = h.shape
    tm, tn = _tile(cfg.seq, 1024), FFN_TN
    n_j, n_i = n_cols // tn, t // tm
    assert n_cols % tn == 0 and n_i >= 4 and n_i % 2 == 0
    est = (2 * _nbytes((tm, d), BF16) + _nbytes((d, tn), F32) + 2 * _nbytes((d, 2 * tn), BF16)
           + 2 * _nbytes((tm, tn), BF16) + 3 * _nbytes((tm, 2 * tn), F32) + (2 << 20))
    return pl.pallas_call(
        functools.partial(_ffn_up_wide_kernel, n_j=n_j, n_i=n_i),
        grid=(n_j, n_i),
        in_specs=[
            pl.BlockSpec((tm, d), lambda j, i: (i, 0)),
            pl.BlockSpec(memory_space=pl.ANY),
            pl.BlockSpec(memory_space=pl.ANY),
        ],
        out_specs=pl.BlockSpec((tm, tn), lambda j, i: (i, j)),
        out_shape=jax.ShapeDtypeStruct((t, n_cols), BF16),
        scratch_shapes=[pltpu.VMEM((d, tn), F32), pltpu.VMEM((2, d, 2 * tn), BF16), pltpu.SemaphoreType.DMA],
        compiler_params=_params(("arbitrary", "arbitrary"), est),
        name="ffn_up",
    )(h, w1, w3)


def _epi_residual_sum(accs, e_refs, cols, *, coef):
    return e_refs[0][:, cols] + (coef * e_refs[1][0][:, cols]) * sum(accs[1:], accs[0])


def _ffn(x2, gain, mod3, idx, w1, w3, w2, cfg):
    t, d = x2.shape
    ff = w1.shape[1]
    n_main = (ff // FFN_TN) * FFN_TN
    n_tail = ff - n_main
    w2b = w2.astype(BF16)
    if n_tail:
        assert n_main % n_tail == 0
        tail_w = jnp.concatenate([w1[:, n_main:], w3[:, n_main:]], axis=1).astype(BF16)
        h, u_tail = _norm_mod(x2, gain, mod3, idx, idx + 1, cfg, tail_w)
        us, w2s, row_blocks = [None, u_tail], [w2b, w2b], [(n_main, 0), (n_tail, n_main // n_tail)]
    else:
        h = _norm_mod(x2, gain, mod3, idx, idx + 1, cfg)
        us, w2s, row_blocks = [None], [w2b], [(n_main, 0)]
    us[0] = _ffn_up_wide(h, w1, w3, n_main, cfg)
    tm = _tile(cfg.seq, 512)
    nsb = cfg.seq // tm
    tn = _tile(d, 512)
    epi = [(x2, (tm, tn), lambda i, j: (i, j)),
           (mod3, (1, 1, tn), lambda i, j: ((i // nsb) * N_MOD + idx + 2, 0, j))]
    return _matmul(us, w2s, [(n, n) for n in range(len(us))], functools.partial(_epi_residual_sum, coef=0.5),
                   epi, d, F32, tm=tm, tn=tn, name="ffn_down", w_row_blocks=row_blocks)


def _mixer(x2, gain, mod3, w_in, cmp_pe_k, cmp_w1_k, cmp_w2_k, cmp_pe_v, cmp_w1_v, cmp_w2_v,
           proj_a, proj_b, w_out, cfg):
    b, s, d = cfg.batch, cfg.seq, cfg.d_model
    w_t, offs, w192, scale192, scale128, flag128, wv, wsmall = _mixer_weights(w_in, d)
    h = _norm_mod(x2, gain, mod3, 3, 4, cfg)
    p192 = _project_rope(h, w192, 0, scale192, None, NSA_DK, NSA_DKP, cfg, "proj_rope192")
    p128 = _project_rope(h, w_t, int(offs[5]), scale128, flag128, DSA_DK, DSA_DK, cfg, "proj_rope128")
    k_idx = _project_rope(h, w_t, int(offs[8]), jnp.ones((IDX_DIM,), F32), None, IDX_DIM, IDX_DIM, cfg,
                          "proj_kidx")
    pv = _project(h, wv, 0, wv.shape[0], _epi_plain, BF16, cfg, "proj_values")
    pc = _project(h, w_t, int(offs[1]), NSA_GROUPS * (NSA_DK + NSA_DV), _epi_plain, F32, cfg, "proj_cmp")
    small = _project(h, wsmall, 0, V7X_LANES, _epi_plain, F32, cfg, "proj_small")
    gates_m = _project(h, w_t, int(offs[10]), 2 * d, _epi_sigmoid, BF16, cfg, "proj_merge")

    n_rows = s // CMP_STRIDE
    pc5 = pc.reshape(b, n_rows, CMP_STRIDE, NSA_GROUPS, NSA_DK + NSA_DV)

    def chunks(x5):
        return jnp.transpose(x5, (0, 3, 1, 2, 4)).reshape(b * NSA_GROUPS, n_rows, CMP_STRIDE * x5.shape[-1])

    cmp_pos = jnp.arange(n_rows) * CMP_STRIDE + CMP_BLOCK - 1
    k_cmp = _compress(chunks(pc5[..., :NSA_DK]), cmp_pe_k, cmp_w1_k, cmp_w2_k, NSA_DK, NSA_DKP, cmp_pos,
                      "compress_k").reshape(b, NSA_GROUPS, n_rows, NSA_DKP)
    v_cmp = _compress(chunks(pc5[..., NSA_DK:]), cmp_pe_v, cmp_w1_v, cmp_w2_v, NSA_DV, NSA_DV, None,
                      "compress_v").reshape(b, NSA_GROUPS, n_rows, NSA_DV)

    o_a = _nsa_attention(p192, pv, small, k_cmp, v_cmp, cfg)
    o_b = _dsa_attention(p128, k_idx, small, cfg)

    tm = _tile(s, 1024)
    tn = _tile(d, 1024)
    nsb = s // tm
    epi = [(gates_m, (tm, tn), lambda i, j: (i, j)), (gates_m, (tm, tn), lambda i, j: (i, j + d // tn))]
    y = _matmul([o_a, o_b], [proj_a.astype(BF16), proj_b.astype(BF16)], [(0, 0), (1, 1)], _epi_merge, epi,
                d, BF16, tm=tm, tn=tn, name="merge_proj")
    epi = [(x2, (tm, tn), lambda i, j: (i, j)),
           (mod3, (1, 1, tn), lambda i, j: ((i // nsb) * N_MOD + 5, 0, j))]
    return _matmul([y], [w_out.astype(BF16)], [(0, 0)], functools.partial(_epi_residual, coef=1.0), epi,
                   d, F32, tm=tm, tn=tn, name="out_proj")


def _layer(x, c, ada_w, ada_b, norm_ffn1, ffn1_w1, ffn1_w3, ffn1_w2, norm_mix, w_in,
           cmp_pe_k, cmp_w1_k, cmp_w2_k, cmp_pe_v, cmp_w1_v, cmp_w2_v, proj_a, proj_b, w_out,
           norm_ffn2, ffn2_w1, ffn2_w3, ffn2_w2, norm_final):
    b, s, d = x.shape
    cfg = Cfg(b, s, d, ffn1_w1.shape[-1])
    assert s % Q_BLOCK == 0 and s >= WINDOW + Q_BLOCK and d % V7X_LANES == 0
    x2 = x.reshape(b * s, d)
    c_pad = jnp.pad(c, ((0, 8 - b), (0, 0)))
    for l in range(ada_w.shape[0]):
        mod = _adaln(c_pad, ada_w[l], ada_b[l].reshape(1, -1))
        mod3 = mod[:b].reshape(b * N_MOD, 1, d)
        x2 = _ffn(x2, norm_ffn1[l], mod3, 0, ffn1_w1[l], ffn1_w3[l], ffn1_w2[l], cfg)
        x2 = _mixer(x2, norm_mix[l], mod3, w_in[l], cmp_pe_k[l], cmp_w1_k[l], cmp_w2_k[l],
                    cmp_pe_v[l], cmp_w1_v[l], cmp_w2_v[l], proj_a[l], proj_b[l], w_out[l], cfg)
        x2 = _ffn(x2, norm_ffn2[l], mod3, 6, ffn2_w1[l], ffn2_w3[l], ffn2_w2[l], cfg)
    return _final_norm(x2, norm_final).reshape(b, s, d)


def kernel(x, c, ada_w, ada_b, norm_ffn1, ffn1_w1, ffn1_w3, ffn1_w2, norm_mix, w_in, cmp_pe_k, cmp_w1_k, cmp_w2_k, cmp_pe_v, cmp_w1_v, cmp_w2_v, proj_a, proj_b, w_out, norm_ffn2, ffn2_w1, ffn2_w3, ffn2_w2, norm_final):
    return _layer(x, c, ada_w, ada_b, norm_ffn1, ffn1_w1, ffn1_w3, ffn1_w2, norm_mix, w_in,
                  cmp_pe_k, cmp_w1_k, cmp_w2_k, cmp_pe_v, cmp_w1_v, cmp_w2_v, proj_a, proj_b, w_out,
                  norm_ffn2, ffn2_w1, ffn2_w3, ffn2_w2, norm_final)
```

```python
import functools
import math
from typing import NamedTuple

import numpy as np
import jax
import jax.numpy as jnp
from jax import lax
from jax.experimental import pallas as pl
from jax.experimental.pallas import tpu as pltpu

F32 = jnp.float32
BF16 = jnp.bfloat16

V7X_LANES = 128
V7X_SUBLANES = 8
V7X_VMEM_BYTES = 64 * 1024 * 1024
V7X_VMEM_BUDGET = 56 * 1024 * 1024

EPS = 1e-6
ROPE_THETA = 500000.0
ROPE_FRACTION = 4
N_MOD = 9
Q_BLOCK = 128

NSA_HEADS = 16
NSA_GROUPS = 2
NSA_HPG = NSA_HEADS // NSA_GROUPS
NSA_DK = 192
NSA_DKP = 256
NSA_DV = 128
CMP_BLOCK = 32
CMP_STRIDE = 16
SEL_BLOCK = 64
SEL_TOP = 16
WINDOW = 512

DSA_HEADS = 16
DSA_KV_HEADS = 4
DSA_HPG = DSA_HEADS // DSA_KV_HEADS
DSA_DK = 128
DSA_DV = 128
IDX_HEADS = 32
IDX_DIM = 128
DSA_TOPK = 256

GATE_COLS = 3 * NSA_HEADS
INT_MIN = np.int32(-(2**31))
HALF_OFFSET = np.int32(2**15)
NEG_INF = float("-inf")
LOG2E = math.log2(math.e)
TINY = float(np.finfo(np.float32).tiny)
DN_T = (((1,), (1,)), ((), ()))


class Cfg(NamedTuple):
    batch: int
    seq: int
    d_model: int
    d_ff: int


def _tile(n, pref):
    if n <= pref:
        return n
    t = (pref // V7X_LANES) * V7X_LANES
    while t > V7X_LANES and n % t:
        t -= V7X_LANES
    assert n % t == 0, (n, pref)
    return t


def _nbytes(shape, dtype):
    return int(np.prod(shape)) * jnp.dtype(dtype).itemsize


def _params(dims, est_bytes):
    limit = int(min(V7X_VMEM_BUDGET, max(est_bytes, 16 * 1024 * 1024)))
    return pltpu.CompilerParams(dimension_semantics=dims, vmem_limit_bytes=limit)


def _adaln_kernel(c_ref, w_ref, b_ref, o_ref, acc_ref, *, nk):
    k = pl.program_id(1)

    @pl.when(k == 0)
    def _():
        acc_ref[...] = jnp.zeros_like(acc_ref)

    c = c_ref[...]
    c_act = (c * jax.nn.sigmoid(c)).astype(BF16)
    acc_ref[...] += jnp.dot(c_act, w_ref[...].astype(BF16), preferred_element_type=F32)

    @pl.when(k == nk - 1)
    def _():
        o_ref[...] = acc_ref[...] + b_ref[...]


def _adaln(c_pad, w, b):
    m, kdim = c_pad.shape
    n = w.shape[1]
    tn, tk = _tile(n, 2048), _tile(kdim, 1024)
    nk = kdim // tk
    est = 2 * _nbytes((tk, tn), F32) + _nbytes((tk, tn), BF16) + 4 * _nbytes((m, tn), F32) + (2 << 20)
    return pl.pallas_call(
        functools.partial(_adaln_kernel, nk=nk),
        grid=(n // tn, nk),
        in_specs=[
            pl.BlockSpec((m, tk), lambda j, k: (0, k)),
            pl.BlockSpec((tk, tn), lambda j, k: (k, j)),
            pl.BlockSpec((1, tn), lambda j, k: (0, j)),
        ],
        out_specs=pl.BlockSpec((m, tn), lambda j, k: (0, j)),
        out_shape=jax.ShapeDtypeStruct((m, n), F32),
        scratch_shapes=[pltpu.VMEM((m, tn), F32)],
        compiler_params=_params(("parallel", "arbitrary"), est),
        name="adaln",
    )(c_pad, w, b)


def _norm_mod_kernel(x_ref, g_ref, sh_ref, sc_ref, *rest):
    x = x_ref[...]
    y = x * lax.rsqrt(jnp.mean(x * x, axis=-1, keepdims=True) + EPS)
    y = y * g_ref[...]
    h = (y * (1.0 + sc_ref[0]) + sh_ref[0]).astype(BF16)
    if len(rest) == 1:
        rest[0][...] = h
        return
    wt_ref, o_ref, ut_ref = rest
    o_ref[...] = h
    tn = ut_ref.shape[1]
    ab = jnp.dot(h, wt_ref[...], preferred_element_type=F32)
    ut_ref[...] = (jax.nn.silu(ab[:, :tn]) * ab[:, tn:]).astype(ut_ref.dtype)


def _norm_mod(x2, gain, mod3, shift_idx, scale_idx, cfg, tail_w=None):
    t, d = x2.shape
    tm = _tile(cfg.seq, 512)
    nsb = cfg.seq // tm
    est = 4 * _nbytes((tm, d), F32) + 2 * _nbytes((tm, d), BF16) + (2 << 20)
    ins = [x2, gain.reshape(1, d), mod3, mod3]
    in_specs = [
        pl.BlockSpec((tm, d), lambda i: (i, 0)),
        pl.BlockSpec((1, d), lambda i: (0, 0)),
        pl.BlockSpec((1, 1, d), lambda i: ((i // nsb) * N_MOD + shift_idx, 0, 0)),
        pl.BlockSpec((1, 1, d), lambda i: ((i // nsb) * N_MOD + scale_idx, 0, 0)),
    ]
    out_specs = pl.BlockSpec((tm, d), lambda i: (i, 0))
    out_shape = jax.ShapeDtypeStruct((t, d), BF16)
    if tail_w is not None:
        n_tail = tail_w.shape[1] // 2
        ins.append(tail_w)
        in_specs.append(pl.BlockSpec(tail_w.shape, lambda i: (0, 0)))
        out_specs = [out_specs, pl.BlockSpec((tm, n_tail), lambda i: (i, 0))]
        out_shape = [out_shape, jax.ShapeDtypeStruct((t, n_tail), BF16)]
        est += 2 * _nbytes(tail_w.shape, BF16) + 4 * _nbytes((tm, 2 * n_tail), F32)
    return pl.pallas_call(
        _norm_mod_kernel,
        grid=(t // tm,),
        in_specs=in_specs,
        out_specs=out_specs,
        out_shape=out_shape,
        compiler_params=_params(("parallel",), est),
        name="norm_mod",
    )(*ins)


def _final_norm_kernel(x_ref, g_ref, o_ref):
    x = x_ref[...]
    y = x * lax.rsqrt(jnp.mean(x * x, axis=-1, keepdims=True) + EPS)
    o_ref[...] = y * g_ref[...]


def _final_norm(x2, gain):
    t, d = x2.shape
    tm = _tile(t, 512)
    est = 6 * _nbytes((tm, d), F32) + (2 << 20)
    return pl.pallas_call(
        _final_norm_kernel,
        grid=(t // tm,),
        in_specs=[pl.BlockSpec((tm, d), lambda i: (i, 0)), pl.BlockSpec((1, d), lambda i: (0, 0))],
        out_specs=pl.BlockSpec((tm, d), lambda i: (i, 0)),
        out_shape=jax.ShapeDtypeStruct((t, d), F32),
        compiler_params=_params(("parallel",), est),
        name="final_norm",
    )(x2, gain.reshape(1, d))


MM_SUB = 256


def _mm_kernel(*refs, n_a, n_w, pairs, n_epi, epilogue, sub, w_rows):
    a_refs = refs[:n_a]
    w_refs = refs[n_a:n_a + n_w]
    e_refs = refs[n_a + n_w:n_a + n_w + n_epi]
    o_ref = refs[n_a + n_w + n_epi]
    for c in range(o_ref.shape[1] // sub):
        cols = slice(c * sub, (c + 1) * sub)
        if w_rows:
            accs = [lax.dot_general(a_refs[ai][...], w_refs[wi][cols, :], DN_T, preferred_element_type=F32)
                    for ai, wi in pairs]
        else:
            accs = [jnp.dot(a_refs[ai][...], w_refs[wi][:, cols], preferred_element_type=F32)
                    for ai, wi in pairs]
        o_ref[:, cols] = epilogue(accs, e_refs, cols).astype(o_ref.dtype)


def _matmul(a_list, w_list, pairs, epilogue, epi_inputs, n_out, out_dtype, *, tm, tn, name, w_rows=False,
            w_row0=0, w_row_blocks=None):
    m = a_list[0].shape[0]
    assert m % tm == 0
    sub = MM_SUB if tn % MM_SUB == 0 else tn
    in_specs, est = [], 0
    for a in a_list:
        in_specs.append(pl.BlockSpec((tm, a.shape[1]), lambda i, j: (i, 0)))
        est += 2 * _nbytes((tm, a.shape[1]), a.dtype)
    for w in w_list:
        if w_rows:
            kdim = w.shape[1]
            row_tile = V7X_SUBLANES * 4 // jnp.dtype(w.dtype).itemsize
            assert w_row0 % row_tile == 0 and tn % row_tile == 0
            in_specs.append(pl.BlockSpec(
                (pl.Element(tn), pl.Element(kdim)),
                lambda i, j, row_tile=row_tile: (pl.multiple_of(w_row0 + j * tn, row_tile), 0)))
        else:
            kdim, row_block = (w.shape[0], 0) if w_row_blocks is None else w_row_blocks[len(in_specs) - len(a_list)]
            in_specs.append(pl.BlockSpec((kdim, tn), lambda i, j, row_block=row_block: (row_block, j)))
        est += 2 * _nbytes((kdim, tn), w.dtype)
    for arr, bshape, imap in epi_inputs:
        in_specs.append(pl.BlockSpec(bshape, imap))
        est += 2 * _nbytes(bshape, arr.dtype)
    est += 2 * _nbytes((tm, tn), out_dtype) + 4 * (len(pairs) + 3) * _nbytes((tm, sub), F32) + (2 << 20)
    kern = functools.partial(_mm_kernel, n_a=len(a_list), n_w=len(w_list), pairs=tuple(pairs),
                             n_epi=len(epi_inputs), epilogue=epilogue, sub=sub, w_rows=w_rows)
    return pl.pallas_call(
        kern,
        grid=(m // tm, pl.cdiv(n_out, tn)),
        in_specs=in_specs,
        out_specs=pl.BlockSpec((tm, tn), lambda i, j: (i, j)),
        out_shape=jax.ShapeDtypeStruct((m, n_out), out_dtype),
        compiler_params=_params(("parallel", "parallel"), est),
        name=name,
    )(*a_list, *w_list, *[e[0] for e in epi_inputs])


def _epi_plain(accs, e_refs, cols):
    return accs[0]


def _epi_sigmoid(accs, e_refs, cols):
    return jax.nn.sigmoid(accs[0])


def _epi_merge(accs, e_refs, cols):
    return e_refs[0][:, cols].astype(F32) * accs[0] + e_refs[1][:, cols].astype(F32) * accs[1]


def _epi_residual(accs, e_refs, cols, *, coef):
    return e_refs[0][:, cols] + (coef * e_refs[1][0][:, cols]) * accs[0]


def _rope_apply(a, cos, sin_lo, sin_hi, half):
    width = a.shape[-1]
    reps = (1, width // cos.shape[-1])
    return (a * jnp.tile(cos, reps) + pltpu.roll(a, width - half, 1) * jnp.tile(sin_lo, reps)
            + pltpu.roll(a, half, 1) * jnp.tile(sin_hi, reps))


def _epi_rope(accs, e_refs, cols, *, half):
    cos, sin_lo, sin_hi = e_refs[0][...], e_refs[1][...], e_refs[2][...]
    if len(e_refs) > 4:
        flag = e_refs[4][:, cols]
        reps = (1, flag.shape[-1] // cos.shape[-1])
        cos = 1.0 + flag * (jnp.tile(cos, reps) - 1.0)
        sin_lo, sin_hi = flag * jnp.tile(sin_lo, reps), flag * jnp.tile(sin_hi, reps)
    return _rope_apply(accs[0], cos, sin_lo, sin_hi, half) * e_refs[3][:, cols]


def _rope_tables(pos, dk, head_w):
    r = dk // ROPE_FRACTION
    half = r // 2
    n = pos.shape[0]
    inv = ROPE_THETA ** (-jnp.arange(0, r, 2, dtype=F32) / r)
    ang = pos.astype(F32)[:, None] * inv[None, :]
    cos, sin = jnp.cos(ang), jnp.sin(ang)
    cos_t = jnp.concatenate([cos, cos, jnp.ones((n, head_w - r), F32)], axis=1)
    sin_lo = jnp.concatenate([-sin, jnp.zeros((n, head_w - half), F32)], axis=1)
    sin_hi = jnp.concatenate([jnp.zeros((n, half), F32), sin, jnp.zeros((n, head_w - r), F32)], axis=1)
    return cos_t, sin_lo, sin_hi, half


def _project_rope(h, w_t, row0, colscale, flag, dk, head_w, cfg, name):
    n = colscale.shape[0]
    tm = _tile(cfg.seq, 1024)
    tn = min(n, 1024)
    nsb = cfg.seq // tm
    cos_t, sin_lo, sin_hi, half = _rope_tables(jnp.arange(cfg.seq), dk, head_w)
    epi = [(tb, (tm, head_w), lambda i, j: (i % nsb, 0)) for tb in (cos_t, sin_lo, sin_hi)]
    epi.append((colscale.reshape(1, n), (1, tn), lambda i, j: (0, j)))
    if flag is not None:
        epi.append((flag.reshape(1, n), (1, tn), lambda i, j: (0, j)))
    return _matmul([h], [w_t], [(0, 0)], functools.partial(_epi_rope, half=half), epi, n, BF16,
                   tm=tm, tn=tn, name=name, w_rows=True, w_row0=row0)


def _project(h, w_t, row0, n, epilogue, out_dtype, cfg, name):
    return _matmul([h], [w_t], [(0, 0)], epilogue, [], n, out_dtype,
                   tm=_tile(cfg.seq, 1024), tn=_tile(n, 1024), name=name, w_rows=True, w_row0=row0)


def _gelu_tanh(x):
    return 0.5 * x * (1.0 + jnp.tanh(math.sqrt(2.0 / math.pi) * (x + 0.044715 * (x * x * x))))


def _compress_kernel(x_ref, pe_ref, w1_ref, w2_ref, *rest, half):
    o_ref = rest[-1]
    x = x_ref[0]
    n_rows, width = x.shape
    a_lo = (x + pe_ref[0:1, :]).astype(BF16)
    a_hi = (x + pe_ref[1:2, :]).astype(BF16)
    p = jnp.dot(a_lo, w1_ref[0:width, :], preferred_element_type=F32)
    q = jnp.dot(a_hi, w1_ref[width:2 * width, :], preferred_element_type=F32)
    hid = _gelu_tanh(p + pltpu.roll(q, n_rows - 1, 0))
    y = jnp.dot(hid.astype(BF16), w2_ref[...], preferred_element_type=F32)
    if half:
        y = _rope_apply(y, rest[0][...], rest[1][...], rest[2][...], half)
    o_ref[0] = y.astype(o_ref.dtype)


def _compress(x_chunks, pe, w1, w2, d, d_out, rope_pos, name):
    bg, n_rows, width = x_chunks.shape
    hid_w = ((d + V7X_LANES - 1) // V7X_LANES) * V7X_LANES
    pe2 = pe.reshape(2, width)
    w1p = jnp.pad(w1.astype(BF16), ((0, 0), (0, hid_w - d)))
    w2p = jnp.pad(w2.astype(BF16), ((0, hid_w - d), (0, d_out - d)))
    ins = [x_chunks, pe2, w1p, w2p]
    in_specs = [
        pl.BlockSpec((1, n_rows, width), lambda i: (i, 0, 0)),
        pl.BlockSpec((2, width), lambda i: (0, 0)),
        pl.BlockSpec((2 * width, hid_w), lambda i: (0, 0)),
        pl.BlockSpec((hid_w, d_out), lambda i: (0, 0)),
    ]
    half = 0
    if rope_pos is not None:
        cos_t, sin_lo, sin_hi, half = _rope_tables(rope_pos, d, d_out)
        ins += [cos_t, sin_lo, sin_hi]
        in_specs += [pl.BlockSpec((n_rows, d_out), lambda i: (0, 0))] * 3
    est = (2 * _nbytes((n_rows, width), F32) + 2 * _nbytes((n_rows, width), BF16)
           + 2 * _nbytes((2 * width, hid_w), BF16) + (8 << 20))
    return pl.pallas_call(
        functools.partial(_compress_kernel, half=half),
        grid=(bg,),
        in_specs=in_specs,
        out_specs=pl.BlockSpec((1, n_rows, d_out), lambda i: (i, 0, 0)),
        out_shape=jax.ShapeDtypeStruct((bg, n_rows, d_out), BF16),
        compiler_params=_params(("parallel",), est),
        name=name,
    )(*ins)


def _stack_heads(ref, first, count, width):
    return jnp.concatenate([ref[:, (first + h) * width:(first + h + 1) * width] for h in range(count)], axis=0)


def _mask_bias(mask):
    return jnp.where(mask, 0.0, NEG_INF)


def _with_ones(v):
    return jnp.concatenate([v, jnp.ones((v.shape[0], V7X_LANES), v.dtype)], axis=1)


def _exp2_masked(s3, bias2):
    z = s3 + bias2[None]
    m = jnp.max(z, axis=-1, keepdims=True)
    m = jnp.where(m > NEG_INF, m, 0.0)
    return jnp.exp2(z - m)


def _normalize(pv, dv):
    return pv[:, :dv] / jnp.maximum(pv[:, dv:dv + V7X_LANES], TINY)


def _flash_init(m_ref, acc_ref):
    m_ref[...] = jnp.full(m_ref.shape, NEG_INF, F32)
    acc_ref[...] = jnp.zeros(acc_ref.shape, F32)


def _flash_step(s, bias2, v_ones, heads, m_ref, acc_ref):
    rows, kc = s.shape
    z = (s.reshape(heads, rows // heads, kc) + bias2[None]).reshape(rows, kc)
    m_old = m_ref[...]
    m_new = jnp.maximum(m_old, jnp.max(z, axis=-1, keepdims=True).astype(F32))
    m_safe = jnp.where(m_new > NEG_INF, m_new, 0.0)
    alpha = jnp.exp2(m_old - m_safe)
    p = jnp.exp2(z - jnp.tile(m_safe.astype(z.dtype), (1, kc // V7X_LANES)))
    pv = jnp.dot(p.astype(BF16), v_ones, preferred_element_type=F32)
    acc_ref[...] = jnp.tile(alpha, (1, acc_ref.shape[-1] // V7X_LANES)) * acc_ref[...] + pv
    m_ref[...] = m_new


def _chunk_loop(n_chunks, process):
    def pair_body(j, carry):
        process([2 * j, 2 * j + 1])
        return carry

    lax.fori_loop(0, n_chunks // 2, pair_body, 0)

    @pl.when(n_chunks % 2 == 1)
    def _():
        process([n_chunks - 1])


def _nsa_kernel(q_ref, gate_ref, kc_ref, vc_ref, ks_ref, vs_ref, kw_ref, vw_ref, aggt_ref, o_ref,
                m_ref, acc_ref, *, seq, kc, blk_pad):
    n_cmp = (seq - CMP_BLOCK) // CMP_STRIDE + 1
    n_blk = seq // SEL_BLOCK
    n_sel = min(SEL_TOP, n_blk)
    cmp_pad = aggt_ref.shape[1]
    q0 = pl.program_id(1) * Q_BLOCK
    t_col = q0 + lax.broadcasted_iota(jnp.int32, (Q_BLOCK, 1), 0)
    t_row = q0 + lax.broadcasted_iota(jnp.int32, (1, Q_BLOCK), 1)
    gates = jax.nn.sigmoid(gate_ref[...])

    cidx = lax.broadcasted_iota(jnp.int32, (Q_BLOCK, cmp_pad), 1)
    bias_c = _mask_bias((cidx * CMP_STRIDE + (CMP_BLOCK - 1) <= t_col) & (cidx < n_cmp))

    blk = lax.broadcasted_iota(jnp.int32, (n_blk, Q_BLOCK), 0)
    cur = lax.shift_right_logical(t_row, int(math.log2(SEL_BLOCK)))
    forced = (blk == 0) | (blk == cur) | (blk == cur - 1)
    valid = blk * SEL_BLOCK <= t_row

    win_k = WINDOW + Q_BLOCK
    ws = pl.multiple_of(jnp.maximum(q0 - WINDOW, 0), Q_BLOCK)
    diff = t_col - (ws + lax.broadcasted_iota(jnp.int32, (1, win_k), 1))
    bias_w = _mask_bias((diff >= 0) & (diff < WINDOW)).astype(BF16)
    n_chunks = (q0 + Q_BLOCK + kc - 1) // kc
    rows = NSA_HPG * Q_BLOCK

    q_groups = [_stack_heads(q_ref, g * NSA_HPG, NSA_HPG, NSA_DKP) for g in range(NSA_GROUPS)]
    o_cmp, sels = [], []
    for g in range(NSA_GROUPS):
        qg = q_groups[g]

        s = lax.dot_general(qg, kc_ref[0, g], DN_T, preferred_element_type=F32)
        e_c = _exp2_masked(s.reshape(NSA_HPG, Q_BLOCK, cmp_pad), bias_c)
        p_c = e_c / jnp.maximum(jnp.sum(e_c, axis=-1, keepdims=True), TINY)
        o_c = jnp.dot(p_c.reshape(rows, cmp_pad).astype(BF16), vc_ref[0, g], preferred_element_type=F32)

        p_sum = jnp.sum(p_c, axis=0)
        hi = p_sum.astype(BF16)
        r1 = p_sum - hi.astype(F32)
        mid = r1.astype(BF16)
        lo = (r1 - mid.astype(F32)).astype(BF16)
        aggt = aggt_ref[...]
        imp = (lax.dot_general(aggt, hi, DN_T, preferred_element_type=F32)
               + lax.dot_general(aggt, mid, DN_T, preferred_element_type=F32)
               + lax.dot_general(aggt, lo, DN_T, preferred_element_type=F32))
        val = jnp.where(forced, jnp.inf, jnp.where(valid, imp, NEG_INF))
        rank = jnp.zeros((n_blk, Q_BLOCK), F32)
        for ii in range(n_blk):
            row = val[ii:ii + 1, :]
            beats = (row > val) | ((row == val) & (blk > ii))
            rank = rank + jnp.where(beats, 1.0, 0.0)
        sel_t = jnp.where((rank < n_sel) & (val > NEG_INF), 1.0, 0.0)
        if blk_pad > n_blk:
            sel_t = jnp.concatenate([sel_t, jnp.zeros((blk_pad - n_blk, Q_BLOCK), F32)], axis=0)
        sels.append(jnp.transpose(sel_t).astype(BF16))
        o_cmp.append(o_c)

    _flash_init(m_ref, acc_ref)

    def sel_chunks(chunks):
        scores = []
        for c in chunks:
            k0 = pl.multiple_of(c * kc, kc)
            scores.append([
                lax.dot_general(q_groups[g], ks_ref[pl.ds(k0, kc), g * NSA_DKP:(g + 1) * NSA_DKP], DN_T,
                                preferred_element_type=F32).astype(BF16) for g in range(NSA_GROUPS)])
        for c, s_c in zip(chunks, scores):
            k0 = pl.multiple_of(c * kc, kc)
            kpos_b = k0 + lax.broadcasted_iota(jnp.int32, (blk_pad, kc), 1)
            expand = jnp.where(
                lax.shift_right_logical(kpos_b, int(math.log2(SEL_BLOCK)))
                == lax.broadcasted_iota(jnp.int32, (blk_pad, kc), 0), 1.0, 0.0).astype(BF16)
            causal = k0 + lax.broadcasted_iota(jnp.int32, (1, kc), 1) <= t_col
            for g in range(NSA_GROUPS):
                v_blk = vs_ref[pl.ds(k0, kc), g * NSA_DV:(g + 1) * NSA_DV]
                sel_keys = jnp.dot(sels[g], expand, preferred_element_type=F32)
                bias = _mask_bias((sel_keys > 0.5) & causal).astype(BF16)
                _flash_step(s_c[g], bias, _with_ones(v_blk), NSA_HPG, m_ref.at[g], acc_ref.at[g])

    _chunk_loop(n_chunks, sel_chunks)

    for g in range(NSA_GROUPS):
        qg, o_c = q_groups[g], o_cmp[g]
        o_s = _normalize(acc_ref[g], NSA_DV)

        k_win = kw_ref[pl.ds(ws, win_k), g * NSA_DKP:(g + 1) * NSA_DKP]
        v_win = vw_ref[pl.ds(ws, win_k), g * NSA_DV:(g + 1) * NSA_DV]
        s_w = lax.dot_general(qg, k_win, DN_T, preferred_element_type=F32).astype(BF16)
        e_w = _exp2_masked(s_w.reshape(NSA_HPG, Q_BLOCK, win_k), bias_w)
        o_w = _normalize(jnp.dot(e_w.reshape(rows, win_k), _with_ones(v_win),
                                 preferred_element_type=F32), NSA_DV)

        for h in range(NSA_HPG):
            c0 = (g * NSA_HPG + h) * 3
            hr = slice(h * Q_BLOCK, (h + 1) * Q_BLOCK)
            o = (gates[:, c0:c0 + 1] * o_c[hr] + gates[:, c0 + 1:c0 + 2] * o_s[hr]
                 + gates[:, c0 + 2:c0 + 3] * o_w[hr])
            head = g * NSA_HPG + h
            o_ref[:, head * NSA_DV:(head + 1) * NSA_DV] = o.astype(o_ref.dtype)


def _agg_t(seq, cmp_pad):
    n_cmp = (seq - CMP_BLOCK) // CMP_STRIDE + 1
    n_blk = seq // SEL_BLOCK
    r = SEL_BLOCK // CMP_STRIDE
    c = CMP_BLOCK // CMP_STRIDE
    j = np.arange(n_blk)[:, None, None, None]
    i = np.arange(cmp_pad)[None, :, None, None]
    m = np.arange(r)[None, None, :, None]
    n = np.arange(c)[None, None, None, :]
    a = np.sum(i == r * j + m - n, axis=(2, 3)).astype(np.float32)
    a = a * (np.arange(cmp_pad)[None, :] < n_cmp)
    return jnp.asarray(a, dtype=BF16)


def _nsa_attention(p192, pv, small, small_col, k_cmp, v_cmp, cfg):
    b, s = cfg.batch, cfg.seq
    n_qb = s // Q_BLOCK
    kc = _tile(s, 512)
    cmp_pad = k_cmp.shape[2]
    blk_pad = max(V7X_LANES, s // SEL_BLOCK)
    qw = NSA_HEADS * NSA_DKP
    kw = NSA_GROUPS * NSA_DKP
    vw = NSA_GROUPS * NSA_DV
    rows = NSA_HPG * Q_BLOCK
    est = (2 * (_nbytes((Q_BLOCK, qw), BF16) + 2 * _nbytes((s, kw), BF16) + 2 * _nbytes((s, vw), BF16))
           + 3 * _nbytes((rows, V7X_LANES), F32) + 10 * _nbytes((rows, WINDOW + Q_BLOCK), F32) + (4 << 20))
    return pl.pallas_call(
        functools.partial(_nsa_kernel, seq=s, kc=kc, blk_pad=blk_pad),
        grid=(b, n_qb),
        in_specs=[
            pl.BlockSpec((Q_BLOCK, qw), lambda bi, i: (bi * n_qb + i, 0)),
            pl.BlockSpec((Q_BLOCK, V7X_LANES), lambda bi, i: (bi * n_qb + i, small_col)),
            pl.BlockSpec((1, NSA_GROUPS, cmp_pad, NSA_DKP), lambda bi, i: (bi, 0, 0, 0)),
            pl.BlockSpec((1, NSA_GROUPS, cmp_pad, NSA_DV), lambda bi, i: (bi, 0, 0, 0)),
            pl.BlockSpec((s, kw), lambda bi, i: (bi, qw // kw)),
            pl.BlockSpec((s, vw), lambda bi, i: (bi, 0)),
            pl.BlockSpec((s, kw), lambda bi, i: (bi, qw // kw + 1)),
            pl.BlockSpec((s, vw), lambda bi, i: (bi, 1)),
            pl.BlockSpec((s // SEL_BLOCK, cmp_pad), lambda bi, i: (0, 0)),
        ],
        out_specs=pl.BlockSpec((Q_BLOCK, NSA_HEADS * NSA_DV), lambda bi, i: (bi * n_qb + i, 0)),
        out_shape=jax.ShapeDtypeStruct((b * s, NSA_HEADS * NSA_DV), BF16),
        scratch_shapes=[pltpu.VMEM((NSA_GROUPS, rows, V7X_LANES), F32),
                        pltpu.VMEM((NSA_GROUPS, rows, NSA_DV + V7X_LANES), F32)],
        compiler_params=_params(("parallel", "arbitrary"), est),
        name="nsa_attention",
    )(p192, small, k_cmp, v_cmp, p192, pv, p192, pv, _agg_t(s, cmp_pad))


def _dsa_kernel(*refs, kc, k_top):
    hg = 8
    n_qi = IDX_HEADS // hg
    qi_refs = refs[:n_qi]
    w_ref, ki_ref, qb_ref = refs[n_qi:n_qi + 3]
    kb_refs = refs[n_qi + 3:n_qi + 3 + DSA_KV_HEADS]
    vb_refs = refs[n_qi + 3 + DSA_KV_HEADS:n_qi + 3 + 2 * DSA_KV_HEADS]
    o_ref, key_ref, hi_ref, lo_ref, tie_ref, m_ref, acc_ref = refs[n_qi + 3 + 2 * DSA_KV_HEADS:]
    q0 = pl.program_id(1) * Q_BLOCK
    t_row = q0 + lax.broadcasted_iota(jnp.int32, (1, Q_BLOCK), 1)
    n_chunks = (q0 + Q_BLOCK + kc - 1) // kc
    w_t = jnp.transpose(w_ref[...] * (IDX_HEADS ** -0.5))
    cnt_rows = 64

    def causal_t(c):
        return c * kc + lax.broadcasted_iota(jnp.int32, (kc, 1), 0) <= t_row

    def score_body(c, carry):
        k0 = pl.multiple_of(c * kc, kc)
        k_i = ki_ref[pl.ds(k0, kc), :]
        acc = jnp.zeros((kc, Q_BLOCK), F32)
        for g in range(IDX_HEADS // hg):
            q_h = _stack_heads(qi_refs[g], 0, hg, IDX_DIM)
            rel = jnp.maximum(lax.dot_general(k_i, q_h, DN_T, preferred_element_type=F32), 0.0)
            for h in range(hg):
                col = GATE_COLS + g * hg + h
                acc = acc + rel[:, h * Q_BLOCK:(h + 1) * Q_BLOCK] * w_t[col:col + 1, :]
        bits = lax.bitcast_convert_type(acc, jnp.int32)
        skey = bits ^ (lax.shift_right_arithmetic(bits, 31) & np.int32(0x7FFFFFFF))
        key = jnp.where(causal_t(c), skey, INT_MIN)
        key_ref[c] = key
        hi_ref[c] = lax.shift_right_arithmetic(key, 16).astype(jnp.int16)
        lo_ref[c] = ((key & np.int32(0xFFFF)) - HALF_OFFSET).astype(jnp.int16)
        return carry

    lax.fori_loop(0, n_chunks, score_body, 0)

    def fold16(hit):
        parts = hit.reshape(kc // cnt_rows, cnt_rows, Q_BLOCK)
        acc = parts[0]
        for r in range(1, kc // cnt_rows):
            acc = acc + parts[r]
        return acc

    def count16(pred):
        def body(c, cnt):
            return cnt + fold16(jnp.where(pred(c), jnp.int16(1), jnp.int16(0)))

        cnt = lax.fori_loop(0, n_chunks, body, jnp.zeros((cnt_rows, Q_BLOCK), jnp.int16))
        return jnp.sum(cnt.astype(jnp.int32), axis=0, keepdims=True)

    def search16(values, need):
        def body(it, t_u):
            cand = t_u | lax.shift_left(jnp.int32(1), 15 - it)
            cand_s = (cand - HALF_OFFSET).astype(jnp.int16)
            return jnp.where(count16(lambda c: values(c) >= cand_s) >= need, cand, t_u)

        return lax.fori_loop(0, 16, body, jnp.zeros((1, Q_BLOCK), jnp.int32))

    hi_u = search16(lambda c: hi_ref[c], k_top)
    hi_s = (hi_u - HALF_OFFSET).astype(jnp.int16)
    def above_and_mask(c):
        hi = hi_ref[c]
        lo_ref[c] = jnp.where(hi == hi_s, lo_ref[c], jnp.int16(-HALF_OFFSET))
        return hi > hi_s

    need_lo = k_top - count16(above_and_mask)
    lo_u = search16(lambda c: lo_ref[c], need_lo)
    thr = lax.shift_left(hi_u - HALF_OFFSET, 16) | lo_u

    def fold(hit):
        return jnp.sum(hit.reshape(kc // cnt_rows, cnt_rows, Q_BLOCK), axis=0)

    def tie_body(c, carry):
        keys = key_ref[c]
        above = fold(jnp.where(keys > thr, 1.0, 0.0))
        tied = fold(jnp.where((keys == thr) & (keys > INT_MIN), 1.0, 0.0))
        return carry[0] + above, carry[1] + tied

    zero_cnt = jnp.zeros((cnt_rows, Q_BLOCK), F32)
    above, tied = lax.fori_loop(0, n_chunks, tie_body, (zero_cnt, zero_cnt))
    need = k_top - jnp.sum(above, axis=0, keepdims=True)
    excess = jnp.sum(tied, axis=0, keepdims=True) - need
    tie_ref[...] = jnp.full(tie_ref.shape, np.int32(2**31 - 1), jnp.int32)
    pos_bits = max(1, (key_ref.shape[0] * kc - 1).bit_length())

    @pl.when(jnp.max(excess) > 0.0)
    def _():
        def pos_body(it, last):
            cand = last | lax.shift_left(jnp.int32(1), pos_bits - 1 - it)

            def count_body(c, cnt):
                pos = c * kc + lax.broadcasted_iota(jnp.int32, (kc, 1), 0)
                return cnt + fold(jnp.where((key_ref[c] == thr) & (pos < cand), 1.0, 0.0))

            cnt = lax.fori_loop(0, n_chunks, count_body, zero_cnt)
            return jnp.where(jnp.sum(cnt, axis=0, keepdims=True) < need, cand, last)

        tie_ref[...] = lax.fori_loop(0, pos_bits, pos_body, jnp.zeros((1, Q_BLOCK), jnp.int32))

    last_tie = tie_ref[...]

    def selected_t(c):
        keys = key_ref[c]
        pos = c * kc + lax.broadcasted_iota(jnp.int32, (kc, 1), 0)
        return ((keys > thr) | ((keys == thr) & (pos <= last_tie))) & (pos <= t_row)

    _flash_init(m_ref, acc_ref)
    q_groups = [_stack_heads(qb_ref, g * DSA_HPG, DSA_HPG, DSA_DK) for g in range(DSA_KV_HEADS)]

    def attn_chunks(chunks):
        scores = []
        for c in chunks:
            k0 = pl.multiple_of(c * kc, kc)
            scores.append([
                lax.dot_general(q_groups[g], kb_refs[g][pl.ds(k0, kc), :], DN_T,
                                preferred_element_type=F32).astype(BF16) for g in range(DSA_KV_HEADS)])
        for c, s_c in zip(chunks, scores):
            k0 = pl.multiple_of(c * kc, kc)
            bias = jnp.transpose(_mask_bias(selected_t(c))).astype(BF16)
            for g in range(DSA_KV_HEADS):
                v_blk = vb_refs[g][pl.ds(k0, kc), :]
                _flash_step(s_c[g], bias, _with_ones(v_blk), DSA_HPG, m_ref.at[g], acc_ref.at[g])

    _chunk_loop(n_chunks, attn_chunks)
    for g in range(DSA_KV_HEADS):
        o = _normalize(acc_ref[g], DSA_DV)
        for h in range(DSA_HPG):
            head = g * DSA_HPG + h
            o_ref[:, head * DSA_DV:(head + 1) * DSA_DV] = o[h * Q_BLOCK:(h + 1) * Q_BLOCK].astype(o_ref.dtype)


def _dsa_attention(p128, k_idx, kidx_col, small, small_col, cfg):
    b, s = cfg.batch, cfg.seq
    n_qb = s // Q_BLOCK
    kc = _tile(s, 512)
    k_top = min(DSA_TOPK, s // 4)
    qiw = IDX_HEADS * IDX_DIM
    qbw = DSA_HEADS * DSA_DK
    kvw = DSA_KV_HEADS * (DSA_DK + DSA_DV)
    rows = DSA_HPG * Q_BLOCK
    hg = 8
    qi_w = hg * IDX_DIM
    assert (qbw + kvw) % qi_w == 0 and DSA_DK == DSA_DV == IDX_DIM
    kv0 = qbw // DSA_DK
    est = (2 * (_nbytes((Q_BLOCK, qiw + qbw), BF16) + _nbytes((s, IDX_DIM + kvw), BF16))
           + _nbytes((Q_BLOCK, s), jnp.int32) + 12 * _nbytes((8 * Q_BLOCK, kc), F32) + (4 << 20))

    def q_spec(width, col_block):
        return pl.BlockSpec((Q_BLOCK, width), lambda bi, i: (bi * n_qb + i, col_block))

    def k_spec(col_block):
        return pl.BlockSpec((s, DSA_DK), lambda bi, i: (bi, col_block))

    in_specs = [q_spec(qi_w, (qbw + kvw) // qi_w + g) for g in range(IDX_HEADS // hg)]
    in_specs += [q_spec(V7X_LANES, small_col), k_spec(kidx_col), q_spec(qbw, 0)]
    in_specs += [k_spec(kv0 + 2 * g) for g in range(DSA_KV_HEADS)]
    in_specs += [k_spec(kv0 + 2 * g + 1) for g in range(DSA_KV_HEADS)]
    n_p128 = IDX_HEADS // hg
    operands = [p128] * n_p128 + [small, k_idx, p128] + [p128] * (2 * DSA_KV_HEADS)
    return pl.pallas_call(
        functools.partial(_dsa_kernel, kc=kc, k_top=k_top),
        grid=(b, n_qb),
        in_specs=in_specs,
        out_specs=pl.BlockSpec((Q_BLOCK, DSA_HEADS * DSA_DV), lambda bi, i: (bi * n_qb + i, 0)),
        out_shape=jax.ShapeDtypeStruct((b * s, DSA_HEADS * DSA_DV), BF16),
        scratch_shapes=[
            pltpu.VMEM((s // kc, kc, Q_BLOCK), jnp.int32),
            pltpu.VMEM((s // kc, kc, Q_BLOCK), jnp.int16),
            pltpu.VMEM((s // kc, kc, Q_BLOCK), jnp.int16),
            pltpu.VMEM((1, Q_BLOCK), jnp.int32),
            pltpu.VMEM((DSA_KV_HEADS, rows, V7X_LANES), F32),
            pltpu.VMEM((DSA_KV_HEADS, rows, DSA_DV + V7X_LANES), F32),
        ],
        compiler_params=_params(("parallel", "arbitrary"), est),
        name="dsa_attention",
    )(*operands)


def _pad_heads(w_t, heads, dk, dkp):
    k = w_t.shape[1]
    return jnp.pad(w_t.reshape(heads, dk, k), ((0, 0), (0, dkp - dk), (0, 0))).reshape(heads * dkp, k)


def _split_kv(w_t, heads, dk, dv):
    k = w_t.shape[1]
    w3 = w_t.reshape(heads, dk + dv, k)
    return w3[:, :dk].reshape(heads * dk, k), w3[:, dk:].reshape(heads * dv, k)


def _mixer_weights(w_in, d_model):
    sizes = (
        NSA_HEADS * NSA_DK, NSA_GROUPS * (NSA_DK + NSA_DV), NSA_GROUPS * (NSA_DK + NSA_DV),
        NSA_GROUPS * (NSA_DK + NSA_DV), 3 * NSA_HEADS, DSA_HEADS * DSA_DK, DSA_KV_HEADS * (DSA_DK + DSA_DV),
        IDX_HEADS * IDX_DIM, IDX_DIM, IDX_HEADS, 2 * d_model,
    )
    offs = np.cumsum((0,) + sizes)
    assert offs[-1] == w_in.shape[1]
    w_t = jnp.swapaxes(w_in, 0, 1).astype(BF16)
    q_a, _, kv_s, kv_w, gate_a, _, _, _, _, w_i, _ = (w_t[offs[n]:offs[n + 1]] for n in range(len(sizes)))
    k_s, v_s = _split_kv(kv_s, NSA_GROUPS, NSA_DK, NSA_DV)
    k_w, v_w = _split_kv(kv_w, NSA_GROUPS, NSA_DK, NSA_DV)
    w192 = jnp.concatenate([
        _pad_heads(q_a, NSA_HEADS, NSA_DK, NSA_DKP), _pad_heads(k_s, NSA_GROUPS, NSA_DK, NSA_DKP),
        _pad_heads(k_w, NSA_GROUPS, NSA_DK, NSA_DKP)], axis=0)
    scale192 = jnp.concatenate([
        jnp.full((NSA_HEADS * NSA_DKP,), NSA_DK ** -0.5 * LOG2E, F32), jnp.ones((2 * NSA_GROUPS * NSA_DKP,), F32)])
    kv_flag = jnp.tile(jnp.concatenate([jnp.ones((DSA_DK,), F32), jnp.zeros((DSA_DV,), F32)]), DSA_KV_HEADS)
    scale128 = jnp.concatenate([
        jnp.full((DSA_HEADS * DSA_DK,), DSA_DK ** -0.5 * LOG2E, F32), jnp.ones_like(kv_flag),
        jnp.full((IDX_HEADS * IDX_DIM,), IDX_DIM ** -0.5, F32)])
    flag128 = jnp.concatenate([
        jnp.ones((DSA_HEADS * DSA_DK,), F32), kv_flag, jnp.ones((IDX_HEADS * IDX_DIM,), F32)])
    wv = jnp.concatenate([v_s, v_w], axis=0)
    small_n = GATE_COLS + IDX_HEADS
    wsmall = jnp.pad(jnp.concatenate([gate_a, w_i], axis=0), ((0, V7X_LANES - small_n), (0, 0)))
    wmisc = jnp.concatenate([wv, w_t[offs[8]:offs[9]]], axis=0)
    flag_misc = jnp.concatenate([jnp.zeros((wv.shape[0],), F32), jnp.ones((IDX_DIM,), F32)])
    wcs = jnp.concatenate([w_t[offs[1]:offs[2]], wsmall], axis=0)
    return w_t, offs, w192, scale192, scale128, flag128, wmisc, flag_misc, wcs


FFN_TN = 512


def _ffn_up_wide_kernel(h_ref, w1_hbm, w3_hbm, o_ref, stage_ref, wb_ref, sem, *, n_j, n_i):
    j, i = pl.program_id(0), pl.program_id(1)
    tn = o_ref.shape[1]
    slot = lax.rem(j, 2)

    def fetch(w_hbm, jj):
        return pltpu.make_async_copy(w_hbm.at[:, pl.ds(pl.multiple_of(jj * tn, tn), tn)], stage_ref, sem)

    @pl.when((j == 0) & (i == 0))
    def _():
        fetch(w1_hbm, 0).start()
        fetch(w1_hbm, 0).wait()
        wb_ref[0, :, :tn] = stage_ref[...].astype(BF16)
        fetch(w3_hbm, 0).start()
        fetch(w3_hbm, 0).wait()
        wb_ref[0, :, tn:] = stage_ref[...].astype(BF16)

    has_next = j + 1 < n_j

    @pl.when(has_next & (i == 0))
    def _():
        fetch(w1_hbm, j + 1).start()

    @pl.when(has_next & (i == n_i // 2 - 1))
    def _():
        fetch(w1_hbm, j + 1).wait()
        wb_ref[1 - slot, :, :tn] = stage_ref[...].astype(BF16)
        fetch(w3_hbm, j + 1).start()

    @pl.when(has_next & (i == n_i - 1))
    def _():
        fetch(w3_hbm, j + 1).wait()
        wb_ref[1 - slot, :, tn:] = stage_ref[...].astype(BF16)

    ab = jnp.dot(h_ref[...], wb_ref[slot], preferred_element_type=F32)
    o_ref[...] = (jax.nn.silu(ab[:, :tn]) * ab[:, tn:]).astype(o_ref.dtype)


def _ffn_up_wide(h, w1, w3, n_cols, cfg):
    t, d = h.shape
    tm, tn = _tile(cfg.seq, 1024), FFN_TN
    n_j, n_i = n_cols // tn, t // tm
    assert n_cols % tn == 0 and n_i >= 4 and n_i % 2 == 0
    est = (2 * _nbytes((tm, d), BF16) + _nbytes((d, tn), F32) + 2 * _nbytes((d, 2 * tn), BF16)
           + 2 * _nbytes((tm, tn), BF16) + 3 * _nbytes((tm, 2 * tn), F32) + (2 << 20))
    return pl.pallas_call(
        functools.partial(_ffn_up_wide_kernel, n_j=n_j, n_i=n_i),
        grid=(n_j, n_i),
        in_specs=[
            pl.BlockSpec((tm, d), lambda j, i: (i, 0)),
            pl.BlockSpec(memory_space=pl.ANY),
            pl.BlockSpec(memory_space=pl.ANY),
        ],
        out_specs=pl.BlockSpec((tm, tn), lambda j, i: (i, j)),
        out_shape=jax.ShapeDtypeStruct((t, n_cols), BF16),
        scratch_shapes=[pltpu.VMEM((d, tn), F32), pltpu.VMEM((2, d, 2 * tn), BF16), pltpu.SemaphoreType.DMA],
        compiler_params=_params(("arbitrary", "arbitrary"), est),
        name="ffn_up",
    )(h, w1, w3)


def _epi_residual_sum(accs, e_refs, cols, *, coef):
    return e_refs[0][:, cols] + (coef * e_refs[1][0][:, cols]) * sum(accs[1:], accs[0])


def _ffn(x2, gain, mod3, idx, w1, w3, w2, cfg):
    t, d = x2.shape
    ff = w1.shape[1]
    n_main = (ff // FFN_TN) * FFN_TN
    n_tail = ff - n_main
    w2b = w2.astype(BF16)
    if n_tail:
        assert n_main % n_tail == 0
        tail_w = jnp.concatenate([w1[:, n_main:], w3[:, n_main:]], axis=1).astype(BF16)
        h, u_tail = _norm_mod(x2, gain, mod3, idx, idx + 1, cfg, tail_w)
        us, w2s, row_blocks = [None, u_tail], [w2b, w2b], [(n_main, 0), (n_tail, n_main // n_tail)]
    else:
        h = _norm_mod(x2, gain, mod3, idx, idx + 1, cfg)
        us, w2s, row_blocks = [None], [w2b], [(n_main, 0)]
    us[0] = _ffn_up_wide(h, w1, w3, n_main, cfg)
    tm = _tile(cfg.seq, 512)
    nsb = cfg.seq // tm
    tn = _tile(d, 512)
    epi = [(x2, (tm, tn), lambda i, j: (i, j)),
           (mod3, (1, 1, tn), lambda i, j: ((i // nsb) * N_MOD + idx + 2, 0, j))]
    return _matmul(us, w2s, [(n, n) for n in range(len(us))], functools.partial(_epi_residual_sum, coef=0.5),
                   epi, d, F32, tm=tm, tn=tn, name="ffn_down", w_row_blocks=row_blocks)


def _mixer(x2, gain, mod3, w_in, cmp_pe_k, cmp_w1_k, cmp_w2_k, cmp_pe_v, cmp_w1_v, cmp_w2_v,
           proj_a, proj_b, w_out, cfg):
    b, s, d = cfg.batch, cfg.seq, cfg.d_model
    w_t, offs, w192, scale192, scale128, flag128, wmisc, flag_misc, wcs = _mixer_weights(w_in, d)
    h = _norm_mod(x2, gain, mod3, 3, 4, cfg)
    p192 = _project_rope(h, w192, 0, scale192, None, NSA_DK, NSA_DKP, cfg, "proj_rope192")
    p128 = _project_rope(h, w_t, int(offs[5]), scale128, flag128, DSA_DK, DSA_DK, cfg, "proj_rope128")
    n_misc = wmisc.shape[0]
    pm = _project_rope(h, wmisc, 0, jnp.ones((n_misc,), F32), flag_misc, IDX_DIM, IDX_DIM, cfg, "proj_misc")
    kidx_col = (n_misc - IDX_DIM) // IDX_DIM
    cmp_w = NSA_GROUPS * (NSA_DK + NSA_DV)
    pcs = _project(h, wcs, 0, wcs.shape[0], _epi_plain, F32, cfg, "proj_cmp_small")
    small_col = cmp_w // V7X_LANES
    gates_m = _project(h, w_t, int(offs[10]), 2 * d, _epi_sigmoid, BF16, cfg, "proj_merge")

    n_rows = s // CMP_STRIDE
    pc5 = pcs[:, :cmp_w].reshape(b, n_rows, CMP_STRIDE, NSA_GROUPS, NSA_DK + NSA_DV)

    def chunks(x5):
        return jnp.transpose(x5, (0, 3, 1, 2, 4)).reshape(b * NSA_GROUPS, n_rows, CMP_STRIDE * x5.shape[-1])

    cmp_pos = jnp.arange(n_rows) * CMP_STRIDE + CMP_BLOCK - 1
    k_cmp = _compress(chunks(pc5[..., :NSA_DK]), cmp_pe_k, cmp_w1_k, cmp_w2_k, NSA_DK, NSA_DKP, cmp_pos,
                      "compress_k").reshape(b, NSA_GROUPS, n_rows, NSA_DKP)
    v_cmp = _compress(chunks(pc5[..., NSA_DK:]), cmp_pe_v, cmp_w1_v, cmp_w2_v, NSA_DV, NSA_DV, None,
                      "compress_v").reshape(b, NSA_GROUPS, n_rows, NSA_DV)

    o_a = _nsa_attention(p192, pm, pcs, small_col, k_cmp, v_cmp, cfg)
    o_b = _dsa_attention(p128, pm, kidx_col, pcs, small_col, cfg)

    tm = _tile(s, 1024)
    tn = _tile(d, 1024)
    nsb = s // tm
    epi = [(gates_m, (tm, tn), lambda i, j: (i, j)), (gates_m, (tm, tn), lambda i, j: (i, j + d // tn))]
    y = _matmul([o_a, o_b], [proj_a.astype(BF16), proj_b.astype(BF16)], [(0, 0), (1, 1)], _epi_merge, epi,
                d, BF16, tm=tm, tn=tn, name="merge_proj")
    epi = [(x2, (tm, tn), lambda i, j: (i, j)),
           (mod3, (1, 1, tn), lambda i, j: ((i // nsb) * N_MOD + 5, 0, j))]
    return _matmul([y], [w_out.astype(BF16)], [(0, 0)], functools.partial(_epi_residual, coef=1.0), epi,
                   d, F32, tm=tm, tn=tn, name="out_proj")


def _layer(x, c, ada_w, ada_b, norm_ffn1, ffn1_w1, ffn1_w3, ffn1_w2, norm_mix, w_in,
           cmp_pe_k, cmp_w1_k, cmp_w2_k, cmp_pe_v, cmp_w1_v, cmp_w2_v, proj_a, proj_b, w_out,
           norm_ffn2, ffn2_w1, ffn2_w3, ffn2_w2, norm_final):
    b, s, d = x.shape
    cfg = Cfg(b, s, d, ffn1_w1.shape[-1])
    assert s % Q_BLOCK == 0 and s >= WINDOW + Q_BLOCK and d % V7X_LANES == 0
    x2 = x.reshape(b * s, d)
    c_pad = jnp.pad(c, ((0, 8 - b), (0, 0)))
    for l in range(ada_w.shape[0]):
        mod = _adaln(c_pad, ada_w[l], ada_b[l].reshape(1, -1))
        mod3 = mod[:b].reshape(b * N_MOD, 1, d)
        x2 = _ffn(x2, norm_ffn1[l], mod3, 0, ffn1_w1[l], ffn1_w3[l], ffn1_w2[l], cfg)
        x2 = _mixer(x2, norm_mix[l], mod3, w_in[l], cmp_pe_k[l], cmp_w1_k[l], cmp_w2_k[l],
                    cmp_pe_v[l], cmp_w1_v[l], cmp_w2_v[l], proj_a[l], proj_b[l], w_out[l], cfg)
        x2 = _ffn(x2, norm_ffn2[l], mod3, 6, ffn2_w1[l], ffn2_w3[l], ffn2_w2[l], cfg)
    return _final_norm(x2, norm_final).reshape(b, s, d)


def kernel(x, c, ada_w, ada_b, norm_ffn1, ffn1_w1, ffn1_w3, ffn1_w2, norm_mix, w_in, cmp_pe_k, cmp_w1_k, cmp_w2_k, cmp_pe_v, cmp_w1_v, cmp_w2_v, proj_a, proj_b, w_out, norm_ffn2, ffn2_w1, ffn2_w3, ffn2_w2, norm_final):
    return _layer(x, c, ada_w, ada_b, norm_ffn1, ffn1_w1, ffn1_w3, ffn1_w2, norm_mix, w_in,
                  cmp_pe_k, cmp_w1_k, cmp_w2_k, cmp_pe_v, cmp_w1_v, cmp_w2_v, proj_a, proj_b, w_out,
                  norm_ffn2, ffn2_w1, ffn2_w3, ffn2_w2, norm_final)
```
